```python
import jax, jax.numpy as jnp
from jax import lax
import numpy as np

D_MODEL = 1024
BATCH = 4
SEQ = 4096
DEPTH = 4

N_MIXERS = 2
ROPE_THETA = 500000.0
RMS_EPS = 1e-6
Q_BLOCK = 128
MLA_HEADS = 16
MLA_NOPE = 128
MLA_ROPE = 64
MLA_V = 128
MLA_QK = MLA_NOPE + MLA_ROPE
MLA_Q_LORA = 384
MLA_KV_LORA = 256
MLA_IN = MLA_Q_LORA + MLA_KV_LORA + MLA_ROPE + MLA_HEADS * MLA_V
NSA_HEADS = 16
NSA_GROUPS = 4
NSA_HPG = NSA_HEADS // NSA_GROUPS
NSA_DK = 96
NSA_DV = 64
NSA_ROT = NSA_DK // 4
CMP_LEN = 32
CMP_STRIDE = 16
SEL_LEN = 64
N_SELECT = 16
WINDOW = 512
NSA_Q_BLOCK = 64
SEL_FORCE = 1e4
NSA_SIZES = (NSA_HEADS * NSA_DK,
             NSA_GROUPS * NSA_DK, NSA_GROUPS * NSA_DV,
             NSA_GROUPS * NSA_DK, NSA_GROUPS * NSA_DV,
             NSA_GROUPS * NSA_DK, NSA_GROUPS * NSA_DV,
             3 * NSA_HEADS, NSA_HEADS * NSA_DV)
NSA_IN = sum(NSA_SIZES)

kernel_name = "hybrid_mla_nsa_gated_trunk"


def rms_norm(x, g):
    xf = x.astype(jnp.float32)
    y = xf * lax.rsqrt(jnp.mean(xf * xf, axis=-1, keepdims=True) + RMS_EPS)
    return (y * g.astype(jnp.float32)).astype(x.dtype)


def rope(x, pos, rot_dim):
    half = rot_dim // 2
    inv_freq = ROPE_THETA ** (-jnp.arange(half, dtype=jnp.float32) / half)
    ang = pos.astype(jnp.float32)[:, :, None] * inv_freq
    cos = jnp.cos(ang)[:, :, None, :]
    sin = jnp.sin(ang)[:, :, None, :]
    xf = x[..., :rot_dim].astype(jnp.float32)
    x1, x2 = xf[..., :half], xf[..., half:]
    rot = jnp.concatenate([x1 * cos - x2 * sin, x2 * cos + x1 * sin], axis=-1)
    return jnp.concatenate([rot.astype(x.dtype), x[..., rot_dim:]], axis=-1)


def masked_softmax(s, mask):
    s = jnp.where(mask, s, -jnp.inf)
    m = jnp.max(s, axis=-1, keepdims=True)
    m = jnp.where(jnp.isfinite(m), m, 0.0)
    e = jnp.exp(s - m)
    d = jnp.sum(e, axis=-1, keepdims=True)
    return e / jnp.where(d > 0, d, 1.0)


def causal_block_attention(q, k, v, scale):
    S = q.shape[1]
    outs = []
    for start in range(0, S, Q_BLOCK):
        end = min(start + Q_BLOCK, S)
        s = jnp.einsum('bqhd,bkhd->bhqk', q[:, start:end], k[:, :end],
                       preferred_element_type=jnp.float32) * scale
        mask = jnp.arange(start, end)[:, None] >= jnp.arange(end)[None, :]
        p = jax.nn.softmax(jnp.where(mask, s, -jnp.inf), axis=-1)
        outs.append(jnp.einsum('bhqk,bkhd->bqhd', p.astype(v.dtype), v[:, :end]))
    return jnp.concatenate(outs, axis=1)


def mla_mixer(h, pos, w_in, g_cq, w_uq, g_ckv, w_ukv, g_q, g_k, w_out):
    B, S, _ = h.shape
    proj = h @ w_in
    c_q, c_kv, k_pe, z = jnp.split(
        proj, [MLA_Q_LORA, MLA_Q_LORA + MLA_KV_LORA, MLA_Q_LORA + MLA_KV_LORA + MLA_ROPE], axis=-1)
    q = (rms_norm(c_q, g_cq) @ w_uq).reshape(B, S, MLA_HEADS, MLA_QK)
    kv = (rms_norm(c_kv, g_ckv) @ w_ukv).reshape(B, S, MLA_HEADS, MLA_NOPE + MLA_V)
    k_nope, v = kv[..., :MLA_NOPE], kv[..., MLA_NOPE:]
    k_pe = jnp.broadcast_to(k_pe[:, :, None, :], (B, S, MLA_HEADS, MLA_ROPE))
    k = jnp.concatenate([k_pe, k_nope], axis=-1)
    q = rope(rms_norm(q, g_q), pos, MLA_ROPE)
    k = rope(rms_norm(k, g_k), pos, MLA_ROPE)
    o = causal_block_attention(q, k, v, MLA_QK ** -0.5)
    o = o.reshape(B, S, MLA_HEADS * MLA_V) * jax.nn.silu(z)
    return o @ w_out


def compress(t, tok_idx, pe, w1, w2):
    B, _, G, d = t.shape
    n, L = tok_idx.shape
    blk = t[:, tok_idx] + pe[:, None, :]
    blk = jnp.moveaxis(blk, 3, 2).reshape(B, n, G, L * d)
    return jax.nn.silu(blk @ w1) @ w2


def nsa_mixer(h, pos, w_in, g_q, g_k, pe_k, w1_k, w2_k, pe_v, w1_v, w2_v, w_out):
    B, S, _ = h.shape
    G, HPG, DK, DV = NSA_GROUPS, NSA_HPG, NSA_DK, NSA_DV
    splits = [int(c) for c in np.cumsum(NSA_SIZES)[:-1]]
    q, kc, vc, ks, vs, kw, vw, gl, z = jnp.split(h @ w_in, splits, axis=-1)
    q = rope(rms_norm(q.reshape(B, S, NSA_HEADS, DK), g_q), pos, NSA_ROT)
    ks = rope(rms_norm(ks.reshape(B, S, G, DK), g_k[1]), pos, NSA_ROT)
    kw = rope(rms_norm(kw.reshape(B, S, G, DK), g_k[2]), pos, NSA_ROT)
    vs = vs.reshape(B, S, G, DV)
    vw = vw.reshape(B, S, G, DV)
    gates = jax.nn.sigmoid(gl).reshape(B, S, NSA_HEADS, 3)
    scale = DK ** -0.5

    n_cmp = (S - CMP_LEN) // CMP_STRIDE + 1
    cmp_start = jnp.arange(n_cmp) * CMP_STRIDE
    tok_idx = cmp_start[:, None] + jnp.arange(CMP_LEN)[None, :]
    cmp_end = cmp_start + CMP_LEN - 1
    k_cmp = compress(kc.reshape(B, S, G, DK), tok_idx, pe_k, w1_k, w2_k)
    k_cmp = rope(rms_norm(k_cmp, g_k[0]), pos[:, cmp_end], NSA_ROT)
    v_cmp = compress(vc.reshape(B, S, G, DV), tok_idx, pe_v, w1_v, w2_v)

    n_sel_blk = S // SEL_LEN
    k_sel = min(N_SELECT, n_sel_blk)
    ks_blk = ks.reshape(B, n_sel_blk, SEL_LEN, G, DK).transpose(0, 3, 1, 2, 4)
    vs_blk = vs.reshape(B, n_sel_blk, SEL_LEN, G, DV).transpose(0, 3, 1, 2, 4)
    sel_start = jnp.arange(n_sel_blk) * SEL_LEN
    overlap = ((cmp_start[:, None] < sel_start[None, :] + SEL_LEN)
               & (cmp_start[:, None] + CMP_LEN > sel_start[None, :])).astype(jnp.float32)
    blk_ids = jnp.arange(n_sel_blk)
    bi = jnp.arange(B)[:, None, None, None]
    gi = jnp.arange(G)[None, :, None, None]

    kw_pad = jnp.pad(kw, ((0, 0), (WINDOW, 0), (0, 0), (0, 0)))
    vw_pad = jnp.pad(vw, ((0, 0), (WINDOW, 0), (0, 0), (0, 0)))

    Tq = NSA_Q_BLOCK
    n_chunks = S // Tq
    q_ch = jnp.moveaxis(q.reshape(B, n_chunks, Tq, G, HPG, DK), 1, 0)
    g_ch = jnp.moveaxis(gates.reshape(B, n_chunks, Tq, G, HPG, 3), 1, 0)
    starts = jnp.arange(n_chunks, dtype=jnp.int32) * Tq

    def chunk(args):
        qc, gc, start = args
        t = start + jnp.arange(Tq)
        s_c = jnp.einsum('btghd,bngd->bghtn', qc, k_cmp,
                         preferred_element_type=jnp.float32) * scale
        p_c = masked_softmax(s_c, cmp_end[None, :] <= t[:, None])
        o_c = jnp.einsum('bghtn,bngd->btghd', p_c.astype(v_cmp.dtype), v_cmp)
        imp = jnp.einsum('bgtn,nk->bgtk', p_c.sum(axis=2), overlap)
        cur = t // SEL_LEN
        valid = sel_start[None, :] <= t[:, None]
        forced = ((blk_ids[None, :] == 0) | (blk_ids[None, :] == cur[:, None])
                  | (blk_ids[None, :] == cur[:, None] - 1))
        score = jnp.where(forced, SEL_FORCE, jnp.where(valid, imp, -1.0))
        _, sel = lax.top_k(score, k_sel)
        k_g = ks_blk[bi, gi, sel].reshape(B, G, Tq, k_sel * SEL_LEN, DK)
        v_g = vs_blk[bi, gi, sel].reshape(B, G, Tq, k_sel * SEL_LEN, DV)
        key_tok = (sel[..., None] * SEL_LEN + jnp.arange(SEL_LEN)).reshape(B, G, Tq, k_sel * SEL_LEN)
        s_s = jnp.einsum('btghd,bgtkd->bghtk', qc, k_g,
                         preferred_element_type=jnp.float32) * scale
        p_s = masked_softmax(s_s, (key_tok <= t[None, None, :, None])[:, :, None])
        o_s = jnp.einsum('bghtk,bgtkd->btghd', p_s.astype(v_g.dtype), v_g)
        kb = lax.dynamic_slice_in_dim(kw_pad, start, WINDOW + Tq, axis=1)
        vb = lax.dynamic_slice_in_dim(vw_pad, start, WINDOW + Tq, axis=1)
        kt = start - WINDOW + jnp.arange(WINDOW + Tq)
        m_w = (kt[None, :] <= t[:, None]) & (kt[None, :] > t[:, None] - WINDOW) & (kt[None, :] >= 0)
        s_w = jnp.einsum('btghd,bkgd->bghtk', qc, kb,
                         preferred_element_type=jnp.float32) * scale
        p_w = masked_softmax(s_w, m_w)
        o_w = jnp.einsum('bghtk,bkgd->btghd', p_w.astype(vb.dtype), vb)
        return gc[..., 0:1] * o_c + gc[..., 1:2] * o_s + gc[..., 2:3] * o_w

    o = lax.map(chunk, (q_ch, g_ch, starts))
    o = jnp.moveaxis(o, 0, 1).reshape(B, S, NSA_HEADS * DV) * jax.nn.silu(z)
    return o @ w_out


def setup_inputs(seed: int = 0) -> dict:
    key = jax.random.key(seed)
    k = jax.random.split(key, 24)
    n_mla = (DEPTH + N_MIXERS - 1) // N_MIXERS
    n_nsa = (DEPTH + N_MIXERS - 2) // N_MIXERS

    def dense(kk, shape):
        return jax.random.normal(kk, shape, jnp.float32) * shape[-2] ** -0.5

    def gain(kk, shape):
        return 1.0 + 0.05 * jax.random.normal(kk, shape, jnp.float32)

    x = jax.random.normal(k[0], (BATCH, SEQ, D_MODEL), jnp.float32)
    offsets = jax.random.randint(k[1], (BATCH, 1), 0, 1024, dtype=jnp.int32)
    positions = offsets + jnp.arange(SEQ, dtype=jnp.int32)[None, :]
    return {
        "x": x,
        "positions": positions,
        "norm_g": gain(k[2], (DEPTH, D_MODEL)),
        "mla_w_in": dense(k[3], (n_mla, D_MODEL, MLA_IN)),
        "mla_g_cq": gain(k[4], (n_mla, MLA_Q_LORA)),
        "mla_w_uq": dense(k[5], (n_mla, MLA_Q_LORA, MLA_HEADS * MLA_QK)),
        "mla_g_ckv": gain(k[6], (n_mla, MLA_KV_LORA)),
        "mla_w_ukv": dense(k[7], (n_mla, MLA_KV_LORA, MLA_HEADS * (MLA_NOPE + MLA_V))),
        "mla_g_q": gain(k[8], (n_mla, MLA_QK)),
        "mla_g_k": gain(k[9], (n_mla, MLA_QK)),
        "mla_w_out": dense(k[10], (n_mla, MLA_HEADS * MLA_V, D_MODEL)),
        "nsa_w_in": dense(k[11], (n_nsa, D_MODEL, NSA_IN)),
        "nsa_g_q": gain(k[12], (n_nsa, NSA_DK)),
        "nsa_g_k": gain(k[13], (n_nsa, 3, NSA_DK)),
        "nsa_pe_k": 0.02 * jax.random.normal(k[14], (n_nsa, CMP_LEN, NSA_DK), jnp.float32),
        "nsa_w1_k": dense(k[15], (n_nsa, CMP_LEN * NSA_DK, NSA_DK)),
        "nsa_w2_k": dense(k[16], (n_nsa, NSA_DK, NSA_DK)),
        "nsa_pe_v": 0.02 * jax.random.normal(k[17], (n_nsa, CMP_LEN, NSA_DV), jnp.float32),
        "nsa_w1_v": dense(k[18], (n_nsa, CMP_LEN * NSA_DV, NSA_DV)),
        "nsa_w2_v": dense(k[19], (n_nsa, NSA_DV, NSA_DV)),
        "nsa_w_out": dense(k[20], (n_nsa, NSA_HEADS * NSA_DV, D_MODEL)),
    }


def reference(x, positions, norm_g, mla_w_in, mla_g_cq, mla_w_uq, mla_g_ckv, mla_w_ukv,
              mla_g_q, mla_g_k, mla_w_out, nsa_w_in, nsa_g_q, nsa_g_k, nsa_pe_k, nsa_w1_k,
              nsa_w2_k, nsa_pe_v, nsa_w1_v, nsa_w2_v, nsa_w_out):
    for i in range(DEPTH):
        h = rms_norm(x, norm_g[i])
        j = i // N_MIXERS
        if i % N_MIXERS == 0:
            y = mla_mixer(h, positions, mla_w_in[j], mla_g_cq[j], mla_w_uq[j], mla_g_ckv[j],
                          mla_w_ukv[j], mla_g_q[j], mla_g_k[j], mla_w_out[j])
        else:
            y = nsa_mixer(h, positions, nsa_w_in[j], nsa_g_q[j], nsa_g_k[j], nsa_pe_k[j],
                          nsa_w1_k[j], nsa_w2_k[j], nsa_pe_v[j], nsa_w1_v[j], nsa_w2_v[j],
                          nsa_w_out[j])
        x = x + y
    return x
```

```python
import functools

import numpy as np
import jax
import jax.numpy as jnp
from jax import lax
from jax.experimental import pallas as pl
from jax.experimental.pallas import tpu as pltpu

F32 = jnp.float32
BF16 = jnp.bfloat16

D_MODEL = 1024
DEPTH = 4
N_MIXERS = 2
ROPE_THETA = 500000.0
RMS_EPS = 1e-6
MLA_HEADS = 16
MLA_NOPE = 128
MLA_ROPE = 64
MLA_V = 128
MLA_QK = MLA_NOPE + MLA_ROPE
MLA_Q_LORA = 384
MLA_KV_LORA = 256
MLA_HD = 256
NSA_HEADS = 16
NSA_GROUPS = 4
NSA_HPG = NSA_HEADS // NSA_GROUPS
NSA_DK = 96
NSA_DV = 64
NSA_ROT = NSA_DK // 4
NSA_HALF = NSA_ROT // 2
NSA_HD = 128
CMP_LEN = 32
CMP_STRIDE = 16
SEL_LEN = 64
N_SELECT = 16
WINDOW = 512
SEL_FORCE = 1e4
NSA_SIZES = (NSA_HEADS * NSA_DK,
             NSA_GROUPS * NSA_DK, NSA_GROUPS * NSA_DV,
             NSA_GROUPS * NSA_DK, NSA_GROUPS * NSA_DV,
             NSA_GROUPS * NSA_DK, NSA_GROUPS * NSA_DV,
             3 * NSA_HEADS, NSA_HEADS * NSA_DV)

LANE = 128
TT = 256
TKV = 256
TQ_MLA = 256
TQ_NSA = 128
NEG = -30000.0
M_INIT = -1e30
VMEM_LIMIT = 56 * 1024 * 1024


def _dot(a, b):
    return jnp.dot(a, b, preferred_element_type=F32)


def _dot_nt(a, b):
    return lax.dot_general(a, b, (((1,), (1,)), ((), ())), preferred_element_type=F32)


def _sigmoid(x):
    return 1.0 / (1.0 + jnp.exp(-x))


def _rms_rows(x, g):
    ms = jnp.mean(x * x, axis=-1, keepdims=True)
    return x * lax.rsqrt(ms + RMS_EPS) * g


def _params(sem):
    return pltpu.CompilerParams(dimension_semantics=sem, vmem_limit_bytes=VMEM_LIMIT)


def _rope_kernel(pos_ref, invf_ref, cos_ref, sin_ref):
    ang = invf_ref[...] * pos_ref[...]
    cos_ref[...] = jnp.cos(ang)
    sin_ref[...] = jnp.sin(ang)


def _rope_tables(positions):
    n = positions.size
    tn = 512
    pos = positions.reshape(1, n).astype(F32)
    half_m = MLA_ROPE // 2
    inv_m = ROPE_THETA ** (-jnp.arange(half_m, dtype=F32) / half_m)
    inv_n = ROPE_THETA ** (-jnp.arange(NSA_HALF, dtype=F32) / NSA_HALF)
    invf = jnp.concatenate([inv_m, inv_n, jnp.zeros((16 - NSA_HALF,), F32)])
    rows = invf.shape[0]
    invf = jnp.broadcast_to(invf[:, None], (rows, tn))
    cos_t, sin_t = pl.pallas_call(
        _rope_kernel,
        grid=(n // tn,),
        in_specs=[pl.BlockSpec((1, tn), lambda i: (0, i)),
                  pl.BlockSpec((rows, tn), lambda i: (0, 0))],
        out_specs=[pl.BlockSpec((rows, tn), lambda i: (0, i)),
                   pl.BlockSpec((rows, tn), lambda i: (0, i))],
        out_shape=[jax.ShapeDtypeStruct((rows, n), F32)] * 2,
        compiler_params=_params(("parallel",)),
        name="rope_tables",
    )(pos, invf)
    return cos_t, sin_t


def _mla_in_kernel(x_ref, ng_ref, wa_ref, wzT_ref, gcq_ref, gckv_ref, wuqT_ref, gq_ref,
                   wuk_ref, wuvT_ref, gkn_ref, gkp_ref, ct_ref, st_ref, cosT_ref, sinT_ref,
                   qT_ref, k_ref, vT_ref, sz_ref):
    x = x_ref[0]
    h = _rms_rows(x, ng_ref[...]).astype(BF16)
    pa = _dot(h, wa_ref[...])
    z = _dot_nt(wzT_ref[...], h)
    sz_ref[0] = z * _sigmoid(z)

    cqn = _rms_rows(pa[:, :MLA_Q_LORA], gcq_ref[...]).astype(BF16)
    ckvn = _rms_rows(pa[:, MLA_Q_LORA:MLA_Q_LORA + MLA_KV_LORA], gckv_ref[...]).astype(BF16)
    kpe = pa[:, MLA_Q_LORA + MLA_KV_LORA:]

    cos = cosT_ref[...]
    sin = sinT_ref[...]
    gq = gq_ref[...]
    zeros32 = jnp.zeros((32, x.shape[0]), BF16)
    for hd in range(MLA_HEADS):
        blk = _dot_nt(wuqT_ref[hd * MLA_HD:(hd + 1) * MLA_HD, :], cqn)
        ss = jnp.sum(blk * blk, axis=0, keepdims=True)
        qn = blk * lax.rsqrt(ss * (1.0 / MLA_QK) + RMS_EPS) * gq
        x1 = qn[128:160]
        x2 = qn[192:224]
        qT_ref[0, hd, 0:128, :] = qn[0:128].astype(BF16)
        qT_ref[0, hd, 128:160, :] = (x1 * cos - x2 * sin).astype(BF16)
        qT_ref[0, hd, 160:192, :] = zeros32
        qT_ref[0, hd, 192:224, :] = (x2 * cos + x1 * sin).astype(BF16)
        qT_ref[0, hd, 224:256, :] = zeros32

    kn = _dot(ckvn, wuk_ref[...])
    ss_pe = jnp.sum(kpe * kpe, axis=-1, keepdims=True)
    kpg = kpe * gkp_ref[...]
    prot = kpg * ct_ref[...] + pltpu.roll(kpg, 64, axis=1) * st_ref[...]
    gkn = gkn_ref[...]
    for hd in range(MLA_HEADS):
        kb = kn[:, hd * MLA_NOPE:(hd + 1) * MLA_NOPE]
        ss = jnp.sum(kb * kb, axis=-1, keepdims=True) + ss_pe
        r = lax.rsqrt(ss * (1.0 / MLA_QK) + RMS_EPS)
        k_ref[0, hd, :, 0:128] = (kb * r * gkn).astype(BF16)
        k_ref[0, hd, :, 128:256] = (prot * r).astype(BF16)

    for hd in range(MLA_HEADS):
        v = _dot_nt(wuvT_ref[hd * MLA_V:(hd + 1) * MLA_V, :], ckvn)
        vT_ref[0, hd, 0] = v.astype(BF16)


def _mla_in(x, ng, w, tabs):
    B, S, D = x.shape
    nt = S // TT
    full = lambda shape: pl.BlockSpec(shape, lambda b, t: (0,) * len(shape))
    tok = lambda b, t: (b * nt + t, 0)
    feat = lambda b, t: (0, b * nt + t)
    outs = pl.pallas_call(
        _mla_in_kernel,
        grid=(B, nt),
        in_specs=[
            pl.BlockSpec((1, TT, D), lambda b, t: (b, t, 0)),
            full((1, D)),
            full(w["wa"].shape), full(w["wzT"].shape),
            full((1, MLA_Q_LORA)), full((1, MLA_KV_LORA)),
            full(w["wuqT"].shape), full((MLA_HD, TT)),
            full(w["wuk"].shape), full(w["wuvT"].shape),
            full((1, MLA_NOPE)), full((1, LANE)),
            pl.BlockSpec((TT, LANE), tok), pl.BlockSpec((TT, LANE), tok),
            pl.BlockSpec((32, TT), feat), pl.BlockSpec((32, TT), feat),
        ],
        out_specs=[
            pl.BlockSpec((1, MLA_HEADS, MLA_HD, TT), lambda b, t: (b, 0, 0, t)),
            pl.BlockSpec((1, MLA_HEADS, TT, MLA_HD), lambda b, t: (b, 0, t, 0)),
            pl.BlockSpec((1, MLA_HEADS, 1, MLA_V, TT), lambda b, t: (b, 0, t, 0, 0)),
            pl.BlockSpec((1, MLA_HEADS * MLA_V, TT), lambda b, t: (b, 0, t)),
        ],
        out_shape=[
            jax.ShapeDtypeStruct((B, MLA_HEADS, MLA_HD, S), BF16),
            jax.ShapeDtypeStruct((B, MLA_HEADS, S, MLA_HD), BF16),
            jax.ShapeDtypeStruct((B, MLA_HEADS, nt, MLA_V, TT), BF16),
            jax.ShapeDtypeStruct((B, MLA_HEADS * MLA_V, S), F32),
        ],
        compiler_params=_params(("parallel", "parallel")),
        name="mla_in_proj",
    )(x, ng, w["wa"], w["wzT"], w["gcq"], w["gckv"], w["wuqT"], w["gq"], w["wuk"],
      w["wuvT"], w["gkn"], w["gkp"], tabs["ct_mla"], tabs["st_mla"], tabs["cosT_mla"],
      tabs["sinT_mla"])
    return outs


def _mla_attn_kernel(qT_ref, k_ref, vT_ref, sz_ref, o_ref):
    i = pl.program_id(2)
    q = qT_ref[0, 0]
    scale = MLA_QK ** -0.5
    tq = q.shape[1]

    def step(j, carry, diag):
        m, l, acc = carry
        kt = k_ref[0, 0, pl.ds(pl.multiple_of(j * TKV, TKV), TKV), :]
        s = _dot(kt, q) * scale
        if diag:
            row = lax.broadcasted_iota(jnp.int32, s.shape, 0)
            col = lax.broadcasted_iota(jnp.int32, s.shape, 1)
            s = jnp.where(row <= col, s, NEG)
        m_new = jnp.maximum(m, jnp.max(s, axis=0, keepdims=True))
        alpha = jnp.exp(m - m_new)
        p = jnp.exp(s - m_new)
        l = alpha * l + jnp.sum(p, axis=0, keepdims=True)
        acc = alpha * acc + _dot(vT_ref[0, 0, j], p.astype(BF16))
        return m_new, l, acc

    init = (jnp.full((1, tq), M_INIT, F32), jnp.zeros((1, tq), F32),
            jnp.zeros((MLA_V, tq), F32))
    carry = lax.fori_loop(0, i, lambda j, c: step(j, c, False), init)
    m, l, acc = step(i, carry, True)
    o_ref[0] = ((acc / l) * sz_ref[0]).astype(BF16)


def _mla_attn(qT, k, vT, sz):
    B, H, _, S = qT.shape
    assert TQ_MLA == TKV
    nq = S // TQ_MLA
    return pl.pallas_call(
        _mla_attn_kernel,
        grid=(B, H, nq),
        in_specs=[
            pl.BlockSpec((1, 1, MLA_HD, TQ_MLA), lambda b, h, i: (b, h, 0, i)),
            pl.BlockSpec((1, 1, S, MLA_HD), lambda b, h, i: (b, h, 0, 0)),
            pl.BlockSpec((1, 1, S // TKV, MLA_V, TKV), lambda b, h, i: (b, h, 0, 0, 0)),
            pl.BlockSpec((1, MLA_V, TQ_MLA), lambda b, h, i: (b, h, i)),
        ],
        out_specs=pl.BlockSpec((1, MLA_V, TQ_MLA), lambda b, h, i: (b, h, i)),
        out_shape=jax.ShapeDtypeStruct((B, H * MLA_V, S), BF16),
        compiler_params=_params(("parallel", "parallel", "arbitrary")),
        name="mla_attention",
    )(qT, k, vT, sz)


def _out_proj_kernel(og_ref, wT_ref, x_ref, o_ref):
    yT = _dot(wT_ref[...], og_ref[0])
    o_ref[0] = x_ref[0] + yT.T


def _out_proj(ogT, wT, x):
    B, S, D = x.shape
    K = ogT.shape[1]
    return pl.pallas_call(
        _out_proj_kernel,
        grid=(B, S // TT),
        in_specs=[
            pl.BlockSpec((1, K, TT), lambda b, t: (b, 0, t)),
            pl.BlockSpec((D, K), lambda b, t: (0, 0)),
            pl.BlockSpec((1, TT, D), lambda b, t: (b, t, 0)),
        ],
        out_specs=pl.BlockSpec((1, TT, D), lambda b, t: (b, t, 0)),
        out_shape=jax.ShapeDtypeStruct((B, S, D), F32),
        compiler_params=_params(("parallel", "parallel")),
        name="out_proj",
    )(ogT, wT, x)


def _nsa_in_kernel(x_ref, ng_ref, wtok_ref, wfT_ref, gq_ref, gk_ref, ct_ref, st_ref,
                   cosT_ref, sinT_ref,
                   qT_ref, ks_ref, kw_ref, kc_ref, vc_ref, vsT_ref, vwT_ref, g_ref, sz_ref):
    t = pl.program_id(1)
    x = x_ref[0]
    n = x.shape[0]
    h = _rms_rows(x, ng_ref[...]).astype(BF16)
    G = NSA_GROUPS

    cos = cosT_ref[...]
    sin = sinT_ref[...]
    gq = gq_ref[...]
    for hd in range(NSA_HEADS):
        blk = _dot_nt(wfT_ref[hd * NSA_HD:(hd + 1) * NSA_HD, :], h)
        ss = jnp.sum(blk * blk, axis=0, keepdims=True)
        qn = blk * lax.rsqrt(ss * (1.0 / NSA_DK) + RMS_EPS) * gq
        x1 = qn[0:16]
        x2 = qn[64:80]
        qT_ref[0, hd, 0:16, :] = (x1 * cos - x2 * sin).astype(BF16)
        qT_ref[0, hd, 16:64, :] = qn[16:64].astype(BF16)
        qT_ref[0, hd, 64:80, :] = (x2 * cos + x1 * sin).astype(BF16)
        qT_ref[0, hd, 80:128, :] = qn[80:128].astype(BF16)
    off = NSA_HEADS * NSA_HD
    for g in range(G):
        v = _dot_nt(wfT_ref[off + g * NSA_DV:off + (g + 1) * NSA_DV, :], h)
        vsT_ref[0, g, 0] = v.astype(BF16)
    off += G * NSA_DV
    for g in range(G):
        v = _dot_nt(wfT_ref[off + g * NSA_DV:off + (g + 1) * NSA_DV, :], h)
        vwT_ref[0, g, 0] = v.astype(BF16)
    off += G * NSA_DV
    gl = _dot_nt(wfT_ref[off:off + 16 * G, :], h)
    sg = _sigmoid(gl)
    for g in range(G):
        g_ref[0, g] = sg[g * 16:(g + 1) * 16]
    off += 16 * G
    z = _dot_nt(wfT_ref[off:off + NSA_HEADS * NSA_DV, :], h)
    sz_ref[0] = z * _sigmoid(z)

    pt = _dot(h, wtok_ref[...])
    ct = ct_ref[...]
    st = st_ref[...]
    nblk = ks_ref.shape[-1] - NSA_HD
    row = lax.broadcasted_iota(jnp.int32, (n, nblk), 0) + t * n
    col = lax.broadcasted_iota(jnp.int32, (n, nblk), 1)
    onehot = jnp.where((row // SEL_LEN) == col, 1.0, 0.0).astype(BF16)
    for br, ref in ((0, ks_ref), (1, kw_ref)):
        gk = gk_ref[br + 1:br + 2, :]
        for g in range(G):
            kb = pt[:, (br * G + g) * NSA_HD:(br * G + g + 1) * NSA_HD]
            ss = jnp.sum(kb * kb, axis=-1, keepdims=True)
            kn = kb * lax.rsqrt(ss * (1.0 / NSA_DK) + RMS_EPS) * gk
            kr = kn * ct + pltpu.roll(kn, 64, axis=1) * st
            ref[0, g, :, 0:NSA_HD] = kr.astype(BF16)
            if br == 0:
                ref[0, g, :, NSA_HD:NSA_HD + nblk] = onehot
    for g in range(G):
        kc_ref[0, g] = pt[:, (2 * G + g) * LANE:(2 * G + g + 1) * LANE]
        vc_ref[0, g] = pt[:, (3 * G + g) * LANE:(3 * G + g + 1) * LANE]


def _nsa_in(x, ng, w, tabs):
    B, S, D = x.shape
    nt = S // TT
    G = NSA_GROUPS
    nblk = S // SEL_LEN
    full = lambda shape: pl.BlockSpec(shape, lambda b, t: (0,) * len(shape))
    tok = lambda b, t: (b * nt + t, 0)
    feat = lambda b, t: (0, b * nt + t)
    tokmaj = lambda width: pl.BlockSpec((1, G, TT, width), lambda b, t: (b, 0, t, 0))
    vtile = pl.BlockSpec((1, G, 1, NSA_DV, TT), lambda b, t: (b, 0, t, 0, 0))
    return pl.pallas_call(
        _nsa_in_kernel,
        grid=(B, nt),
        in_specs=[
            pl.BlockSpec((1, TT, D), lambda b, t: (b, t, 0)),
            full((1, D)),
            full(w["wtok"].shape), full(w["wfT"].shape),
            full((NSA_HD, TT)), full((3, NSA_HD)),
            pl.BlockSpec((TT, LANE), tok), pl.BlockSpec((TT, LANE), tok),
            pl.BlockSpec((16, TT), feat), pl.BlockSpec((16, TT), feat),
        ],
        out_specs=[
            pl.BlockSpec((1, NSA_HEADS, NSA_HD, TT), lambda b, t: (b, 0, 0, t)),
            tokmaj(NSA_HD + nblk), tokmaj(NSA_HD), tokmaj(LANE), tokmaj(LANE),
            vtile, vtile,
            pl.BlockSpec((1, G, 16, TT), lambda b, t: (b, 0, 0, t)),
            pl.BlockSpec((1, NSA_HEADS * NSA_DV, TT), lambda b, t: (b, 0, t)),
        ],
        out_shape=[
            jax.ShapeDtypeStruct((B, NSA_HEADS, NSA_HD, S), BF16),
            jax.ShapeDtypeStruct((B, G, S, NSA_HD + nblk), BF16),
            jax.ShapeDtypeStruct((B, G, S, NSA_HD), BF16),
            jax.ShapeDtypeStruct((B, G, S, LANE), F32),
            jax.ShapeDtypeStruct((B, G, S, LANE), F32),
            jax.ShapeDtypeStruct((B, G, nt, NSA_DV, TT), BF16),
            jax.ShapeDtypeStruct((B, G, nt, NSA_DV, TT), BF16),
            jax.ShapeDtypeStruct((B, G, 16, S), F32),
            jax.ShapeDtypeStruct((B, NSA_HEADS * NSA_DV, S), F32),
        ],
        compiler_params=_params(("parallel", "parallel")),
        name="nsa_in_proj",
    )(x, ng, w["wtok"], w["wfT"], w["gq"], w["gk"], tabs["ct_nsa"], tabs["st_nsa"],
      tabs["cosT_nsa"], tabs["sinT_nsa"])


def _nsa_cmp_kernel(kc_ref, vc_ref, pek_ref, pev_ref, w1k_ref, w2k_ref, w1v_ref, w2vT_ref,
                    gk_ref, ct_ref, st_ref, kcmp_ref, vcmpT_ref):
    nrow = kc_ref.shape[2]

    def pre(x, pe_ref, w1_ref):
        lo = _dot((x + pe_ref[0:1, :]).astype(BF16), w1_ref[0])
        hi = _dot((x + pe_ref[1:2, :]).astype(BF16), w1_ref[1])
        return lo + pltpu.roll(hi, nrow - 1, axis=0)

    a = pre(kc_ref[0, 0], pek_ref, w1k_ref)
    kc = _dot((a * _sigmoid(a)).astype(BF16), w2k_ref[...])
    ss = jnp.sum(kc * kc, axis=-1, keepdims=True)
    kn = kc * lax.rsqrt(ss * (1.0 / NSA_DK) + RMS_EPS) * gk_ref[0:1, :]
    kr = kn * ct_ref[0] + pltpu.roll(kn, 64, axis=1) * st_ref[0]
    kcmp_ref[0, 0] = kr.astype(BF16)

    a = pre(vc_ref[0, 0], pev_ref, w1v_ref)
    sv = (a * _sigmoid(a)).astype(BF16)
    vcmpT_ref[0, 0] = _dot_nt(w2vT_ref[...], sv).astype(BF16)


def _nsa_cmp(kc16, vc16, w, tabs):
    B, G, nrow, width = kc16.shape
    full = lambda shape: pl.BlockSpec(shape, lambda b, g: (0,) * len(shape))
    blk = pl.BlockSpec((1, 1, nrow, width), lambda b, g: (b, g, 0, 0))
    tab = pl.BlockSpec((1, nrow, LANE), lambda b, g: (b, 0, 0))
    return pl.pallas_call(
        _nsa_cmp_kernel,
        grid=(B, G),
        in_specs=[blk, blk, full((2, width)), full((2, width)),
                  full(w["w1k"].shape), full(w["w2k"].shape),
                  full(w["w1v"].shape), full(w["w2vT"].shape),
                  full((3, NSA_HD)), tab, tab],
        out_specs=[pl.BlockSpec((1, 1, nrow, NSA_HD), lambda b, g: (b, g, 0, 0)),
                   pl.BlockSpec((1, 1, NSA_DV, nrow), lambda b, g: (b, g, 0, 0))],
        out_shape=[jax.ShapeDtypeStruct((B, G, nrow, NSA_HD), BF16),
                   jax.ShapeDtypeStruct((B, G, NSA_DV, nrow), BF16)],
        compiler_params=_params(("parallel", "parallel")),
        name="nsa_compress",
    )(kc16, vc16, w["pek"], w["pev"], w["w1k"], w["w2k"], w["w1v"], w["w2vT"], w["gk"],
      tabs["ct_cmp"], tabs["st_cmp"])


def _tile4(a):
    return jnp.concatenate([a] * NSA_HPG, axis=1)


def _nsa_attn_kernel(qT_ref, ks_ref, vsT_ref, kw_ref, vwT_ref, kc_ref, vcT_ref, ovT_ref,
                     g_ref, sz_ref, o_ref, qaug_ref, *, k_sel, n_win_tiles):
    i = pl.program_id(2)
    tq = TQ_NSA
    q0 = i * tq
    scale = NSA_DK ** -0.5
    ncmp = kc_ref.shape[2]

    for hh in range(NSA_HPG):
        qaug_ref[0:NSA_HD, hh * tq:(hh + 1) * tq] = qT_ref[0, hh]
    qT = qaug_ref[0:NSA_HD, :]
    tok = q0 + lax.broadcasted_iota(jnp.int32, (1, tq), 1)

    sc = _dot(kc_ref[0, 0], qT) * scale
    nidx = lax.broadcasted_iota(jnp.int32, (ncmp, tq), 0)
    validc = (nidx * CMP_STRIDE + (CMP_LEN - 1)) <= tok
    biasc = _tile4(jnp.where(validc, 0.0, NEG))
    validf = _tile4(jnp.where(validc, 1.0, 0.0))
    sc = sc + biasc
    mc = jnp.max(sc, axis=0, keepdims=True)
    ec = jnp.exp(sc - mc) * validf
    dc = jnp.sum(ec, axis=0, keepdims=True)
    pc = ec * (1.0 / jnp.where(dc > 0, dc, 1.0))
    o_c = _dot(vcT_ref[0, 0], pc.astype(BF16))
    psum = pc[:, 0:tq]
    for hh in range(1, NSA_HPG):
        psum = psum + pc[:, hh * tq:(hh + 1) * tq]
    imp = _dot(ovT_ref[...], psum.astype(BF16))

    nblk = imp.shape[0]
    bidx = lax.broadcasted_iota(jnp.int32, (nblk, tq), 0)
    cur = tok // SEL_LEN
    forced = (bidx == 0) | (bidx == cur) | (bidx == cur - 1)
    valid = (bidx * SEL_LEN) <= tok
    score = jnp.where(forced, SEL_FORCE, jnp.where(valid, imp, -1.0))
    cnt = jnp.zeros((nblk, tq), F32)
    for j in range(nblk):
        rowj = score[j:j + 1, :]
        cnt = cnt + jnp.where(rowj > score, 1.0, 0.0)
        cnt = cnt + jnp.where(rowj == score, jnp.where(bidx > j, 1.0, 0.0), 0.0)
    csel = jnp.where(cnt < k_sel, 0.0, NEG).astype(BF16)
    for hh in range(NSA_HPG):
        qaug_ref[NSA_HD:NSA_HD + nblk, hh * tq:(hh + 1) * tq] = csel
    qaug = qaug_ref[...]

    def sel_step(j, carry, diag):
        m, l, acc = carry
        kt = ks_ref[0, 0, pl.ds(pl.multiple_of(j * TKV, TKV), TKV), :]
        s = _dot(kt, qaug) * scale
        if diag:
            key = j * TKV + lax.broadcasted_iota(jnp.int32, (TKV, tq), 0)
            s = s + _tile4(jnp.where(key <= tok, 0.0, NEG))
        m_new = jnp.maximum(m, jnp.max(s, axis=0, keepdims=True))
        alpha = jnp.exp(m - m_new)
        p = jnp.exp(s - m_new)
        l = alpha * l + jnp.sum(p, axis=0, keepdims=True)
        acc = alpha * acc + _dot(vsT_ref[0, 0, j], p.astype(BF16))
        return m_new, l, acc

    nfull = q0 // TKV
    init = (jnp.full((1, NSA_HPG * tq), M_INIT, F32), jnp.zeros((1, NSA_HPG * tq), F32),
            jnp.zeros((NSA_DV, NSA_HPG * tq), F32))
    carry = lax.fori_loop(0, nfull, lambda j, c: sel_step(j, c, False), init)
    _, l_s, acc_s = sel_step(nfull, carry, True)
    o_s = acc_s * (1.0 / l_s)

    w0 = jnp.maximum(nfull - (n_win_tiles - 1), 0)
    kwin = kw_ref[0, 0, pl.ds(pl.multiple_of(w0 * TKV, TKV), n_win_tiles * TKV), :]
    sw = _dot(kwin, qT) * scale
    key = w0 * TKV + lax.broadcasted_iota(jnp.int32, (n_win_tiles * TKV, tq), 0)
    inband = (key <= tok) & (key > tok - WINDOW)
    sw = sw + _tile4(jnp.where(inband, 0.0, NEG))
    mw = jnp.max(sw, axis=0, keepdims=True)
    ew = jnp.exp(sw - mw)
    dw = jnp.sum(ew, axis=0, keepdims=True)
    pw = (ew * (1.0 / dw)).astype(BF16)
    o_w = _dot(vwT_ref[0, 0, w0], pw[0:TKV])
    for r in range(1, n_win_tiles):
        o_w = o_w + _dot(vwT_ref[0, 0, w0 + r], pw[r * TKV:(r + 1) * TKV])

    for hh in range(NSA_HPG):
        sl = slice(hh * tq, (hh + 1) * tq)
        g0 = g_ref[0, 0, hh:hh + 1, :]
        g1 = g_ref[0, 0, NSA_HPG + hh:NSA_HPG + hh + 1, :]
        g2 = g_ref[0, 0, 2 * NSA_HPG + hh:2 * NSA_HPG + hh + 1, :]
        o = g0 * o_c[:, sl] + g1 * o_s[:, sl] + g2 * o_w[:, sl]
        rows = slice(hh * NSA_DV, (hh + 1) * NSA_DV)
        o_ref[0, rows, :] = (o * sz_ref[0, rows, :]).astype(BF16)


def _nsa_attn(qT, ks, vsT, kw, vwT, kcmp, vcmpT, ovT, gates, sz):
    B, H, _, S = qT.shape
    G = NSA_GROUPS
    nq = S // TQ_NSA
    nblk = S // SEL_LEN
    ncmp = kcmp.shape[2]
    assert TKV % TQ_NSA == 0
    n_win_tiles = (WINDOW - 1 + TKV - 1) // TKV + 1
    assert S // TKV >= n_win_tiles
    kern = functools.partial(_nsa_attn_kernel, k_sel=min(N_SELECT, nblk), n_win_tiles=n_win_tiles)
    per_bg = lambda shape: pl.BlockSpec((1, 1) + shape, lambda b, g, i: (b, g) + (0,) * len(shape))
    return pl.pallas_call(
        kern,
        grid=(B, G, nq),
        in_specs=[
            pl.BlockSpec((1, NSA_HPG, NSA_HD, TQ_NSA), lambda b, g, i: (b, g, 0, i)),
            per_bg((S, NSA_HD + nblk)),
            per_bg((S // TKV, NSA_DV, TKV)),
            per_bg((S, NSA_HD)),
            per_bg((S // TKV, NSA_DV, TKV)),
            per_bg((ncmp, NSA_HD)),
            per_bg((NSA_DV, ncmp)),
            pl.BlockSpec((nblk, ncmp), lambda b, g, i: (0, 0)),
            pl.BlockSpec((1, 1, 16, TQ_NSA), lambda b, g, i: (b, g, 0, i)),
            pl.BlockSpec((1, NSA_HPG * NSA_DV, TQ_NSA), lambda b, g, i: (b, g, i)),
        ],
        out_specs=pl.BlockSpec((1, NSA_HPG * NSA_DV, TQ_NSA), lambda b, g, i: (b, g, i)),
        out_shape=jax.ShapeDtypeStruct((B, H * NSA_DV, S), BF16),
        scratch_shapes=[pltpu.VMEM((NSA_HD + nblk, NSA_HPG * TQ_NSA), BF16)],
        compiler_params=_params(("parallel", "parallel", "arbitrary")),
        name="nsa_attention",
    )(qT, ks, vsT, kw, vwT, kcmp, vcmpT, ovT, gates, sz)


def _nsa_perm():
    src = np.full((NSA_HD,), -1, np.int64)
    src[0:12] = np.arange(0, 12)
    src[12:16] = np.arange(24, 28)
    src[16:64] = np.arange(28, 76)
    src[64:76] = np.arange(12, 24)
    src[76:80] = np.arange(76, 80)
    src[80:96] = np.arange(80, 96)
    return src


def _take_cols(w, src):
    idx = np.where(src >= 0, src, 0)
    out = jnp.take(w, jnp.asarray(idx), axis=-1)
    return jnp.where(jnp.asarray(src >= 0), out, 0.0)


def _pad_last(w, width):
    return jnp.pad(w, [(0, 0)] * (w.ndim - 1) + [(0, width - w.shape[-1])])


def _prep_mla(w_in, g_cq, w_uq, g_ckv, w_ukv, g_q, g_k, w_out):
    D = w_in.shape[0]
    o1 = MLA_Q_LORA + MLA_KV_LORA
    z32 = jnp.zeros((D, 32), F32)
    wa = jnp.concatenate([w_in[:, :o1], w_in[:, o1:o1 + 32], z32, w_in[:, o1 + 32:o1 + 64], z32], axis=1)
    wq = w_uq.reshape(MLA_Q_LORA, MLA_HEADS, MLA_QK)
    zq = jnp.zeros((MLA_Q_LORA, MLA_HEADS, 32), F32)
    wq = jnp.concatenate([wq[:, :, 64:], wq[:, :, :32], zq, wq[:, :, 32:64], zq], axis=2)
    z1 = jnp.zeros((32,), F32)
    gq = jnp.concatenate([g_q[64:], g_q[:32], z1, g_q[32:64], z1])
    wkv = w_ukv.reshape(MLA_KV_LORA, MLA_HEADS, MLA_NOPE + MLA_V)
    return {
        "wa": wa.astype(BF16),
        "wzT": w_in[:, o1 + MLA_ROPE:].T.astype(BF16),
        "gcq": g_cq.reshape(1, -1), "gckv": g_ckv.reshape(1, -1),
        "wuqT": wq.reshape(MLA_Q_LORA, MLA_HEADS * MLA_HD).T.astype(BF16),
        "gq": jnp.broadcast_to(gq[:, None], (MLA_HD, TT)),
        "wuk": wkv[:, :, :MLA_NOPE].reshape(MLA_KV_LORA, -1).astype(BF16),
        "wuvT": wkv[:, :, MLA_NOPE:].reshape(MLA_KV_LORA, -1).T.astype(BF16),
        "gkn": g_k[64:].reshape(1, -1),
        "gkp": jnp.concatenate([g_k[:32], z1, g_k[32:64], z1]).reshape(1, -1),
        "woT": w_out.T.astype(BF16),
    }


def _prep_nsa(w_in, g_q, g_k, pe_k, w1_k, w2_k, pe_v, w1_v, w2_v, w_out):
    D = w_in.shape[0]
    G = NSA_GROUPS
    src = _nsa_perm()
    offs = np.concatenate([[0], np.cumsum(NSA_SIZES)])
    part = lambda i: w_in[:, offs[i]:offs[i + 1]]
    q, kc, vc, ks, vs, kw, vw, gl, z = [part(i) for i in range(9)]
    perm_heads = lambda w, nh: _take_cols(w.reshape(D, nh, NSA_DK), src).reshape(D, nh * NSA_HD)
    pad_groups = lambda w, d: _pad_last(w.reshape(D, G, d), LANE).reshape(D, G * LANE)
    wtok = jnp.concatenate([perm_heads(ks, G), perm_heads(kw, G), pad_groups(kc, NSA_DK),
                            pad_groups(vc, NSA_DV)], axis=1)
    glr = gl.reshape(D, G, NSA_HPG, 3).transpose(0, 1, 3, 2).reshape(D, G, 3 * NSA_HPG)
    glr = _pad_last(glr, 16).reshape(D, G * 16)
    wf = jnp.concatenate([perm_heads(q, NSA_HEADS), vs, vw, glr, z], axis=1)
    halves = lambda w, d: _pad_last(w.reshape(2, CMP_LEN // 2, d, -1).transpose(0, 1, 3, 2), LANE) \
        .transpose(0, 1, 3, 2).reshape(2, (CMP_LEN // 2) * LANE, -1)
    pe_flat = lambda pe: _pad_last(pe, LANE).reshape(2, (CMP_LEN // 2) * LANE)
    return {
        "wtok": wtok.astype(BF16),
        "wfT": wf.T.astype(BF16),
        "gq": jnp.broadcast_to(_take_cols(g_q, src)[:, None], (NSA_HD, TT)),
        "gk": _take_cols(g_k, src),
        "pek": pe_flat(pe_k), "pev": pe_flat(pe_v),
        "w1k": _pad_last(halves(w1_k, NSA_DK), LANE).astype(BF16),
        "w2k": _pad_last(_take_cols(w2_k, src).T, LANE).T.astype(BF16),
        "w1v": _pad_last(halves(w1_v, NSA_DV), LANE).astype(BF16),
        "w2vT": _pad_last(w2_v.T, LANE).astype(BF16),
        "woT": w_out.T.astype(BF16),
    }


def _prep_tables(cos_t, sin_t, B, S):
    cm, sm = cos_t[:32].T, sin_t[:32].T
    cn, sn = cos_t[32:48].T, sin_t[32:48].T
    n = cm.shape[0]
    one = lambda w: jnp.ones((n, w), F32)
    zero = lambda w: jnp.zeros((n, w), F32)
    ct_nsa = jnp.concatenate([cn, one(48), cn, one(48)], axis=1)
    st_nsa = jnp.concatenate([-sn, zero(48), sn, zero(48)], axis=1)
    ncmp = S // CMP_STRIDE
    last = CMP_LEN - 1
    pick = lambda t: jnp.pad(t.reshape(B, S, LANE)[:, last::CMP_STRIDE],
                             ((0, 0), (0, ncmp - (S - last + CMP_STRIDE - 1) // CMP_STRIDE), (0, 0)))
    return {
        "cosT_mla": cos_t[:32], "sinT_mla": sin_t[:32],
        "cosT_nsa": cos_t[32:48], "sinT_nsa": sin_t[32:48],
        "ct_mla": jnp.concatenate([cm, one(32), cm, one(32)], axis=1),
        "st_mla": jnp.concatenate([-sm, zero(32), sm, zero(32)], axis=1),
        "ct_nsa": ct_nsa, "st_nsa": st_nsa,
        "ct_cmp": pick(ct_nsa), "st_cmp": pick(st_nsa),
    }


def _overlap_T(S):
    ncmp = S // CMP_STRIDE
    nblk = S // SEL_LEN
    n_cmp = (S - CMP_LEN) // CMP_STRIDE + 1
    cs = np.arange(ncmp) * CMP_STRIDE
    ss = np.arange(nblk) * SEL_LEN
    ov = (cs[None, :] < ss[:, None] + SEL_LEN) & (cs[None, :] + CMP_LEN > ss[:, None])
    ov = ov & (np.arange(ncmp)[None, :] < n_cmp)
    return jnp.asarray(ov.astype(np.float32)).astype(BF16)


def _mla_layer(x, ng, w, tabs):
    qT, k, vT, sz = _mla_in(x, ng.reshape(1, -1), w, tabs)
    ogT = _mla_attn(qT, k, vT, sz)
    return _out_proj(ogT, w["woT"], x)


def _nsa_layer(x, ng, w, tabs, ovT):
    B, S, _ = x.shape
    qT, ks, kw, kc, vc, vsT, vwT, gates, sz = _nsa_in(x, ng.reshape(1, -1), w, tabs)
    ncmp = S // CMP_STRIDE
    kc16 = kc.reshape(B, NSA_GROUPS, ncmp, CMP_STRIDE * LANE)
    vc16 = vc.reshape(B, NSA_GROUPS, ncmp, CMP_STRIDE * LANE)
    kcmp, vcmpT = _nsa_cmp(kc16, vc16, w, tabs)
    ogT = _nsa_attn(qT, ks, vsT, kw, vwT, kcmp, vcmpT, ovT, gates, sz)
    return _out_proj(ogT, w["woT"], x)


def kernel(x, positions, norm_g, mla_w_in, mla_g_cq, mla_w_uq, mla_g_ckv, mla_w_ukv, mla_g_q, mla_g_k, mla_w_out, nsa_w_in, nsa_g_q, nsa_g_k, nsa_pe_k, nsa_w1_k, nsa_w2_k, nsa_pe_v, nsa_w1_v, nsa_w2_v, nsa_w_out):
    B, S, _ = x.shape
    cos_t, sin_t = _rope_tables(positions)
    tabs = _prep_tables(cos_t, sin_t, B, S)
    ovT = _overlap_T(S)
    for i in range(DEPTH):
        j = i // N_MIXERS
        if i % N_MIXERS == 0:
            w = _prep_mla(mla_w_in[j], mla_g_cq[j], mla_w_uq[j], mla_g_ckv[j], mla_w_ukv[j],
                          mla_g_q[j], mla_g_k[j], mla_w_out[j])
            x = _mla_layer(x, norm_g[i], w, tabs)
        else:
            w = _prep_nsa(nsa_w_in[j], nsa_g_q[j], nsa_g_k[j], nsa_pe_k[j], nsa_w1_k[j],
                          nsa_w2_k[j], nsa_pe_v[j], nsa_w1_v[j], nsa_w2_v[j], nsa_w_out[j])
            x = _nsa_layer(x, norm_g[i], w, tabs, ovT)
    return x
```

```python
import functools

import numpy as np
import jax
import jax.numpy as jnp
from jax import lax
from jax.experimental import pallas as pl
from jax.experimental.pallas import tpu as pltpu

F32 = jnp.float32
BF16 = jnp.bfloat16

D_MODEL = 1024
DEPTH = 4
N_MIXERS = 2
ROPE_THETA = 500000.0
RMS_EPS = 1e-6
MLA_HEADS = 16
MLA_NOPE = 128
MLA_ROPE = 64
MLA_V = 128
MLA_QK = MLA_NOPE + MLA_ROPE
MLA_Q_LORA = 384
MLA_KV_LORA = 256
MLA_HD = 256
NSA_HEADS = 16
NSA_GROUPS = 4
NSA_HPG = NSA_HEADS // NSA_GROUPS
NSA_DK = 96
NSA_DV = 64
NSA_ROT = NSA_DK // 4
NSA_HALF = NSA_ROT // 2
NSA_HD = 128
CMP_LEN = 32
CMP_STRIDE = 16
SEL_LEN = 64
N_SELECT = 16
WINDOW = 512
SEL_FORCE = 1e4
NSA_SIZES = (NSA_HEADS * NSA_DK,
             NSA_GROUPS * NSA_DK, NSA_GROUPS * NSA_DV,
             NSA_GROUPS * NSA_DK, NSA_GROUPS * NSA_DV,
             NSA_GROUPS * NSA_DK, NSA_GROUPS * NSA_DV,
             3 * NSA_HEADS, NSA_HEADS * NSA_DV)

LANE = 128
TT = 256
TKV = 256
TSTEP = 512
TQ_MLA = 512
TQ_NSA = 128
NEG = -30000.0
M_INIT = -1e30
LOG2E = 1.4426950408889634
VMEM_LIMIT = 56 * 1024 * 1024


def _dot(a, b):
    return jnp.dot(a, b, preferred_element_type=F32)


def _dot_nt(a, b):
    return lax.dot_general(a, b, (((1,), (1,)), ((), ())), preferred_element_type=F32)


def _sigmoid(x):
    return 1.0 / (1.0 + jnp.exp(-x))


def _rms_rows(x, g):
    ms = jnp.mean(x * x, axis=-1, keepdims=True)
    return x * lax.rsqrt(ms + RMS_EPS) * g


def _params(sem):
    return pltpu.CompilerParams(dimension_semantics=sem, vmem_limit_bytes=VMEM_LIMIT)


def _rope_kernel(pos_ref, invf_ref, cos_ref, sin_ref):
    ang = invf_ref[...] * pos_ref[...]
    cos_ref[...] = jnp.cos(ang)
    sin_ref[...] = jnp.sin(ang)


def _rope_tables(positions):
    n = positions.size
    tn = 512
    pos = positions.reshape(1, n).astype(F32)
    half_m = MLA_ROPE // 2
    inv_m = ROPE_THETA ** (-jnp.arange(half_m, dtype=F32) / half_m)
    inv_n = ROPE_THETA ** (-jnp.arange(NSA_HALF, dtype=F32) / NSA_HALF)
    invf = jnp.concatenate([inv_m, inv_n, jnp.zeros((16 - NSA_HALF,), F32)])
    rows = invf.shape[0]
    invf = jnp.broadcast_to(invf[:, None], (rows, tn))
    cos_t, sin_t = pl.pallas_call(
        _rope_kernel,
        grid=(n // tn,),
        in_specs=[pl.BlockSpec((1, tn), lambda i: (0, i)),
                  pl.BlockSpec((rows, tn), lambda i: (0, 0))],
        out_specs=[pl.BlockSpec((rows, tn), lambda i: (0, i)),
                   pl.BlockSpec((rows, tn), lambda i: (0, i))],
        out_shape=[jax.ShapeDtypeStruct((rows, n), F32)] * 2,
        compiler_params=_params(("parallel",)),
        name="rope_tables",
    )(pos, invf)
    return cos_t, sin_t


def _mla_in_kernel(x_ref, ng_ref, wa_ref, wzT_ref, gcq_ref, gckv_ref, wuqT_ref, gq_ref,
                   wuk_ref, wuvT_ref, gkn_ref, gkp_ref, ct_ref, st_ref, cosT_ref, sinT_ref,
                   qT_ref, k_ref, vT_ref, sz_ref):
    x = x_ref[0]
    h = _rms_rows(x, ng_ref[...]).astype(BF16)
    pa = _dot(h, wa_ref[...])
    z = _dot_nt(wzT_ref[...], h)
    sz_ref[0] = z * _sigmoid(z)

    cqn = _rms_rows(pa[:, :MLA_Q_LORA], gcq_ref[...]).astype(BF16)
    ckvn = _rms_rows(pa[:, MLA_Q_LORA:MLA_Q_LORA + MLA_KV_LORA], gckv_ref[...]).astype(BF16)
    kpe = pa[:, MLA_Q_LORA + MLA_KV_LORA:]

    cos = cosT_ref[...]
    sin = sinT_ref[...]
    gq = gq_ref[...]
    zeros32 = jnp.zeros((32, x.shape[0]), BF16)
    for hd in range(MLA_HEADS):
        blk = _dot_nt(wuqT_ref[hd * MLA_HD:(hd + 1) * MLA_HD, :], cqn)
        ss = jnp.sum(blk * blk, axis=0, keepdims=True)
        qn = blk * lax.rsqrt(ss * (1.0 / MLA_QK) + RMS_EPS) * gq
        x1 = qn[128:160]
        x2 = qn[192:224]
        qT_ref[0, hd, 0:128, :] = qn[0:128].astype(BF16)
        qT_ref[0, hd, 128:160, :] = (x1 * cos - x2 * sin).astype(BF16)
        qT_ref[0, hd, 160:192, :] = zeros32
        qT_ref[0, hd, 192:224, :] = (x2 * cos + x1 * sin).astype(BF16)
        qT_ref[0, hd, 224:256, :] = zeros32

    kn = _dot(ckvn, wuk_ref[...])
    ss_pe = jnp.sum(kpe * kpe, axis=-1, keepdims=True)
    kpg = kpe * gkp_ref[...]
    prot = kpg * ct_ref[...] + pltpu.roll(kpg, 64, axis=1) * st_ref[...]
    gkn = gkn_ref[...]
    for hd in range(MLA_HEADS):
        kb = kn[:, hd * MLA_NOPE:(hd + 1) * MLA_NOPE]
        ss = jnp.sum(kb * kb, axis=-1, keepdims=True) + ss_pe
        r = lax.rsqrt(ss * (1.0 / MLA_QK) + RMS_EPS)
        k_ref[0, hd, :, 0:128] = (kb * r * gkn).astype(BF16)
        k_ref[0, hd, :, 128:256] = (prot * r).astype(BF16)

    for hd in range(MLA_HEADS):
        v = _dot_nt(wuvT_ref[hd * MLA_V:(hd + 1) * MLA_V, :], ckvn)
        vT_ref[0, hd, 0] = v.astype(BF16)


def _mla_in(x, ng, w, tabs):
    B, S, D = x.shape
    nt = S // TT
    full = lambda shape: pl.BlockSpec(shape, lambda b, t: (0,) * len(shape))
    tok = lambda b, t: (b * nt + t, 0)
    feat = lambda b, t: (0, b * nt + t)
    outs = pl.pallas_call(
        _mla_in_kernel,
        grid=(B, nt),
        in_specs=[
            pl.BlockSpec((1, TT, D), lambda b, t: (b, t, 0)),
            full((1, D)),
            full(w["wa"].shape), full(w["wzT"].shape),
            full((1, MLA_Q_LORA)), full((1, MLA_KV_LORA)),
            full(w["wuqT"].shape), full((MLA_HD, TT)),
            full(w["wuk"].shape), full(w["wuvT"].shape),
            full((1, MLA_NOPE)), full((1, LANE)),
            pl.BlockSpec((TT, LANE), tok), pl.BlockSpec((TT, LANE), tok),
            pl.BlockSpec((32, TT), feat), pl.BlockSpec((32, TT), feat),
        ],
        out_specs=[
            pl.BlockSpec((1, MLA_HEADS, MLA_HD, TT), lambda b, t: (b, 0, 0, t)),
            pl.BlockSpec((1, MLA_HEADS, TT, MLA_HD), lambda b, t: (b, 0, t, 0)),
            pl.BlockSpec((1, MLA_HEADS, 1, MLA_V, TT), lambda b, t: (b, 0, t, 0, 0)),
            pl.BlockSpec((1, MLA_HEADS * MLA_V, TT), lambda b, t: (b, 0, t)),
        ],
        out_shape=[
            jax.ShapeDtypeStruct((B, MLA_HEADS, MLA_HD, S), BF16),
            jax.ShapeDtypeStruct((B, MLA_HEADS, S, MLA_HD), BF16),
            jax.ShapeDtypeStruct((B, MLA_HEADS, nt, MLA_V, TT), BF16),
            jax.ShapeDtypeStruct((B, MLA_HEADS * MLA_V, S), F32),
        ],
        compiler_params=_params(("parallel", "parallel")),
        name="mla_in_proj",
    )(x, ng, w["wa"], w["wzT"], w["gcq"], w["gckv"], w["wuqT"], w["gq"], w["wuk"],
      w["wuvT"], w["gkn"], w["gkp"], tabs["ct_mla"], tabs["st_mla"], tabs["cosT_mla"],
      tabs["sinT_mla"])
    return outs


def _softmax_step(s, carry, c, vtile):
    m, l, acc = carry
    m_new = jnp.maximum(m, jnp.max(s, axis=0, keepdims=True))
    alpha = jnp.exp2((m - m_new) * c)
    p = jnp.exp2((s - m_new) * c)
    l = alpha * l + jnp.sum(p, axis=0, keepdims=True)
    pb = p.astype(BF16)
    pv = _dot(vtile(0), pb[0:TKV])
    for r in range(1, s.shape[0] // TKV):
        pv = pv + _dot(vtile(r), pb[r * TKV:(r + 1) * TKV])
    return m_new, l, alpha * acc + pv


SUB = 8
CHUNK = 64


def _softmax_step_ref(s_ref, carry, c, vtile):
    m, l, acc = carry
    n, w = s_ref.shape
    fold = lambda a, op: op(a.reshape(a.shape[0] // SUB, SUB, w), axis=0)
    mx = fold(s_ref[0:CHUNK, :], jnp.max)
    for r in range(1, n // CHUNK):
        mx = jnp.maximum(mx, fold(s_ref[r * CHUNK:(r + 1) * CHUNK, :], jnp.max))
    m_new = jnp.maximum(m, jnp.max(mx, axis=0, keepdims=True))
    alpha = jnp.exp2((m - m_new) * c)
    lsum = None
    pv = None
    for t in range(n // TKV):
        parts = []
        for r in range(TKV // CHUNK):
            lo = t * TKV + r * CHUNK
            p = jnp.exp2((s_ref[lo:lo + CHUNK, :] - m_new) * c)
            ps = fold(p, jnp.sum)
            lsum = ps if lsum is None else lsum + ps
            parts.append(p.astype(BF16))
        d = _dot(vtile(t), jnp.concatenate(parts, axis=0))
        pv = d if pv is None else pv + d
    l = alpha * l + jnp.sum(lsum, axis=0, keepdims=True)
    return m_new, l, alpha * acc + pv


def _sweep(scores, vtile, mask_last, n_full, carry, c, s_a, s_b):
    def step(src, dst, j, carry):
        dst[...] = scores(j + 1)
        return _softmax_step_ref(src, carry, c, vtile(j))

    def last(src, j, carry):
        src[...] = mask_last(src[...])
        return _softmax_step_ref(src, carry, c, vtile(j))

    def pair(t, carry):
        carry = step(s_a, s_b, 2 * t, carry)
        return step(s_b, s_a, 2 * t + 1, carry)

    s_a[...] = scores(0)
    carry = lax.fori_loop(0, n_full // 2, pair, carry)
    odd = lambda carry: last(s_b, n_full, step(s_a, s_b, n_full - 1, carry))
    even = lambda carry: last(s_a, n_full, carry)
    return lax.cond(n_full % 2 == 1, odd, even, carry)


def _mla_attn_kernel(qT_ref, k_ref, vT_ref, sz_ref, o_ref, s_a, s_b):
    i = pl.program_id(2)
    q = qT_ref[0, 0]
    c = (MLA_QK ** -0.5) * LOG2E
    tq = q.shape[1]
    nsub = TSTEP // TKV

    def scores(j):
        kt = k_ref[0, 0, pl.ds(pl.multiple_of(j * TSTEP, TSTEP), TSTEP), :]
        return _dot(kt, q)

    def vtile(j):
        return lambda r: vT_ref[0, 0, j * nsub + r]

    def causal(s):
        row = lax.broadcasted_iota(jnp.int32, s.shape, 0)
        col = lax.broadcasted_iota(jnp.int32, s.shape, 1)
        return jnp.where(row <= col, s, M_INIT)

    init = (jnp.full((1, tq), M_INIT, F32), jnp.zeros((1, tq), F32),
            jnp.zeros((MLA_V, tq), F32))
    m, l, acc = _sweep(scores, vtile, causal, i, init, c, s_a, s_b)
    o_ref[0] = ((acc / l) * sz_ref[0]).astype(BF16)


def _mla_attn(qT, k, vT, sz):
    B, H, _, S = qT.shape
    assert TQ_MLA == TSTEP and TSTEP % TKV == 0
    nq = S // TQ_MLA
    return pl.pallas_call(
        _mla_attn_kernel,
        grid=(B, H, nq),
        in_specs=[
            pl.BlockSpec((1, 1, MLA_HD, TQ_MLA), lambda b, h, i: (b, h, 0, i)),
            pl.BlockSpec((1, 1, S, MLA_HD), lambda b, h, i: (b, h, 0, 0)),
            pl.BlockSpec((1, 1, S // TKV, MLA_V, TKV), lambda b, h, i: (b, h, 0, 0, 0)),
            pl.BlockSpec((1, MLA_V, TQ_MLA), lambda b, h, i: (b, h, i)),
        ],
        out_specs=pl.BlockSpec((1, MLA_V, TQ_MLA), lambda b, h, i: (b, h, i)),
        out_shape=jax.ShapeDtypeStruct((B, H * MLA_V, S), BF16),
        scratch_shapes=[pltpu.VMEM((TSTEP, TQ_MLA), F32)] * 2,
        compiler_params=_params(("parallel", "parallel", "arbitrary")),
        name="mla_attention",
    )(qT, k, vT, sz)


def _out_proj_kernel(og_ref, wT_ref, x_ref, o_ref):
    yT = _dot(wT_ref[...], og_ref[0])
    o_ref[0] = x_ref[0] + yT.T


def _out_proj(ogT, wT, x):
    B, S, D = x.shape
    K = ogT.shape[1]
    return pl.pallas_call(
        _out_proj_kernel,
        grid=(B, S // TT),
        in_specs=[
            pl.BlockSpec((1, K, TT), lambda b, t: (b, 0, t)),
            pl.BlockSpec((D, K), lambda b, t: (0, 0)),
            pl.BlockSpec((1, TT, D), lambda b, t: (b, t, 0)),
        ],
        out_specs=pl.BlockSpec((1, TT, D), lambda b, t: (b, t, 0)),
        out_shape=jax.ShapeDtypeStruct((B, S, D), F32),
        compiler_params=_params(("parallel", "parallel")),
        name="out_proj",
    )(ogT, wT, x)


def _nsa_in_kernel(x_ref, ng_ref, wtok_ref, wfT_ref, gq_ref, gk_ref, ct_ref, st_ref,
                   cosT_ref, sinT_ref,
                   qT_ref, ks_ref, kw_ref, kc_ref, vc_ref, vsT_ref, vwT_ref, g_ref, sz_ref):
    t = pl.program_id(1)
    x = x_ref[0]
    n = x.shape[0]
    h = _rms_rows(x, ng_ref[...]).astype(BF16)
    G = NSA_GROUPS

    cos = cosT_ref[...]
    sin = sinT_ref[...]
    gq = gq_ref[...]
    for hd in range(NSA_HEADS):
        blk = _dot_nt(wfT_ref[hd * NSA_HD:(hd + 1) * NSA_HD, :], h)
        ss = jnp.sum(blk * blk, axis=0, keepdims=True)
        qn = blk * lax.rsqrt(ss * (1.0 / NSA_DK) + RMS_EPS) * gq
        x1 = qn[0:16]
        x2 = qn[64:80]
        qT_ref[0, hd, 0:16, :] = (x1 * cos - x2 * sin).astype(BF16)
        qT_ref[0, hd, 16:64, :] = qn[16:64].astype(BF16)
        qT_ref[0, hd, 64:80, :] = (x2 * cos + x1 * sin).astype(BF16)
        qT_ref[0, hd, 80:128, :] = qn[80:128].astype(BF16)
    off = NSA_HEADS * NSA_HD
    for g in range(G):
        v = _dot_nt(wfT_ref[off + g * NSA_DV:off + (g + 1) * NSA_DV, :], h)
        vsT_ref[0, g, 0] = v.astype(BF16)
    off += G * NSA_DV
    for g in range(G):
        v = _dot_nt(wfT_ref[off + g * NSA_DV:off + (g + 1) * NSA_DV, :], h)
        vwT_ref[0, g, 0] = v.astype(BF16)
    off += G * NSA_DV
    gl = _dot_nt(wfT_ref[off:off + 16 * G, :], h)
    sg = _sigmoid(gl)
    for g in range(G):
        g_ref[0, g] = sg[g * 16:(g + 1) * 16]
    off += 16 * G
    z = _dot_nt(wfT_ref[off:off + NSA_HEADS * NSA_DV, :], h)
    sz_ref[0] = z * _sigmoid(z)

    pt = _dot(h, wtok_ref[...])
    ct = ct_ref[...]
    st = st_ref[...]
    nblk = ks_ref.shape[-1] - NSA_HD
    row = lax.broadcasted_iota(jnp.int32, (n, nblk), 0) + t * n
    col = lax.broadcasted_iota(jnp.int32, (n, nblk), 1)
    onehot = jnp.where((row // SEL_LEN) == col, 1.0, 0.0).astype(BF16)
    for br, ref in ((0, ks_ref), (1, kw_ref)):
        gk = gk_ref[br + 1:br + 2, :]
        for g in range(G):
            kb = pt[:, (br * G + g) * NSA_HD:(br * G + g + 1) * NSA_HD]
            ss = jnp.sum(kb * kb, axis=-1, keepdims=True)
            kn = kb * lax.rsqrt(ss * (1.0 / NSA_DK) + RMS_EPS) * gk
            kr = kn * ct + pltpu.roll(kn, 64, axis=1) * st
            ref[0, g, :, 0:NSA_HD] = kr.astype(BF16)
            if br == 0:
                ref[0, g, :, NSA_HD:NSA_HD + nblk] = onehot
    for g in range(G):
        kc_ref[0, g] = pt[:, (2 * G + g) * LANE:(2 * G + g + 1) * LANE]
        vc_ref[0, g] = pt[:, (3 * G + g) * LANE:(3 * G + g + 1) * LANE]


def _nsa_in(x, ng, w, tabs):
    B, S, D = x.shape
    nt = S // TT
    G = NSA_GROUPS
    nblk = S // SEL_LEN
    full = lambda shape: pl.BlockSpec(shape, lambda b, t: (0,) * len(shape))
    tok = lambda b, t: (b * nt + t, 0)
    feat = lambda b, t: (0, b * nt + t)
    tokmaj = lambda width: pl.BlockSpec((1, G, TT, width), lambda b, t: (b, 0, t, 0))
    vtile = pl.BlockSpec((1, G, 1, NSA_DV, TT), lambda b, t: (b, 0, t, 0, 0))
    return pl.pallas_call(
        _nsa_in_kernel,
        grid=(B, nt),
        in_specs=[
            pl.BlockSpec((1, TT, D), lambda b, t: (b, t, 0)),
            full((1, D)),
            full(w["wtok"].shape), full(w["wfT"].shape),
            full((NSA_HD, TT)), full((3, NSA_HD)),
            pl.BlockSpec((TT, LANE), tok), pl.BlockSpec((TT, LANE), tok),
            pl.BlockSpec((16, TT), feat), pl.BlockSpec((16, TT), feat),
        ],
        out_specs=[
            pl.BlockSpec((1, NSA_HEADS, NSA_HD, TT), lambda b, t: (b, 0, 0, t)),
            tokmaj(NSA_HD + nblk), tokmaj(NSA_HD), tokmaj(LANE), tokmaj(LANE),
            vtile, vtile,
            pl.BlockSpec((1, G, 16, TT), lambda b, t: (b, 0, 0, t)),
            pl.BlockSpec((1, NSA_HEADS * NSA_DV, TT), lambda b, t: (b, 0, t)),
        ],
        out_shape=[
            jax.ShapeDtypeStruct((B, NSA_HEADS, NSA_HD, S), BF16),
            jax.ShapeDtypeStruct((B, G, S, NSA_HD + nblk), BF16),
            jax.ShapeDtypeStruct((B, G, S, NSA_HD), BF16),
            jax.ShapeDtypeStruct((B, G, S, LANE), F32),
            jax.ShapeDtypeStruct((B, G, S, LANE), F32),
            jax.ShapeDtypeStruct((B, G, nt, NSA_DV, TT), BF16),
            jax.ShapeDtypeStruct((B, G, nt, NSA_DV, TT), BF16),
            jax.ShapeDtypeStruct((B, G, 16, S), F32),
            jax.ShapeDtypeStruct((B, NSA_HEADS * NSA_DV, S), F32),
        ],
        compiler_params=_params(("parallel", "parallel")),
        name="nsa_in_proj",
    )(x, ng, w["wtok"], w["wfT"], w["gq"], w["gk"], tabs["ct_nsa"], tabs["st_nsa"],
      tabs["cosT_nsa"], tabs["sinT_nsa"])


def _nsa_cmp_kernel(kc_ref, vc_ref, pek_ref, pev_ref, w1k_ref, w2k_ref, w1v_ref, w2vT_ref,
                    gk_ref, ct_ref, st_ref, kcmp_ref, vcmpT_ref):
    nrow = kc_ref.shape[2]

    def pre(x, pe_ref, w1_ref):
        lo = _dot((x + pe_ref[0:1, :]).astype(BF16), w1_ref[0])
        hi = _dot((x + pe_ref[1:2, :]).astype(BF16), w1_ref[1])
        return lo + pltpu.roll(hi, nrow - 1, axis=0)

    a = pre(kc_ref[0, 0], pek_ref, w1k_ref)
    kc = _dot((a * _sigmoid(a)).astype(BF16), w2k_ref[...])
    ss = jnp.sum(kc * kc, axis=-1, keepdims=True)
    kn = kc * lax.rsqrt(ss * (1.0 / NSA_DK) + RMS_EPS) * gk_ref[0:1, :]
    kr = kn * ct_ref[0] + pltpu.roll(kn, 64, axis=1) * st_ref[0]
    kcmp_ref[0, 0] = kr.astype(BF16)

    a = pre(vc_ref[0, 0], pev_ref, w1v_ref)
    sv = (a * _sigmoid(a)).astype(BF16)
    vcmpT_ref[0, 0] = _dot_nt(w2vT_ref[...], sv).astype(BF16)


def _nsa_cmp(kc16, vc16, w, tabs):
    B, G, nrow, width = kc16.shape
    full = lambda shape: pl.BlockSpec(shape, lambda b, g: (0,) * len(shape))
    blk = pl.BlockSpec((1, 1, nrow, width), lambda b, g: (b, g, 0, 0))
    tab = pl.BlockSpec((1, nrow, LANE), lambda b, g: (b, 0, 0))
    return pl.pallas_call(
        _nsa_cmp_kernel,
        grid=(B, G),
        in_specs=[blk, blk, full((2, width)), full((2, width)),
                  full(w["w1k"].shape), full(w["w2k"].shape),
                  full(w["w1v"].shape), full(w["w2vT"].shape),
                  full((3, NSA_HD)), tab, tab],
        out_specs=[pl.BlockSpec((1, 1, nrow, NSA_HD), lambda b, g: (b, g, 0, 0)),
                   pl.BlockSpec((1, 1, NSA_DV, nrow), lambda b, g: (b, g, 0, 0))],
        out_shape=[jax.ShapeDtypeStruct((B, G, nrow, NSA_HD), BF16),
                   jax.ShapeDtypeStruct((B, G, NSA_DV, nrow), BF16)],
        compiler_params=_params(("parallel", "parallel")),
        name="nsa_compress",
    )(kc16, vc16, w["pek"], w["pev"], w["w1k"], w["w2k"], w["w1v"], w["w2vT"], w["gk"],
      tabs["ct_cmp"], tabs["st_cmp"])


def _tile4(a):
    return jnp.concatenate([a] * NSA_HPG, axis=1)


def _nsa_attn_kernel(qT_ref, ks_ref, vsT_ref, kw_ref, vwT_ref, kc_ref, vcT_ref, ovT_ref,
                     g_ref, sz_ref, o_ref, qaug_ref, s_a, s_b, *, k_sel, n_win_tiles):
    i = pl.program_id(2)
    tq = TQ_NSA
    q0 = i * tq
    c = (NSA_DK ** -0.5) * LOG2E
    ncmp = kc_ref.shape[2]

    for hh in range(NSA_HPG):
        qaug_ref[0:NSA_HD, hh * tq:(hh + 1) * tq] = qT_ref[0, hh]
    qT = qaug_ref[0:NSA_HD, :]
    tok = q0 + lax.broadcasted_iota(jnp.int32, (1, tq), 1)

    sc = _dot(kc_ref[0, 0], qT)
    nidx = lax.broadcasted_iota(jnp.int32, (ncmp, tq), 0)
    validc = (nidx * CMP_STRIDE + (CMP_LEN - 1)) <= tok
    biasc = _tile4(jnp.where(validc, 0.0, NEG))
    validf = _tile4(jnp.where(validc, 1.0, 0.0))
    sc = sc + biasc
    mc = jnp.max(sc, axis=0, keepdims=True)
    ec = jnp.exp2((sc - mc) * c) * validf
    dc = jnp.sum(ec, axis=0, keepdims=True)
    pc = ec * (1.0 / jnp.where(dc > 0, dc, 1.0))
    o_c = _dot(vcT_ref[0, 0], pc.astype(BF16))
    psum = pc[:, 0:tq]
    for hh in range(1, NSA_HPG):
        psum = psum + pc[:, hh * tq:(hh + 1) * tq]
    imp = _dot(ovT_ref[...], psum.astype(BF16))

    nblk = imp.shape[0]
    bidx = lax.broadcasted_iota(jnp.int32, (nblk, tq), 0)
    cur = tok // SEL_LEN
    forced = (bidx == 0) | (bidx == cur) | (bidx == cur - 1)
    valid = (bidx * SEL_LEN) <= tok
    score = jnp.where(forced, SEL_FORCE, jnp.where(valid, imp, -1.0))
    sub = 8
    grp = [score[r * sub:(r + 1) * sub] for r in range(nblk // sub)]
    bsub = lax.broadcasted_iota(jnp.int32, (sub, tq), 0)
    cnt = [jnp.zeros((sub, tq), F32) for _ in grp]
    for j in range(nblk):
        rowj = score[j:j + 1, :]
        for r in range(nblk // sub):
            if r * sub > j:
                cnt[r] = cnt[r] + jnp.where(rowj >= grp[r], 1.0, 0.0)
            elif r * sub + sub - 1 < j:
                cnt[r] = cnt[r] + jnp.where(rowj > grp[r], 1.0, 0.0)
            else:
                tie = jnp.where(bsub + r * sub > j, 1.0, 0.0)
                cnt[r] = cnt[r] + jnp.where(rowj > grp[r], 1.0, 0.0)
                cnt[r] = cnt[r] + jnp.where(rowj == grp[r], tie, 0.0)
    cnt = jnp.concatenate(cnt, axis=0)
    csel = jnp.where(cnt < k_sel, 0.0, NEG).astype(BF16)
    for hh in range(NSA_HPG):
        qaug_ref[NSA_HD:NSA_HD + nblk, hh * tq:(hh + 1) * tq] = csel
    qaug = qaug_ref[...]

    nsub = TSTEP // TKV

    def scores(j):
        kt = ks_ref[0, 0, pl.ds(pl.multiple_of(j * TSTEP, TSTEP), TSTEP), :]
        return _dot(kt, qaug)

    def vtile(j):
        return lambda r: vsT_ref[0, 0, j * nsub + r]

    nfull = q0 // TSTEP

    def causal(s):
        key = nfull * TSTEP + lax.broadcasted_iota(jnp.int32, (TSTEP, tq), 0)
        return s + _tile4(jnp.where(key <= tok, 0.0, NEG))

    init = (jnp.full((1, NSA_HPG * tq), M_INIT, F32),
            jnp.zeros((1, NSA_HPG * tq), F32), jnp.zeros((NSA_DV, NSA_HPG * tq), F32))
    _, l_s, acc_s = _sweep(scores, vtile, causal, nfull, init, c, s_a, s_b)
    o_s = acc_s * (1.0 / l_s)

    w0 = jnp.maximum(q0 // TKV - (n_win_tiles - 1), 0)
    kwin = kw_ref[0, 0, pl.ds(pl.multiple_of(w0 * TKV, TKV), n_win_tiles * TKV), :]
    sw = _dot(kwin, qT)
    key = w0 * TKV + lax.broadcasted_iota(jnp.int32, (n_win_tiles * TKV, tq), 0)
    inband = (key <= tok) & (key > tok - WINDOW)
    sw = sw + _tile4(jnp.where(inband, 0.0, NEG))
    mw = jnp.max(sw, axis=0, keepdims=True)
    ew = jnp.exp2((sw - mw) * c)
    dw = jnp.sum(ew, axis=0, keepdims=True)
    pw = (ew * (1.0 / dw)).astype(BF16)
    o_w = _dot(vwT_ref[0, 0, w0], pw[0:TKV])
    for r in range(1, n_win_tiles):
        o_w = o_w + _dot(vwT_ref[0, 0, w0 + r], pw[r * TKV:(r + 1) * TKV])

    for hh in range(NSA_HPG):
        sl = slice(hh * tq, (hh + 1) * tq)
        g0 = g_ref[0, 0, hh:hh + 1, :]
        g1 = g_ref[0, 0, NSA_HPG + hh:NSA_HPG + hh + 1, :]
        g2 = g_ref[0, 0, 2 * NSA_HPG + hh:2 * NSA_HPG + hh + 1, :]
        o = g0 * o_c[:, sl] + g1 * o_s[:, sl] + g2 * o_w[:, sl]
        rows = slice(hh * NSA_DV, (hh + 1) * NSA_DV)
        o_ref[0, rows, :] = (o * sz_ref[0, rows, :]).astype(BF16)


def _nsa_attn(qT, ks, vsT, kw, vwT, kcmp, vcmpT, ovT, gates, sz):
    B, H, _, S = qT.shape
    G = NSA_GROUPS
    nq = S // TQ_NSA
    nblk = S // SEL_LEN
    ncmp = kcmp.shape[2]
    assert TKV % TQ_NSA == 0 and TSTEP % TKV == 0 and S % TSTEP == 0
    n_win_tiles = (WINDOW - 1 + TKV - 1) // TKV + 1
    assert S // TKV >= n_win_tiles
    kern = functools.partial(_nsa_attn_kernel, k_sel=min(N_SELECT, nblk), n_win_tiles=n_win_tiles)
    per_bg = lambda shape: pl.BlockSpec((1, 1) + shape, lambda b, g, i: (b, g) + (0,) * len(shape))
    return pl.pallas_call(
        kern,
        grid=(B, G, nq),
        in_specs=[
            pl.BlockSpec((1, NSA_HPG, NSA_HD, TQ_NSA), lambda b, g, i: (b, g, 0, i)),
            per_bg((S, NSA_HD + nblk)),
            per_bg((S // TKV, NSA_DV, TKV)),
            per_bg((S, NSA_HD)),
            per_bg((S // TKV, NSA_DV, TKV)),
            per_bg((ncmp, NSA_HD)),
            per_bg((NSA_DV, ncmp)),
            pl.BlockSpec((nblk, ncmp), lambda b, g, i: (0, 0)),
            pl.BlockSpec((1, 1, 16, TQ_NSA), lambda b, g, i: (b, g, 0, i)),
            pl.BlockSpec((1, NSA_HPG * NSA_DV, TQ_NSA), lambda b, g, i: (b, g, i)),
        ],
        out_specs=pl.BlockSpec((1, NSA_HPG * NSA_DV, TQ_NSA), lambda b, g, i: (b, g, i)),
        out_shape=jax.ShapeDtypeStruct((B, H * NSA_DV, S), BF16),
        scratch_shapes=[pltpu.VMEM((NSA_HD + nblk, NSA_HPG * TQ_NSA), BF16)]
        + [pltpu.VMEM((TSTEP, NSA_HPG * TQ_NSA), F32)] * 2,
        compiler_params=_params(("parallel", "parallel", "arbitrary")),
        name="nsa_attention",
    )(qT, ks, vsT, kw, vwT, kcmp, vcmpT, ovT, gates, sz)


def _nsa_perm():
    src = np.full((NSA_HD,), -1, np.int64)
    src[0:12] = np.arange(0, 12)
    src[12:16] = np.arange(24, 28)
    src[16:64] = np.arange(28, 76)
    src[64:76] = np.arange(12, 24)
    src[76:80] = np.arange(76, 80)
    src[80:96] = np.arange(80, 96)
    return src


def _take_cols(w, src):
    idx = np.where(src >= 0, src, 0)
    out = jnp.take(w, jnp.asarray(idx), axis=-1)
    return jnp.where(jnp.asarray(src >= 0), out, 0.0)


def _pad_last(w, width):
    return jnp.pad(w, [(0, 0)] * (w.ndim - 1) + [(0, width - w.shape[-1])])


def _prep_mla(w_in, g_cq, w_uq, g_ckv, w_ukv, g_q, g_k, w_out):
    D = w_in.shape[0]
    o1 = MLA_Q_LORA + MLA_KV_LORA
    z32 = jnp.zeros((D, 32), F32)
    wa = jnp.concatenate([w_in[:, :o1], w_in[:, o1:o1 + 32], z32, w_in[:, o1 + 32:o1 + 64], z32], axis=1)
    wq = w_uq.reshape(MLA_Q_LORA, MLA_HEADS, MLA_QK)
    zq = jnp.zeros((MLA_Q_LORA, MLA_HEADS, 32), F32)
    wq = jnp.concatenate([wq[:, :, 64:], wq[:, :, :32], zq, wq[:, :, 32:64], zq], axis=2)
    z1 = jnp.zeros((32,), F32)
    gq = jnp.concatenate([g_q[64:], g_q[:32], z1, g_q[32:64], z1])
    wkv = w_ukv.reshape(MLA_KV_LORA, MLA_HEADS, MLA_NOPE + MLA_V)
    return {
        "wa": wa.astype(BF16),
        "wzT": w_in[:, o1 + MLA_ROPE:].T.astype(BF16),
        "gcq": g_cq.reshape(1, -1), "gckv": g_ckv.reshape(1, -1),
        "wuqT": wq.reshape(MLA_Q_LORA, MLA_HEADS * MLA_HD).T.astype(BF16),
        "gq": jnp.broadcast_to(gq[:, None], (MLA_HD, TT)),
        "wuk": wkv[:, :, :MLA_NOPE].reshape(MLA_KV_LORA, -1).astype(BF16),
        "wuvT": wkv[:, :, MLA_NOPE:].reshape(MLA_KV_LORA, -1).T.astype(BF16),
        "gkn": g_k[64:].reshape(1, -1),
        "gkp": jnp.concatenate([g_k[:32], z1, g_k[32:64], z1]).reshape(1, -1),
        "woT": w_out.T.astype(BF16),
    }


def _prep_nsa(w_in, g_q, g_k, pe_k, w1_k, w2_k, pe_v, w1_v, w2_v, w_out):
    D = w_in.shape[0]
    G = NSA_GROUPS
    src = _nsa_perm()
    offs = np.concatenate([[0], np.cumsum(NSA_SIZES)])
    part = lambda i: w_in[:, offs[i]:offs[i + 1]]
    q, kc, vc, ks, vs, kw, vw, gl, z = [part(i) for i in range(9)]
    perm_heads = lambda w, nh: _take_cols(w.reshape(D, nh, NSA_DK), src).reshape(D, nh * NSA_HD)
    pad_groups = lambda w, d: _pad_last(w.reshape(D, G, d), LANE).reshape(D, G * LANE)
    wtok = jnp.concatenate([perm_heads(ks, G), perm_heads(kw, G), pad_groups(kc, NSA_DK),
                            pad_groups(vc, NSA_DV)], axis=1)
    glr = gl.reshape(D, G, NSA_HPG, 3).transpose(0, 1, 3, 2).reshape(D, G, 3 * NSA_HPG)
    glr = _pad_last(glr, 16).reshape(D, G * 16)
    wf = jnp.concatenate([perm_heads(q, NSA_HEADS), vs, vw, glr, z], axis=1)
    halves = lambda w, d: _pad_last(w.reshape(2, CMP_LEN // 2, d, -1).transpose(0, 1, 3, 2), LANE) \
        .transpose(0, 1, 3, 2).reshape(2, (CMP_LEN // 2) * LANE, -1)
    pe_flat = lambda pe: _pad_last(pe, LANE).reshape(2, (CMP_LEN // 2) * LANE)
    return {
        "wtok": wtok.astype(BF16),
        "wfT": wf.T.astype(BF16),
        "gq": jnp.broadcast_to(_take_cols(g_q, src)[:, None], (NSA_HD, TT)),
        "gk": _take_cols(g_k, src),
        "pek": pe_flat(pe_k), "pev": pe_flat(pe_v),
        "w1k": _pad_last(halves(w1_k, NSA_DK), LANE).astype(BF16),
        "w2k": _pad_last(_take_cols(w2_k, src).T, LANE).T.astype(BF16),
        "w1v": _pad_last(halves(w1_v, NSA_DV), LANE).astype(BF16),
        "w2vT": _pad_last(w2_v.T, LANE).astype(BF16),
        "woT": w_out.T.astype(BF16),
    }


def _prep_tables(cos_t, sin_t, B, S):
    cm, sm = cos_t[:32].T, sin_t[:32].T
    cn, sn = cos_t[32:48].T, sin_t[32:48].T
    n = cm.shape[0]
    one = lambda w: jnp.ones((n, w), F32)
    zero = lambda w: jnp.zeros((n, w), F32)
    ct_nsa = jnp.concatenate([cn, one(48), cn, one(48)], axis=1)
    st_nsa = jnp.concatenate([-sn, zero(48), sn, zero(48)], axis=1)
    ncmp = S // CMP_STRIDE
    last = CMP_LEN - 1
    pick = lambda t: jnp.pad(t.reshape(B, S, LANE)[:, last::CMP_STRIDE],
                             ((0, 0), (0, ncmp - (S - last + CMP_STRIDE - 1) // CMP_STRIDE), (0, 0)))
    return {
        "cosT_mla": cos_t[:32], "sinT_mla": sin_t[:32],
        "cosT_nsa": cos_t[32:48], "sinT_nsa": sin_t[32:48],
        "ct_mla": jnp.concatenate([cm, one(32), cm, one(32)], axis=1),
        "st_mla": jnp.concatenate([-sm, zero(32), sm, zero(32)], axis=1),
        "ct_nsa": ct_nsa, "st_nsa": st_nsa,
        "ct_cmp": pick(ct_nsa), "st_cmp": pick(st_nsa),
    }


def _overlap_T(S):
    ncmp = S // CMP_STRIDE
    nblk = S // SEL_LEN
    n_cmp = (S - CMP_LEN) // CMP_STRIDE + 1
    cs = np.arange(ncmp) * CMP_STRIDE
    ss = np.arange(nblk) * SEL_LEN
    ov = (cs[None, :] < ss[:, None] + SEL_LEN) & (cs[None, :] + CMP_LEN > ss[:, None])
    ov = ov & (np.arange(ncmp)[None, :] < n_cmp)
    return jnp.asarray(ov.astype(np.float32)).astype(BF16)


def _mla_layer(x, ng, w, tabs):
    qT, k, vT, sz = _mla_in(x, ng.reshape(1, -1), w, tabs)
    ogT = _mla_attn(qT, k, vT, sz)
    return _out_proj(ogT, w["woT"], x)


def _nsa_layer(x, ng, w, tabs, ovT):
    B, S, _ = x.shape
    qT, ks, kw, kc, vc, vsT, vwT, gates, sz = _nsa_in(x, ng.reshape(1, -1), w, tabs)
    ncmp = S // CMP_STRIDE
    kc16 = kc.reshape(B, NSA_GROUPS, ncmp, CMP_STRIDE * LANE)
    vc16 = vc.reshape(B, NSA_GROUPS, ncmp, CMP_STRIDE * LANE)
    kcmp, vcmpT = _nsa_cmp(kc16, vc16, w, tabs)
    ogT = _nsa_attn(qT, ks, vsT, kw, vwT, kcmp, vcmpT, ovT, gates, sz)
    return _out_proj(ogT, w["woT"], x)


def kernel(x, positions, norm_g, mla_w_in, mla_g_cq, mla_w_uq, mla_g_ckv, mla_w_ukv, mla_g_q, mla_g_k, mla_w_out, nsa_w_in, nsa_g_q, nsa_g_k, nsa_pe_k, nsa_w1_k, nsa_w2_k, nsa_pe_v, nsa_w1_v, nsa_w2_v, nsa_w_out):
    B, S, _ = x.shape
    cos_t, sin_t = _rope_tables(positions)
    tabs = _prep_tables(cos_t, sin_t, B, S)
    ovT = _overlap_T(S)
    for i in range(DEPTH):
        j = i // N_MIXERS
        if i % N_MIXERS == 0:
            w = _prep_mla(mla_w_in[j], mla_g_cq[j], mla_w_uq[j], mla_g_ckv[j], mla_w_ukv[j],
                          mla_g_q[j], mla_g_k[j], mla_w_out[j])
            x = _mla_layer(x, norm_g[i], w, tabs)
        else:
            w = _prep_nsa(nsa_w_in[j], nsa_g_q[j], nsa_g_k[j], nsa_pe_k[j], nsa_w1_k[j],
                          nsa_w2_k[j], nsa_pe_v[j], nsa_w1_v[j], nsa_w2_v[j], nsa_w_out[j])
            x = _nsa_layer(x, norm_g[i], w, tabs, ovT)
    return x
```

```python
import functools

import numpy as np
import jax
import jax.numpy as jnp
from jax import lax
from jax.experimental import pallas as pl
from jax.experimental.pallas import tpu as pltpu

F32 = jnp.float32
BF16 = jnp.bfloat16

D_MODEL = 1024
DEPTH = 4
N_MIXERS = 2
ROPE_THETA = 500000.0
RMS_EPS = 1e-6
MLA_HEADS = 16
MLA_NOPE = 128
MLA_ROPE = 64
MLA_V = 128
MLA_QK = MLA_NOPE + MLA_ROPE
MLA_Q_LORA = 384
MLA_KV_LORA = 256
MLA_HD = 256
NSA_HEADS = 16
NSA_GROUPS = 4
NSA_HPG = NSA_HEADS // NSA_GROUPS
NSA_DK = 96
NSA_DV = 64
NSA_ROT = NSA_DK // 4
NSA_HALF = NSA_ROT // 2
NSA_HD = 128
CMP_LEN = 32
CMP_STRIDE = 16
SEL_LEN = 64
N_SELECT = 16
WINDOW = 512
SEL_FORCE = 1e4
NSA_SIZES = (NSA_HEADS * NSA_DK,
             NSA_GROUPS * NSA_DK, NSA_GROUPS * NSA_DV,
             NSA_GROUPS * NSA_DK, NSA_GROUPS * NSA_DV,
             NSA_GROUPS * NSA_DK, NSA_GROUPS * NSA_DV,
             3 * NSA_HEADS, NSA_HEADS * NSA_DV)

LANE = 128
SUB = 8
TT = 256
TO = 512
TKV = 256
TSTEP = 512
TQ_MLA = 512
TQ_NSA = 128
CHUNK = 64
ONES = 16
NEG = -30000.0
M_INIT = -1e30
LOG2E = 1.4426950408889634
C_MLA = (MLA_QK ** -0.5) * LOG2E
C_NSA = (NSA_DK ** -0.5) * LOG2E
VMEM_LIMIT = 56 * 1024 * 1024


def _dot(a, b):
    return jnp.dot(a, b, preferred_element_type=F32)


def _dot_nt(a, b):
    return lax.dot_general(a, b, (((1,), (1,)), ((), ())), preferred_element_type=F32)


def _sigmoid(x):
    return 1.0 / (1.0 + jnp.exp(-x))


def _rms_rows(x, g):
    ms = jnp.mean(x * x, axis=-1, keepdims=True)
    return x * lax.rsqrt(ms + RMS_EPS) * g


def _params(sem):
    return pltpu.CompilerParams(dimension_semantics=sem, vmem_limit_bytes=VMEM_LIMIT)


def _rope_kernel(pos_ref, invf_ref, cos_ref, sin_ref):
    ang = invf_ref[...] * pos_ref[...]
    cos_ref[...] = jnp.cos(ang)
    sin_ref[...] = jnp.sin(ang)


def _rope_tables(positions):
    n = positions.size
    tn = 512
    pos = positions.reshape(1, n).astype(F32)
    half_m = MLA_ROPE // 2
    inv_m = ROPE_THETA ** (-jnp.arange(half_m, dtype=F32) / half_m)
    inv_n = ROPE_THETA ** (-jnp.arange(NSA_HALF, dtype=F32) / NSA_HALF)
    invf = jnp.concatenate([inv_m, inv_n, jnp.zeros((16 - NSA_HALF,), F32)])
    rows = invf.shape[0]
    invf = jnp.broadcast_to(invf[:, None], (rows, tn))
    cos_t, sin_t = pl.pallas_call(
        _rope_kernel,
        grid=(n // tn,),
        in_specs=[pl.BlockSpec((1, tn), lambda i: (0, i)),
                  pl.BlockSpec((rows, tn), lambda i: (0, 0))],
        out_specs=[pl.BlockSpec((rows, tn), lambda i: (0, i)),
                   pl.BlockSpec((rows, tn), lambda i: (0, i))],
        out_shape=[jax.ShapeDtypeStruct((rows, n), F32)] * 2,
        compiler_params=_params(("parallel",)),
        name="rope_tables",
    )(pos, invf)
    return cos_t, sin_t


def _mla_in_kernel(x_ref, ng_ref, wa_ref, wzT_ref, gcq_ref, gckv_ref, wuqT_ref, gq_ref,
                   wuk_ref, wuvT_ref, gkn_ref, gkp_ref, ct_ref, st_ref, cosT_ref, sinT_ref,
                   qT_ref, k_ref, vT_ref, sz_ref):
    x = x_ref[0]
    n = x.shape[0]
    h = _rms_rows(x, ng_ref[...]).astype(BF16)
    pa = _dot(h, wa_ref[...])
    z = _dot_nt(wzT_ref[...], h)
    sz_ref[0, 0] = z * _sigmoid(z)

    cqn = _rms_rows(pa[:, :MLA_Q_LORA], gcq_ref[...]).astype(BF16)
    ckvn = _rms_rows(pa[:, MLA_Q_LORA:MLA_Q_LORA + MLA_KV_LORA], gckv_ref[...]).astype(BF16)
    kpe = pa[:, MLA_Q_LORA + MLA_KV_LORA:]

    qa = _dot_nt(wuqT_ref[...], cqn)
    cos = cosT_ref[...]
    sin = sinT_ref[...]
    gq = gq_ref[...]
    zeros32 = jnp.zeros((32, n), BF16)
    for hd in range(MLA_HEADS):
        blk = qa[hd * MLA_HD:(hd + 1) * MLA_HD]
        ss = jnp.sum(blk * blk, axis=0, keepdims=True)
        qn = blk * lax.rsqrt(ss * (1.0 / MLA_QK) + RMS_EPS) * gq
        x1 = qn[128:160]
        x2 = qn[192:224]
        qT_ref[0, hd, 0, 0:128, :] = qn[0:128].astype(BF16)
        qT_ref[0, hd, 0, 128:160, :] = (x1 * cos - x2 * sin).astype(BF16)
        qT_ref[0, hd, 0, 160:192, :] = zeros32
        qT_ref[0, hd, 0, 192:224, :] = (x2 * cos + x1 * sin).astype(BF16)
        qT_ref[0, hd, 0, 224:256, :] = zeros32

    kn = _dot(ckvn, wuk_ref[...])
    ss_pe = jnp.sum(kpe * kpe, axis=-1, keepdims=True)
    kpg = kpe * gkp_ref[...]
    prot = kpg * ct_ref[...] + pltpu.roll(kpg, 64, axis=1) * st_ref[...]
    gkn = gkn_ref[...]
    for hd in range(MLA_HEADS):
        kb = kn[:, hd * MLA_NOPE:(hd + 1) * MLA_NOPE]
        ss = jnp.sum(kb * kb, axis=-1, keepdims=True) + ss_pe
        r = lax.rsqrt(ss * (1.0 / MLA_QK) + RMS_EPS)
        k_ref[0, hd, :, 0:128] = (kb * r * gkn).astype(BF16)
        k_ref[0, hd, :, 128:256] = (prot * r).astype(BF16)

    va = _dot_nt(wuvT_ref[...], ckvn)
    ones = jnp.ones((ONES, n), BF16)
    for hd in range(MLA_HEADS):
        vT_ref[0, hd, 0, 0:MLA_V, :] = va[hd * MLA_V:(hd + 1) * MLA_V].astype(BF16)
        vT_ref[0, hd, 0, MLA_V:MLA_V + ONES, :] = ones


def _mla_in(x, ng, w, tabs):
    B, S, D = x.shape
    nt = S // TT
    r = TQ_MLA // TT
    assert TT == TKV and TQ_MLA % TT == 0
    full = lambda shape: pl.BlockSpec(shape, lambda b, t: (0,) * len(shape))
    tok = lambda b, t: (b * nt + t, 0)
    feat = lambda b, t: (0, b * nt + t)
    return pl.pallas_call(
        _mla_in_kernel,
        grid=(B, nt),
        in_specs=[
            pl.BlockSpec((1, TT, D), lambda b, t: (b, t, 0)),
            full((1, D)),
            full(w["wa"].shape), full(w["wzT"].shape),
            full((1, MLA_Q_LORA)), full((1, MLA_KV_LORA)),
            full(w["wuqT"].shape), full((MLA_HD, TT)),
            full(w["wuk"].shape), full(w["wuvT"].shape),
            full((1, MLA_NOPE)), full((1, LANE)),
            pl.BlockSpec((TT, LANE), tok), pl.BlockSpec((TT, LANE), tok),
            pl.BlockSpec((32, TT), feat), pl.BlockSpec((32, TT), feat),
        ],
        out_specs=[
            pl.BlockSpec((1, MLA_HEADS, 1, MLA_HD, TT), lambda b, t: (b, 0, t // r, 0, t % r)),
            pl.BlockSpec((1, MLA_HEADS, TT, MLA_HD), lambda b, t: (b, 0, t, 0)),
            pl.BlockSpec((1, MLA_HEADS, 1, MLA_V + ONES, TT), lambda b, t: (b, 0, t, 0, 0)),
            pl.BlockSpec((1, 1, MLA_HEADS * MLA_V, TT), lambda b, t: (b, t // r, 0, t % r)),
        ],
        out_shape=[
            jax.ShapeDtypeStruct((B, MLA_HEADS, S // TQ_MLA, MLA_HD, TQ_MLA), BF16),
            jax.ShapeDtypeStruct((B, MLA_HEADS, S, MLA_HD), BF16),
            jax.ShapeDtypeStruct((B, MLA_HEADS, nt, MLA_V + ONES, TT), BF16),
            jax.ShapeDtypeStruct((B, S // TQ_MLA, MLA_HEADS * MLA_V, TQ_MLA), F32),
        ],
        compiler_params=_params(("parallel", "parallel")),
        name="mla_in_proj",
    )(x, ng, w["wa"], w["wzT"], w["gcq"], w["gckv"], w["wuqT"], w["gq"], w["wuk"],
      w["wuvT"], w["gkn"], w["gkp"], tabs["ct_mla"], tabs["st_mla"], tabs["cosT_mla"],
      tabs["sinT_mla"])


def _softmax_step_ref(s_ref, carry, vtile):
    m, acc = carry
    n, w = s_ref.shape
    fold = lambda a, op: op(a.reshape(a.shape[0] // SUB, SUB, w), axis=0)
    mx = fold(s_ref[0:CHUNK, :], jnp.max)
    for r in range(1, n // CHUNK):
        mx = jnp.maximum(mx, fold(s_ref[r * CHUNK:(r + 1) * CHUNK, :], jnp.max))
    m_new = jnp.maximum(m, jnp.max(mx, axis=0, keepdims=True))
    alpha = jnp.exp2(m - m_new)
    pv = None
    for t in range(n // TKV):
        parts = []
        for r in range(TKV // CHUNK):
            lo = t * TKV + r * CHUNK
            parts.append(jnp.exp2(s_ref[lo:lo + CHUNK, :] - m_new).astype(BF16))
        d = _dot(vtile(t), jnp.concatenate(parts, axis=0))
        pv = d if pv is None else pv + d
    return m_new, alpha * acc + pv


def _sweep(scores, vtile, mask_last, n_full, carry, s_a, s_b):
    def step(src, dst, j, carry):
        dst[...] = scores(j + 1)
        return _softmax_step_ref(src, carry, vtile(j))

    def last(src, j, carry):
        src[...] = mask_last(src[...])
        return _softmax_step_ref(src, carry, vtile(j))

    def pair(t, carry):
        carry = step(s_a, s_b, 2 * t, carry)
        return step(s_b, s_a, 2 * t + 1, carry)

    s_a[...] = scores(0)
    carry = lax.fori_loop(0, n_full // 2, pair, carry)
    odd = lambda carry: last(s_b, n_full, step(s_a, s_b, n_full - 1, carry))
    even = lambda carry: last(s_a, n_full, carry)
    return lax.cond(n_full % 2 == 1, odd, even, carry)


def _mla_attn_kernel(qT_ref, k_ref, vT_ref, sz_ref, o_ref, s_a, s_b):
    nq, _, tq = qT_ref.shape[2:]
    nsub = TSTEP // TKV

    def causal(s):
        row = lax.broadcasted_iota(jnp.int32, s.shape, 0)
        col = lax.broadcasted_iota(jnp.int32, s.shape, 1)
        return jnp.where(row <= col, s, M_INIT)

    def qtile(i, _):
        q = qT_ref[0, 0, i]

        def scores(j):
            kt = k_ref[0, 0, pl.ds(pl.multiple_of(j * TSTEP, TSTEP), TSTEP), :]
            return _dot(kt, q)

        def vtile(j):
            return lambda r: vT_ref[0, 0, j * nsub + r]

        init = (jnp.full((1, tq), M_INIT, F32), jnp.zeros((MLA_V + ONES, tq), F32))
        _, acc = _sweep(scores, vtile, causal, i, init, s_a, s_b)
        o = acc[0:MLA_V] / acc[MLA_V:MLA_V + 1]
        o_ref[0, i] = (o * sz_ref[0, i]).astype(BF16)
        return 0

    lax.fori_loop(0, nq, qtile, 0)


def _mla_attn(qT, k, vT, sz):
    B, H, nq, _, tq = qT.shape
    S = k.shape[2]
    assert tq == TSTEP and TSTEP % TKV == 0
    per_bh = lambda shape: pl.BlockSpec((1, 1) + shape, lambda b, h: (b, h) + (0,) * len(shape))
    gate = pl.BlockSpec((1, nq, MLA_V, tq), lambda b, h: (b, 0, h, 0))
    return pl.pallas_call(
        _mla_attn_kernel,
        grid=(B, H),
        in_specs=[per_bh((nq, MLA_HD, tq)), per_bh((S, MLA_HD)),
                  per_bh((S // TKV, MLA_V + ONES, TKV)), gate],
        out_specs=gate,
        out_shape=jax.ShapeDtypeStruct((B, nq, H * MLA_V, tq), BF16),
        scratch_shapes=[pltpu.VMEM((TSTEP, tq), F32)] * 2,
        compiler_params=_params(("parallel", "parallel")),
        name="mla_attention",
    )(qT, k, vT, sz)


def _out_proj_kernel(og_ref, wT_ref, x_ref, o_ref):
    ntile = og_ref.shape[1]
    og = og_ref[0, 0] if ntile == 1 else jnp.concatenate([og_ref[0, r] for r in range(ntile)], axis=1)
    yT = _dot(wT_ref[...], og)
    o_ref[0] = x_ref[0] + yT.T


def _out_proj(og, wT, x):
    B, S, D = x.shape
    _, _, K, W = og.shape
    r = TO // W
    return pl.pallas_call(
        _out_proj_kernel,
        grid=(B, S // TO),
        in_specs=[
            pl.BlockSpec((1, r, K, W), lambda b, t: (b, t, 0, 0)),
            pl.BlockSpec((D, K), lambda b, t: (0, 0)),
            pl.BlockSpec((1, TO, D), lambda b, t: (b, t, 0)),
        ],
        out_specs=pl.BlockSpec((1, TO, D), lambda b, t: (b, t, 0)),
        out_shape=jax.ShapeDtypeStruct((B, S, D), F32),
        compiler_params=_params(("parallel", "parallel")),
        name="out_proj",
    )(og, wT, x)


def _nsa_in_kernel(x_ref, ng_ref, wtok_ref, wfT_ref, gq_ref, gk_ref, ct_ref, st_ref,
                   cosT_ref, sinT_ref,
                   qT_ref, ks_ref, kw_ref, kc_ref, vc_ref, vsT_ref, vwT_ref, g_ref, sz_ref):
    t = pl.program_id(1)
    x = x_ref[0]
    n = x.shape[0]
    tq = qT_ref.shape[-1]
    h = _rms_rows(x, ng_ref[...]).astype(BF16)
    G = NSA_GROUPS
    lanes = [slice(u * tq, (u + 1) * tq) for u in range(n // tq)]

    fa = _dot_nt(wfT_ref[...], h)
    cos = cosT_ref[...]
    sin = sinT_ref[...]
    gq = gq_ref[...]
    for hd in range(NSA_HEADS):
        blk = fa[hd * NSA_HD:(hd + 1) * NSA_HD]
        ss = jnp.sum(blk * blk, axis=0, keepdims=True)
        qn = blk * lax.rsqrt(ss * (1.0 / NSA_DK) + RMS_EPS) * gq
        x1 = qn[0:16]
        x2 = qn[64:80]
        r1 = (x1 * cos - x2 * sin).astype(BF16)
        r2 = (x2 * cos + x1 * sin).astype(BF16)
        qb = qn.astype(BF16)
        for u, sl in enumerate(lanes):
            qT_ref[0, hd, u, 0:16, :] = r1[:, sl]
            qT_ref[0, hd, u, 16:64, :] = qb[16:64, sl]
            qT_ref[0, hd, u, 64:80, :] = r2[:, sl]
            qT_ref[0, hd, u, 80:128, :] = qb[80:128, sl]
    off = NSA_HEADS * NSA_HD
    ones = jnp.ones((ONES, n), BF16)
    for ref in (vsT_ref, vwT_ref):
        for g in range(G):
            ref[0, g, 0, 0:NSA_DV, :] = fa[off + g * NSA_DV:off + (g + 1) * NSA_DV].astype(BF16)
            ref[0, g, 0, NSA_DV:NSA_DV + ONES, :] = ones
        off += G * NSA_DV
    sg = _sigmoid(fa[off:off + 16 * G])
    for g in range(G):
        for u, sl in enumerate(lanes):
            g_ref[0, g, u] = sg[g * 16:(g + 1) * 16, sl]
    off += 16 * G
    z = fa[off:off + NSA_HEADS * NSA_DV]
    sz = z * _sigmoid(z)
    for u, sl in enumerate(lanes):
        sz_ref[0, u] = sz[:, sl]

    pt = _dot(h, wtok_ref[...])
    ct = ct_ref[...]
    st = st_ref[...]
    nblk = ks_ref.shape[-1] - NSA_HD
    row = lax.broadcasted_iota(jnp.int32, (n, nblk), 0) + t * n
    col = lax.broadcasted_iota(jnp.int32, (n, nblk), 1)
    onehot = jnp.where((row // SEL_LEN) == col, 1.0, 0.0).astype(BF16)
    for br, ref in ((0, ks_ref), (1, kw_ref)):
        gk = gk_ref[br + 1:br + 2, :]
        for g in range(G):
            kb = pt[:, (br * G + g) * NSA_HD:(br * G + g + 1) * NSA_HD]
            ss = jnp.sum(kb * kb, axis=-1, keepdims=True)
            kn = kb * lax.rsqrt(ss * (1.0 / NSA_DK) + RMS_EPS) * gk
            kr = kn * ct + pltpu.roll(kn, 64, axis=1) * st
            ref[0, g, :, 0:NSA_HD] = kr.astype(BF16)
            if br == 0:
                ref[0, g, :, NSA_HD:NSA_HD + nblk] = onehot
    for g in range(G):
        kc_ref[0, g] = pt[:, (2 * G + g) * LANE:(2 * G + g + 1) * LANE]
        vc_ref[0, g] = pt[:, (3 * G + g) * LANE:(3 * G + g + 1) * LANE]


def _nsa_in(x, ng, w, tabs):
    B, S, D = x.shape
    nt = S // TT
    G = NSA_GROUPS
    nblk = S // SEL_LEN
    r = TT // TQ_NSA
    nq = S // TQ_NSA
    assert TT == TKV and TT % TQ_NSA == 0
    full = lambda shape: pl.BlockSpec(shape, lambda b, t: (0,) * len(shape))
    tok = lambda b, t: (b * nt + t, 0)
    feat = lambda b, t: (0, b * nt + t)
    tokmaj = lambda width: pl.BlockSpec((1, G, TT, width), lambda b, t: (b, 0, t, 0))
    vtile = pl.BlockSpec((1, G, 1, NSA_DV + ONES, TT), lambda b, t: (b, 0, t, 0, 0))
    vshape = jax.ShapeDtypeStruct((B, G, nt, NSA_DV + ONES, TT), BF16)
    return pl.pallas_call(
        _nsa_in_kernel,
        grid=(B, nt),
        in_specs=[
            pl.BlockSpec((1, TT, D), lambda b, t: (b, t, 0)),
            full((1, D)),
            full(w["wtok"].shape), full(w["wfT"].shape),
            full((NSA_HD, TT)), full((3, NSA_HD)),
            pl.BlockSpec((TT, LANE), tok), pl.BlockSpec((TT, LANE), tok),
            pl.BlockSpec((16, TT), feat), pl.BlockSpec((16, TT), feat),
        ],
        out_specs=[
            pl.BlockSpec((1, NSA_HEADS, r, NSA_HD, TQ_NSA), lambda b, t: (b, 0, t, 0, 0)),
            tokmaj(NSA_HD + nblk), tokmaj(NSA_HD), tokmaj(LANE), tokmaj(LANE),
            vtile, vtile,
            pl.BlockSpec((1, G, r, 16, TQ_NSA), lambda b, t: (b, 0, t, 0, 0)),
            pl.BlockSpec((1, r, NSA_HEADS * NSA_DV, TQ_NSA), lambda b, t: (b, t, 0, 0)),
        ],
        out_shape=[
            jax.ShapeDtypeStruct((B, NSA_HEADS, nq, NSA_HD, TQ_NSA), BF16),
            jax.ShapeDtypeStruct((B, G, S, NSA_HD + nblk), BF16),
            jax.ShapeDtypeStruct((B, G, S, NSA_HD), BF16),
            jax.ShapeDtypeStruct((B, G, S, LANE), F32),
            jax.ShapeDtypeStruct((B, G, S, LANE), F32),
            vshape, vshape,
            jax.ShapeDtypeStruct((B, G, nq, 16, TQ_NSA), F32),
            jax.ShapeDtypeStruct((B, nq, NSA_HEADS * NSA_DV, TQ_NSA), F32),
        ],
        compiler_params=_params(("parallel", "parallel")),
        name="nsa_in_proj",
    )(x, ng, w["wtok"], w["wfT"], w["gq"], w["gk"], tabs["ct_nsa"], tabs["st_nsa"],
      tabs["cosT_nsa"], tabs["sinT_nsa"])


def _nsa_cmp_kernel(kc_ref, vc_ref, pek_ref, pev_ref, w1k_ref, w2k_ref, w1v_ref, w2vT_ref,
                    gk_ref, ct_ref, st_ref, kcmp_ref, vcmpT_ref):
    nrow = kc_ref.shape[2]

    def pre(x, pe_ref, w1_ref):
        lo = _dot((x + pe_ref[0:1, :]).astype(BF16), w1_ref[0])
        hi = _dot((x + pe_ref[1:2, :]).astype(BF16), w1_ref[1])
        return lo + pltpu.roll(hi, nrow - 1, axis=0)

    a = pre(kc_ref[0, 0], pek_ref, w1k_ref)
    kc = _dot((a * _sigmoid(a)).astype(BF16), w2k_ref[...])
    ss = jnp.sum(kc * kc, axis=-1, keepdims=True)
    kn = kc * lax.rsqrt(ss * (1.0 / NSA_DK) + RMS_EPS) * gk_ref[0:1, :]
    kr = kn * ct_ref[0] + pltpu.roll(kn, 64, axis=1) * st_ref[0]
    kcmp_ref[0, 0] = kr.astype(BF16)

    a = pre(vc_ref[0, 0], pev_ref, w1v_ref)
    sv = (a * _sigmoid(a)).astype(BF16)
    vcmpT_ref[0, 0] = _dot_nt(w2vT_ref[...], sv).astype(BF16)


def _nsa_cmp(kc16, vc16, w, tabs):
    B, G, nrow, width = kc16.shape
    full = lambda shape: pl.BlockSpec(shape, lambda b, g: (0,) * len(shape))
    blk = pl.BlockSpec((1, 1, nrow, width), lambda b, g: (b, g, 0, 0))
    tab = pl.BlockSpec((1, nrow, LANE), lambda b, g: (b, 0, 0))
    return pl.pallas_call(
        _nsa_cmp_kernel,
        grid=(B, G),
        in_specs=[blk, blk, full((2, width)), full((2, width)),
                  full(w["w1k"].shape), full(w["w2k"].shape),
                  full(w["w1v"].shape), full(w["w2vT"].shape),
                  full((3, NSA_HD)), tab, tab],
        out_specs=[pl.BlockSpec((1, 1, nrow, NSA_HD), lambda b, g: (b, g, 0, 0)),
                   pl.BlockSpec((1, 1, NSA_DV, nrow), lambda b, g: (b, g, 0, 0))],
        out_shape=[jax.ShapeDtypeStruct((B, G, nrow, NSA_HD), BF16),
                   jax.ShapeDtypeStruct((B, G, NSA_DV, nrow), BF16)],
        compiler_params=_params(("parallel", "parallel")),
        name="nsa_compress",
    )(kc16, vc16, w["pek"], w["pev"], w["w1k"], w["w2k"], w["w1v"], w["w2vT"], w["gk"],
      tabs["ct_cmp"], tabs["st_cmp"])


def _tile4(a):
    return jnp.concatenate([a] * NSA_HPG, axis=1)


def _nsa_attn_kernel(qT_ref, ks_ref, vsT_ref, kw_ref, vwT_ref, kc_ref, vcT_ref, ovT_ref,
                     g_ref, sz_ref, o_ref, qaug_ref, s_a, s_b, *, k_sel, n_win_tiles):
    nq, _, tq = qT_ref.shape[2:]
    ncmp = kc_ref.shape[2]
    nblk = ovT_ref.shape[0]
    nsub = TSTEP // TKV
    wide = NSA_HPG * tq

    def qtile(i, _):
        q0 = i * tq
        for hh in range(NSA_HPG):
            qaug_ref[0:NSA_HD, hh * tq:(hh + 1) * tq] = qT_ref[0, hh, i]
        qT = qaug_ref[0:NSA_HD, :]
        tok = q0 + lax.broadcasted_iota(jnp.int32, (1, tq), 1)

        sc = _dot(kc_ref[0, 0], qT)
        nidx = lax.broadcasted_iota(jnp.int32, (ncmp, tq), 0)
        validc = (nidx * CMP_STRIDE + (CMP_LEN - 1)) <= tok
        sc = sc + _tile4(jnp.where(validc, 0.0, NEG))
        mc = jnp.max(sc, axis=0, keepdims=True)
        ec = jnp.exp2(sc - mc) * _tile4(jnp.where(validc, 1.0, 0.0))
        dc = jnp.sum(ec, axis=0, keepdims=True)
        pc = ec * (1.0 / jnp.where(dc > 0, dc, 1.0))
        o_c = _dot(vcT_ref[0, 0], pc.astype(BF16))
        psum = pc[:, 0:tq]
        for hh in range(1, NSA_HPG):
            psum = psum + pc[:, hh * tq:(hh + 1) * tq]
        imp = _dot(ovT_ref[...], psum.astype(BF16))

        w0 = jnp.maximum(q0 // TKV - (n_win_tiles - 1), 0)
        kwin = kw_ref[0, 0, pl.ds(pl.multiple_of(w0 * TKV, TKV), n_win_tiles * TKV), :]
        sw = _dot(kwin, qT)
        key = w0 * TKV + lax.broadcasted_iota(jnp.int32, (n_win_tiles * TKV, tq), 0)
        inband = (key <= tok) & (key > tok - WINDOW)
        sw = sw + _tile4(jnp.where(inband, 0.0, NEG))
        mw = jnp.max(sw, axis=0, keepdims=True)
        pw = jnp.exp2(sw - mw).astype(BF16)
        aw = _dot(vwT_ref[0, 0, w0], pw[0:TKV])
        for r in range(1, n_win_tiles):
            aw = aw + _dot(vwT_ref[0, 0, w0 + r], pw[r * TKV:(r + 1) * TKV])
        o_w = aw[0:NSA_DV] * (1.0 / aw[NSA_DV:NSA_DV + 1])

        bidx = lax.broadcasted_iota(jnp.int32, (nblk, tq), 0)
        cur = tok // SEL_LEN
        forced = (bidx == 0) | (bidx == cur) | (bidx == cur - 1)
        valid = (bidx * SEL_LEN) <= tok
        score = jnp.where(forced, SEL_FORCE, jnp.where(valid, imp, -1.0))
        grp = [score[r * SUB:(r + 1) * SUB] for r in range(nblk // SUB)]
        bsub = lax.broadcasted_iota(jnp.int32, (SUB, tq), 0)
        cnt = [jnp.zeros((SUB, tq), F32) for _ in grp]
        for j in range(nblk):
            rowj = score[j:j + 1, :]
            for r in range(nblk // SUB):
                if r * SUB > j:
                    cnt[r] = cnt[r] + jnp.where(rowj >= grp[r], 1.0, 0.0)
                elif r * SUB + SUB - 1 < j:
                    cnt[r] = cnt[r] + jnp.where(rowj > grp[r], 1.0, 0.0)
                else:
                    tie = jnp.where(bsub + r * SUB > j, 1.0, 0.0)
                    cnt[r] = cnt[r] + jnp.where(rowj > grp[r], 1.0, 0.0)
                    cnt[r] = cnt[r] + jnp.where(rowj == grp[r], tie, 0.0)
        cnt = jnp.concatenate(cnt, axis=0)
        csel = jnp.where(cnt < k_sel, 0.0, NEG).astype(BF16)
        for hh in range(NSA_HPG):
            qaug_ref[NSA_HD:NSA_HD + nblk, hh * tq:(hh + 1) * tq] = csel
        qaug = qaug_ref[...]

        def scores(j):
            kt = ks_ref[0, 0, pl.ds(pl.multiple_of(j * TSTEP, TSTEP), TSTEP), :]
            return _dot(kt, qaug)

        def vtile(j):
            return lambda r: vsT_ref[0, 0, j * nsub + r]

        nfull = q0 // TSTEP

        def causal(s):
            key = nfull * TSTEP + lax.broadcasted_iota(jnp.int32, (TSTEP, tq), 0)
            return s + _tile4(jnp.where(key <= tok, 0.0, NEG))

        init = (jnp.full((1, wide), M_INIT, F32), jnp.zeros((NSA_DV + ONES, wide), F32))
        _, acc = _sweep(scores, vtile, causal, nfull, init, s_a, s_b)
        o_s = acc[0:NSA_DV] * (1.0 / acc[NSA_DV:NSA_DV + 1])

        for hh in range(NSA_HPG):
            sl = slice(hh * tq, (hh + 1) * tq)
            g0 = g_ref[0, 0, i, hh:hh + 1, :]
            g1 = g_ref[0, 0, i, NSA_HPG + hh:NSA_HPG + hh + 1, :]
            g2 = g_ref[0, 0, i, 2 * NSA_HPG + hh:2 * NSA_HPG + hh + 1, :]
            o = g0 * o_c[:, sl] + g1 * o_s[:, sl] + g2 * o_w[:, sl]
            rows = slice(hh * NSA_DV, (hh + 1) * NSA_DV)
            o_ref[0, i, rows, :] = (o * sz_ref[0, i, rows, :]).astype(BF16)
        return 0

    lax.fori_loop(0, nq, qtile, 0)


def _nsa_attn(qT, ks, vsT, kw, vwT, kcmp, vcmpT, ovT, gates, sz):
    B, H, nq, _, tq = qT.shape
    G = NSA_GROUPS
    S = ks.shape[2]
    nblk = S // SEL_LEN
    ncmp = kcmp.shape[2]
    assert TKV % tq == 0 and TSTEP % TKV == 0 and S % TSTEP == 0
    n_win_tiles = (WINDOW - 1 + TKV - 1) // TKV + 1
    assert S // TKV >= n_win_tiles
    kern = functools.partial(_nsa_attn_kernel, k_sel=min(N_SELECT, nblk), n_win_tiles=n_win_tiles)
    per_bg = lambda shape: pl.BlockSpec((1, 1) + shape, lambda b, g: (b, g) + (0,) * len(shape))
    gate = pl.BlockSpec((1, nq, NSA_HPG * NSA_DV, tq), lambda b, g: (b, 0, g, 0))
    return pl.pallas_call(
        kern,
        grid=(B, G),
        in_specs=[
            pl.BlockSpec((1, NSA_HPG, nq, NSA_HD, tq), lambda b, g: (b, g, 0, 0, 0)),
            per_bg((S, NSA_HD + nblk)),
            per_bg((S // TKV, NSA_DV + ONES, TKV)),
            per_bg((S, NSA_HD)),
            per_bg((S // TKV, NSA_DV + ONES, TKV)),
            per_bg((ncmp, NSA_HD)),
            per_bg((NSA_DV, ncmp)),
            pl.BlockSpec((nblk, ncmp), lambda b, g: (0, 0)),
            per_bg((nq, 16, tq)),
            gate,
        ],
        out_specs=gate,
        out_shape=jax.ShapeDtypeStruct((B, nq, H * NSA_DV, tq), BF16),
        scratch_shapes=[pltpu.VMEM((NSA_HD + nblk, NSA_HPG * tq), BF16)]
        + [pltpu.VMEM((TSTEP, NSA_HPG * tq), F32)] * 2,
        compiler_params=_params(("parallel", "parallel")),
        name="nsa_attention",
    )(qT, ks, vsT, kw, vwT, kcmp, vcmpT, ovT, gates, sz)


def _nsa_perm():
    src = np.full((NSA_HD,), -1, np.int64)
    src[0:12] = np.arange(0, 12)
    src[12:16] = np.arange(24, 28)
    src[16:64] = np.arange(28, 76)
    src[64:76] = np.arange(12, 24)
    src[76:80] = np.arange(76, 80)
    src[80:96] = np.arange(80, 96)
    return src


def _take_cols(w, src):
    idx = np.where(src >= 0, src, 0)
    out = jnp.take(w, jnp.asarray(idx), axis=-1)
    return jnp.where(jnp.asarray(src >= 0), out, 0.0)


def _pad_last(w, width):
    return jnp.pad(w, [(0, 0)] * (w.ndim - 1) + [(0, width - w.shape[-1])])


def _prep_mla(w_in, g_cq, w_uq, g_ckv, w_ukv, g_q, g_k, w_out):
    D = w_in.shape[0]
    o1 = MLA_Q_LORA + MLA_KV_LORA
    z32 = jnp.zeros((D, 32), F32)
    wa = jnp.concatenate([w_in[:, :o1], w_in[:, o1:o1 + 32], z32, w_in[:, o1 + 32:o1 + 64], z32], axis=1)
    wq = w_uq.reshape(MLA_Q_LORA, MLA_HEADS, MLA_QK)
    zq = jnp.zeros((MLA_Q_LORA, MLA_HEADS, 32), F32)
    wq = jnp.concatenate([wq[:, :, 64:], wq[:, :, :32], zq, wq[:, :, 32:64], zq], axis=2)
    z1 = jnp.zeros((32,), F32)
    gq = jnp.concatenate([g_q[64:], g_q[:32], z1, g_q[32:64], z1]) * C_MLA
    wkv = w_ukv.reshape(MLA_KV_LORA, MLA_HEADS, MLA_NOPE + MLA_V)
    return {
        "wa": wa.astype(BF16),
        "wzT": w_in[:, o1 + MLA_ROPE:].T.astype(BF16),
        "gcq": g_cq.reshape(1, -1), "gckv": g_ckv.reshape(1, -1),
        "wuqT": wq.reshape(MLA_Q_LORA, MLA_HEADS * MLA_HD).T.astype(BF16),
        "gq": jnp.broadcast_to(gq[:, None], (MLA_HD, TT)),
        "wuk": wkv[:, :, :MLA_NOPE].reshape(MLA_KV_LORA, -1).astype(BF16),
        "wuvT": wkv[:, :, MLA_NOPE:].reshape(MLA_KV_LORA, -1).T.astype(BF16),
        "gkn": g_k[64:].reshape(1, -1),
        "gkp": jnp.concatenate([g_k[:32], z1, g_k[32:64], z1]).reshape(1, -1),
        "woT": w_out.T.astype(BF16),
    }


def _prep_nsa(w_in, g_q, g_k, pe_k, w1_k, w2_k, pe_v, w1_v, w2_v, w_out):
    D = w_in.shape[0]
    G = NSA_GROUPS
    src = _nsa_perm()
    offs = np.concatenate([[0], np.cumsum(NSA_SIZES)])
    part = lambda i: w_in[:, offs[i]:offs[i + 1]]
    q, kc, vc, ks, vs, kw, vw, gl, z = [part(i) for i in range(9)]
    perm_heads = lambda w, nh: _take_cols(w.reshape(D, nh, NSA_DK), src).reshape(D, nh * NSA_HD)
    pad_groups = lambda w, d: _pad_last(w.reshape(D, G, d), LANE).reshape(D, G * LANE)
    wtok = jnp.concatenate([perm_heads(ks, G), perm_heads(kw, G), pad_groups(kc, NSA_DK),
                            pad_groups(vc, NSA_DV)], axis=1)
    glr = gl.reshape(D, G, NSA_HPG, 3).transpose(0, 1, 3, 2).reshape(D, G, 3 * NSA_HPG)
    glr = _pad_last(glr, 16).reshape(D, G * 16)
    wf = jnp.concatenate([perm_heads(q, NSA_HEADS), vs, vw, glr, z], axis=1)
    halves = lambda w, d: _pad_last(w.reshape(2, CMP_LEN // 2, d, -1).transpose(0, 1, 3, 2), LANE) \
        .transpose(0, 1, 3, 2).reshape(2, (CMP_LEN // 2) * LANE, -1)
    pe_flat = lambda pe: _pad_last(pe, LANE).reshape(2, (CMP_LEN // 2) * LANE)
    return {
        "wtok": wtok.astype(BF16),
        "wfT": wf.T.astype(BF16),
        "gq": jnp.broadcast_to((_take_cols(g_q, src) * C_NSA)[:, None], (NSA_HD, TT)),
        "gk": _take_cols(g_k, src),
        "pek": pe_flat(pe_k), "pev": pe_flat(pe_v),
        "w1k": _pad_last(halves(w1_k, NSA_DK), LANE).astype(BF16),
        "w2k": _pad_last(_take_cols(w2_k, src).T, LANE).T.astype(BF16),
        "w1v": _pad_last(halves(w1_v, NSA_DV), LANE).astype(BF16),
        "w2vT": _pad_last(w2_v.T, LANE).astype(BF16),
        "woT": w_out.T.astype(BF16),
    }


def _prep_tables(cos_t, sin_t, B, S):
    cm, sm = cos_t[:32].T, sin_t[:32].T
    cn, sn = cos_t[32:48].T, sin_t[32:48].T
    n = cm.shape[0]
    one = lambda w: jnp.ones((n, w), F32)
    zero = lambda w: jnp.zeros((n, w), F32)
    ct_nsa = jnp.concatenate([cn, one(48), cn, one(48)], axis=1)
    st_nsa = jnp.concatenate([-sn, zero(48), sn, zero(48)], axis=1)
    ncmp = S // CMP_STRIDE
    last = CMP_LEN - 1
    pick = lambda t: jnp.pad(t.reshape(B, S, LANE)[:, last::CMP_STRIDE],
                             ((0, 0), (0, ncmp - (S - last + CMP_STRIDE - 1) // CMP_STRIDE), (0, 0)))
    return {
        "cosT_mla": cos_t[:32], "sinT_mla": sin_t[:32],
        "cosT_nsa": cos_t[32:48], "sinT_nsa": sin_t[32:48],
        "ct_mla": jnp.concatenate([cm, one(32), cm, one(32)], axis=1),
        "st_mla": jnp.concatenate([-sm, zero(32), sm, zero(32)], axis=1),
        "ct_nsa": ct_nsa, "st_nsa": st_nsa,
        "ct_cmp": pick(ct_nsa), "st_cmp": pick(st_nsa),
    }


def _overlap_T(S):
    ncmp = S // CMP_STRIDE
    nblk = S // SEL_LEN
    n_cmp = (S - CMP_LEN) // CMP_STRIDE + 1
    cs = np.arange(ncmp) * CMP_STRIDE
    ss = np.arange(nblk) * SEL_LEN
    ov = (cs[None, :] < ss[:, None] + SEL_LEN) & (cs[None, :] + CMP_LEN > ss[:, None])
    ov = ov & (np.arange(ncmp)[None, :] < n_cmp)
    return jnp.asarray(ov.astype(np.float32)).astype(BF16)


def _mla_layer(x, ng, w, tabs):
    qT, k, vT, sz = _mla_in(x, ng.reshape(1, -1), w, tabs)
    og = _mla_attn(qT, k, vT, sz)
    return _out_proj(og, w["woT"], x)


def _nsa_layer(x, ng, w, tabs, ovT):
    B, S, _ = x.shape
    qT, ks, kw, kc, vc, vsT, vwT, gates, sz = _nsa_in(x, ng.reshape(1, -1), w, tabs)
    ncmp = S // CMP_STRIDE
    kc16 = kc.reshape(B, NSA_GROUPS, ncmp, CMP_STRIDE * LANE)
    vc16 = vc.reshape(B, NSA_GROUPS, ncmp, CMP_STRIDE * LANE)
    kcmp, vcmpT = _nsa_cmp(kc16, vc16, w, tabs)
    og = _nsa_attn(qT, ks, vsT, kw, vwT, kcmp, vcmpT, ovT, gates, sz)
    return _out_proj(og, w["woT"], x)


def kernel(x, positions, norm_g, mla_w_in, mla_g_cq, mla_w_uq, mla_g_ckv, mla_w_ukv, mla_g_q, mla_g_k, mla_w_out, nsa_w_in, nsa_g_q, nsa_g_k, nsa_pe_k, nsa_w1_k, nsa_w2_k, nsa_pe_v, nsa_w1_v, nsa_w2_v, nsa_w_out):
    B, S, _ = x.shape
    cos_t, sin_t = _rope_tables(positions)
    tabs = _prep_tables(cos_t, sin_t, B, S)
    ovT = _overlap_T(S)
    for i in range(DEPTH):
        j = i // N_MIXERS
        if i % N_MIXERS == 0:
            w = _prep_mla(mla_w_in[j], mla_g_cq[j], mla_w_uq[j], mla_g_ckv[j], mla_w_ukv[j],
                          mla_g_q[j], mla_g_k[j], mla_w_out[j])
            x = _mla_layer(x, norm_g[i], w, tabs)
        else:
            w = _prep_nsa(nsa_w_in[j], nsa_g_q[j], nsa_g_k[j], nsa_pe_k[j], nsa_w1_k[j],
                          nsa_w2_k[j], nsa_pe_v[j], nsa_w1_v[j], nsa_w2_v[j], nsa_w_out[j])
            x = _nsa_layer(x, norm_g[i], w, tabs, ovT)
    return x
```

```python
import functools

import numpy as np
import jax
import jax.numpy as jnp
from jax import lax
from jax.experimental import pallas as pl
from jax.experimental.pallas import tpu as pltpu

F32 = jnp.float32
BF16 = jnp.bfloat16

D_MODEL = 1024
DEPTH = 4
N_MIXERS = 2
ROPE_THETA = 500000.0
RMS_EPS = 1e-6
MLA_HEADS = 16
MLA_NOPE = 128
MLA_ROPE = 64
MLA_V = 128
MLA_QK = MLA_NOPE + MLA_ROPE
MLA_Q_LORA = 384
MLA_KV_LORA = 256
MLA_HD = 256
NSA_HEADS = 16
NSA_GROUPS = 4
NSA_HPG = NSA_HEADS // NSA_GROUPS
NSA_DK = 96
NSA_DV = 64
NSA_ROT = NSA_DK // 4
NSA_HALF = NSA_ROT // 2
NSA_HD = 128
CMP_LEN = 32
CMP_STRIDE = 16
SEL_LEN = 64
N_SELECT = 16
WINDOW = 512
SEL_FORCE = 1e4
NSA_SIZES = (NSA_HEADS * NSA_DK,
             NSA_GROUPS * NSA_DK, NSA_GROUPS * NSA_DV,
             NSA_GROUPS * NSA_DK, NSA_GROUPS * NSA_DV,
             NSA_GROUPS * NSA_DK, NSA_GROUPS * NSA_DV,
             3 * NSA_HEADS, NSA_HEADS * NSA_DV)

LANE = 128
SUB = 8
TT = 256
TO = 512
TKV = 256
TSTEP = 512
TQ_MLA = 512
TQ_NSA = 128
TQ_SEL = 256
PRE_TILES = 2
CHUNK = 64
ONES = 16
NEG = -30000.0
M_INIT = -1e30
LOG2E = 1.4426950408889634
C_MLA = (MLA_QK ** -0.5) * LOG2E
C_NSA = (NSA_DK ** -0.5) * LOG2E
VMEM_LIMIT = 56 * 1024 * 1024


def _dot(a, b):
    return jnp.dot(a, b, preferred_element_type=F32)


def _dot_nt(a, b):
    return lax.dot_general(a, b, (((1,), (1,)), ((), ())), preferred_element_type=F32)


def _sigmoid(x):
    return 1.0 / (1.0 + jnp.exp(-x))


def _rms_rows(x, g):
    ms = jnp.mean(x * x, axis=-1, keepdims=True)
    return x * lax.rsqrt(ms + RMS_EPS) * g


def _params(sem):
    return pltpu.CompilerParams(dimension_semantics=sem, vmem_limit_bytes=VMEM_LIMIT)


def _rope_kernel(pos_ref, invf_ref, cos_ref, sin_ref):
    ang = invf_ref[...] * pos_ref[...]
    cos_ref[...] = jnp.cos(ang)
    sin_ref[...] = jnp.sin(ang)


def _rope_tables(positions):
    n = positions.size
    tn = 512
    pos = positions.reshape(1, n).astype(F32)
    half_m = MLA_ROPE // 2
    inv_m = ROPE_THETA ** (-jnp.arange(half_m, dtype=F32) / half_m)
    inv_n = ROPE_THETA ** (-jnp.arange(NSA_HALF, dtype=F32) / NSA_HALF)
    invf = jnp.concatenate([inv_m, inv_n, jnp.zeros((16 - NSA_HALF,), F32)])
    rows = invf.shape[0]
    invf = jnp.broadcast_to(invf[:, None], (rows, tn))
    cos_t, sin_t = pl.pallas_call(
        _rope_kernel,
        grid=(n // tn,),
        in_specs=[pl.BlockSpec((1, tn), lambda i: (0, i)),
                  pl.BlockSpec((rows, tn), lambda i: (0, 0))],
        out_specs=[pl.BlockSpec((rows, tn), lambda i: (0, i)),
                   pl.BlockSpec((rows, tn), lambda i: (0, i))],
        out_shape=[jax.ShapeDtypeStruct((rows, n), F32)] * 2,
        compiler_params=_params(("parallel",)),
        name="rope_tables",
    )(pos, invf)
    return cos_t, sin_t


def _mla_in_kernel(x_ref, ng_ref, wa_ref, wzT_ref, gcq_ref, gckv_ref, wuqT_ref, gq_ref,
                   wuk_ref, wuvT_ref, gkn_ref, gkp_ref, ct_ref, st_ref, cosT_ref, sinT_ref,
                   qT_ref, k_ref, vT_ref, sz_ref):
    x = x_ref[0]
    n = x.shape[0]
    h = _rms_rows(x, ng_ref[...]).astype(BF16)
    pa = _dot(h, wa_ref[...])
    z = _dot_nt(wzT_ref[...], h)
    sz_ref[0, 0] = z * _sigmoid(z)

    cqn = _rms_rows(pa[:, :MLA_Q_LORA], gcq_ref[...]).astype(BF16)
    ckvn = _rms_rows(pa[:, MLA_Q_LORA:MLA_Q_LORA + MLA_KV_LORA], gckv_ref[...]).astype(BF16)
    kpe = pa[:, MLA_Q_LORA + MLA_KV_LORA:]

    qa = _dot_nt(wuqT_ref[...], cqn)
    cos = cosT_ref[...]
    sin = sinT_ref[...]
    gq = gq_ref[...]
    zeros32 = jnp.zeros((32, n), BF16)
    for hd in range(MLA_HEADS):
        blk = qa[hd * MLA_HD:(hd + 1) * MLA_HD]
        ss = jnp.sum(blk * blk, axis=0, keepdims=True)
        qn = blk * lax.rsqrt(ss * (1.0 / MLA_QK) + RMS_EPS) * gq
        x1 = qn[128:160]
        x2 = qn[192:224]
        qT_ref[0, hd, 0, 0:128, :] = qn[0:128].astype(BF16)
        qT_ref[0, hd, 0, 128:160, :] = (x1 * cos - x2 * sin).astype(BF16)
        qT_ref[0, hd, 0, 160:192, :] = zeros32
        qT_ref[0, hd, 0, 192:224, :] = (x2 * cos + x1 * sin).astype(BF16)
        qT_ref[0, hd, 0, 224:256, :] = zeros32

    kn = _dot(ckvn, wuk_ref[...])
    ss_pe = jnp.sum(kpe * kpe, axis=-1, keepdims=True)
    kpg = kpe * gkp_ref[...]
    prot = kpg * ct_ref[...] + pltpu.roll(kpg, 64, axis=1) * st_ref[...]
    gkn = gkn_ref[...]
    for hd in range(MLA_HEADS):
        kb = kn[:, hd * MLA_NOPE:(hd + 1) * MLA_NOPE]
        ss = jnp.sum(kb * kb, axis=-1, keepdims=True) + ss_pe
        r = lax.rsqrt(ss * (1.0 / MLA_QK) + RMS_EPS)
        k_ref[0, hd, :, 0:128] = (kb * r * gkn).astype(BF16)
        k_ref[0, hd, :, 128:256] = (prot * r).astype(BF16)

    va = _dot_nt(wuvT_ref[...], ckvn)
    ones = jnp.ones((ONES, n), BF16)
    for hd in range(MLA_HEADS):
        vT_ref[0, hd, 0, 0:MLA_V, :] = va[hd * MLA_V:(hd + 1) * MLA_V].astype(BF16)
        vT_ref[0, hd, 0, MLA_V:MLA_V + ONES, :] = ones


def _mla_in(x, ng, w, tabs):
    B, S, D = x.shape
    nt = S // TT
    r = TQ_MLA // TT
    assert TT == TKV and TQ_MLA % TT == 0
    full = lambda shape: pl.BlockSpec(shape, lambda b, t: (0,) * len(shape))
    tok = lambda b, t: (b * nt + t, 0)
    feat = lambda b, t: (0, b * nt + t)
    return pl.pallas_call(
        _mla_in_kernel,
        grid=(B, nt),
        in_specs=[
            pl.BlockSpec((1, TT, D), lambda b, t: (b, t, 0)),
            full((1, D)),
            full(w["wa"].shape), full(w["wzT"].shape),
            full((1, MLA_Q_LORA)), full((1, MLA_KV_LORA)),
            full(w["wuqT"].shape), full((MLA_HD, TT)),
            full(w["wuk"].shape), full(w["wuvT"].shape),
            full((1, MLA_NOPE)), full((1, LANE)),
            pl.BlockSpec((TT, LANE), tok), pl.BlockSpec((TT, LANE), tok),
            pl.BlockSpec((32, TT), feat), pl.BlockSpec((32, TT), feat),
        ],
        out_specs=[
            pl.BlockSpec((1, MLA_HEADS, 1, MLA_HD, TT), lambda b, t: (b, 0, t // r, 0, t % r)),
            pl.BlockSpec((1, MLA_HEADS, TT, MLA_HD), lambda b, t: (b, 0, t, 0)),
            pl.BlockSpec((1, MLA_HEADS, 1, MLA_V + ONES, TT), lambda b, t: (b, 0, t, 0, 0)),
            pl.BlockSpec((1, 1, MLA_HEADS * MLA_V, TT), lambda b, t: (b, t // r, 0, t % r)),
        ],
        out_shape=[
            jax.ShapeDtypeStruct((B, MLA_HEADS, S // TQ_MLA, MLA_HD, TQ_MLA), BF16),
            jax.ShapeDtypeStruct((B, MLA_HEADS, S, MLA_HD), BF16),
            jax.ShapeDtypeStruct((B, MLA_HEADS, nt, MLA_V + ONES, TT), BF16),
            jax.ShapeDtypeStruct((B, S // TQ_MLA, MLA_HEADS * MLA_V, TQ_MLA), F32),
        ],
        compiler_params=_params(("parallel", "parallel")),
        name="mla_in_proj",
    )(x, ng, w["wa"], w["wzT"], w["gcq"], w["gckv"], w["wuqT"], w["gq"], w["wuk"],
      w["wuvT"], w["gkn"], w["gkp"], tabs["ct_mla"], tabs["st_mla"], tabs["cosT_mla"],
      tabs["sinT_mla"])


def _softmax_step_ref(s_ref, carry, vtile):
    m, acc = carry
    n, w = s_ref.shape
    fold = lambda a, op: op(a.reshape(a.shape[0] // SUB, SUB, w), axis=0)
    mx = fold(s_ref[0:CHUNK, :], jnp.max)
    for r in range(1, n // CHUNK):
        mx = jnp.maximum(mx, fold(s_ref[r * CHUNK:(r + 1) * CHUNK, :], jnp.max))
    m_new = jnp.maximum(m, jnp.max(mx, axis=0, keepdims=True))
    alpha = jnp.exp2(m - m_new)
    pv = None
    for t in range(n // TKV):
        parts = []
        for r in range(TKV // CHUNK):
            lo = t * TKV + r * CHUNK
            parts.append(jnp.exp2(s_ref[lo:lo + CHUNK, :] - m_new).astype(BF16))
        d = _dot(vtile(t), jnp.concatenate(parts, axis=0))
        pv = d if pv is None else pv + d
    return m_new, alpha * acc + pv


def _sweep(scores, vtile, mask_last, n_full, carry, s_a, s_b):
    def step(src, dst, j, carry):
        dst[...] = scores(j + 1)
        return _softmax_step_ref(src, carry, vtile(j))

    def last(src, j, carry):
        src[...] = mask_last(src[...])
        return _softmax_step_ref(src, carry, vtile(j))

    def pair(t, carry):
        carry = step(s_a, s_b, 2 * t, carry)
        return step(s_b, s_a, 2 * t + 1, carry)

    s_a[...] = scores(0)
    if isinstance(n_full, int):
        bufs = (s_a, s_b)
        for j in range(n_full):
            carry = step(bufs[j % 2], bufs[(j + 1) % 2], j, carry)
        return last(bufs[n_full % 2], n_full, carry)
    carry = lax.fori_loop(0, n_full // 2, pair, carry)
    odd = lambda carry: last(s_b, n_full, step(s_a, s_b, n_full - 1, carry))
    even = lambda carry: last(s_a, n_full, carry)
    return lax.cond(n_full % 2 == 1, odd, even, carry)


def _mla_attn_kernel(qT_ref, k_ref, vT_ref, sz_ref, o_ref, s_a, s_b):
    nq, _, tq = qT_ref.shape[2:]
    nsub = TSTEP // TKV

    def causal(s):
        row = lax.broadcasted_iota(jnp.int32, s.shape, 0)
        col = lax.broadcasted_iota(jnp.int32, s.shape, 1)
        return jnp.where(row <= col, s, M_INIT)

    def qtile(i, _):
        q = qT_ref[0, 0, i]

        def scores(j):
            kt = k_ref[0, 0, j * TSTEP:(j + 1) * TSTEP, :]
            return _dot(kt, q)

        def vtile(j):
            return lambda r: vT_ref[0, 0, j * nsub + r]

        init = (jnp.full((1, tq), M_INIT, F32), jnp.zeros((MLA_V + ONES, tq), F32))
        _, acc = _sweep(scores, vtile, causal, i, init, s_a, s_b)
        o = acc[0:MLA_V] / acc[MLA_V:MLA_V + 1]
        o_ref[0, i] = (o * sz_ref[0, i]).astype(BF16)
        return 0

    for i in range(nq):
        qtile(i, 0)


def _mla_attn(qT, k, vT, sz):
    B, H, nq, _, tq = qT.shape
    S = k.shape[2]
    assert tq == TSTEP and TSTEP % TKV == 0
    per_bh = lambda shape: pl.BlockSpec((1, 1) + shape, lambda b, h: (b, h) + (0,) * len(shape))
    gate = pl.BlockSpec((1, nq, MLA_V, tq), lambda b, h: (b, 0, h, 0))
    return pl.pallas_call(
        _mla_attn_kernel,
        grid=(B, H),
        in_specs=[per_bh((nq, MLA_HD, tq)), per_bh((S, MLA_HD)),
                  per_bh((S // TKV, MLA_V + ONES, TKV)), gate],
        out_specs=gate,
        out_shape=jax.ShapeDtypeStruct((B, nq, H * MLA_V, tq), BF16),
        scratch_shapes=[pltpu.VMEM((TSTEP, tq), F32)] * 2,
        compiler_params=_params(("parallel", "parallel")),
        name="mla_attention",
    )(qT, k, vT, sz)


def _out_proj_kernel(og_ref, wT_ref, x_ref, o_ref):
    ntile = og_ref.shape[1]
    og = og_ref[0, 0] if ntile == 1 else jnp.concatenate([og_ref[0, r] for r in range(ntile)], axis=1)
    yT = _dot(wT_ref[...], og)
    o_ref[0] = x_ref[0] + yT.T


def _out_proj(og, wT, x):
    B, S, D = x.shape
    _, _, K, W = og.shape
    r = TO // W
    return pl.pallas_call(
        _out_proj_kernel,
        grid=(B, S // TO),
        in_specs=[
            pl.BlockSpec((1, r, K, W), lambda b, t: (b, t, 0, 0)),
            pl.BlockSpec((D, K), lambda b, t: (0, 0)),
            pl.BlockSpec((1, TO, D), lambda b, t: (b, t, 0)),
        ],
        out_specs=pl.BlockSpec((1, TO, D), lambda b, t: (b, t, 0)),
        out_shape=jax.ShapeDtypeStruct((B, S, D), F32),
        compiler_params=_params(("parallel", "parallel")),
        name="out_proj",
    )(og, wT, x)


def _nsa_in_kernel(x_ref, ng_ref, wtok_ref, wfT_ref, gq_ref, gk_ref, ct_ref, st_ref,
                   cosT_ref, sinT_ref,
                   qT_ref, ks_ref, kw_ref, kc_ref, vc_ref, vsT_ref, vwT_ref, g_ref, sz_ref):
    t = pl.program_id(1)
    x = x_ref[0]
    n = x.shape[0]
    tq = qT_ref.shape[-1]
    h = _rms_rows(x, ng_ref[...]).astype(BF16)
    G = NSA_GROUPS
    lanes = [slice(u * tq, (u + 1) * tq) for u in range(n // tq)]

    fa = _dot_nt(wfT_ref[...], h)
    cos = cosT_ref[...]
    sin = sinT_ref[...]
    gq = gq_ref[...]
    for hd in range(NSA_HEADS):
        blk = fa[hd * NSA_HD:(hd + 1) * NSA_HD]
        ss = jnp.sum(blk * blk, axis=0, keepdims=True)
        qn = blk * lax.rsqrt(ss * (1.0 / NSA_DK) + RMS_EPS) * gq
        x1 = qn[0:16]
        x2 = qn[64:80]
        r1 = (x1 * cos - x2 * sin).astype(BF16)
        r2 = (x2 * cos + x1 * sin).astype(BF16)
        qb = qn.astype(BF16)
        for u, sl in enumerate(lanes):
            qT_ref[0, hd, u, 0:16, :] = r1[:, sl]
            qT_ref[0, hd, u, 16:64, :] = qb[16:64, sl]
            qT_ref[0, hd, u, 64:80, :] = r2[:, sl]
            qT_ref[0, hd, u, 80:128, :] = qb[80:128, sl]
    off = NSA_HEADS * NSA_HD
    ones = jnp.ones((ONES, n), BF16)
    for ref in (vsT_ref, vwT_ref):
        for g in range(G):
            ref[0, g, 0, 0:NSA_DV, :] = fa[off + g * NSA_DV:off + (g + 1) * NSA_DV].astype(BF16)
            ref[0, g, 0, NSA_DV:NSA_DV + ONES, :] = ones
        off += G * NSA_DV
    sg = _sigmoid(fa[off:off + 16 * G])
    for g in range(G):
        for u, sl in enumerate(lanes):
            g_ref[0, g, u] = sg[g * 16:(g + 1) * 16, sl]
    off += 16 * G
    z = fa[off:off + NSA_HEADS * NSA_DV]
    sz = z * _sigmoid(z)
    for u, sl in enumerate(lanes):
        sz_ref[0, u] = sz[:, sl]

    pt = _dot(h, wtok_ref[...])
    ct = ct_ref[...]
    st = st_ref[...]
    nblk = ks_ref.shape[-1] - NSA_HD
    row = lax.broadcasted_iota(jnp.int32, (n, nblk), 0) + t * n
    col = lax.broadcasted_iota(jnp.int32, (n, nblk), 1)
    onehot = jnp.where((row // SEL_LEN) == col, 1.0, 0.0).astype(BF16)
    for br, ref in ((0, ks_ref), (1, kw_ref)):
        gk = gk_ref[br + 1:br + 2, :]
        for g in range(G):
            kb = pt[:, (br * G + g) * NSA_HD:(br * G + g + 1) * NSA_HD]
            ss = jnp.sum(kb * kb, axis=-1, keepdims=True)
            kn = kb * lax.rsqrt(ss * (1.0 / NSA_DK) + RMS_EPS) * gk
            kr = kn * ct + pltpu.roll(kn, 64, axis=1) * st
            ref[0, g, :, 0:NSA_HD] = kr.astype(BF16)
            if br == 0:
                ref[0, g, :, NSA_HD:NSA_HD + nblk] = onehot
    for g in range(G):
        kc_ref[0, g] = pt[:, (2 * G + g) * LANE:(2 * G + g + 1) * LANE]
        vc_ref[0, g] = pt[:, (3 * G + g) * LANE:(3 * G + g + 1) * LANE]


def _nsa_in(x, ng, w, tabs):
    B, S, D = x.shape
    nt = S // TT
    G = NSA_GROUPS
    nblk = S // SEL_LEN
    r = TT // TQ_NSA
    nq = S // TQ_NSA
    assert TT == TKV and TT % TQ_NSA == 0
    full = lambda shape: pl.BlockSpec(shape, lambda b, t: (0,) * len(shape))
    tok = lambda b, t: (b * nt + t, 0)
    feat = lambda b, t: (0, b * nt + t)
    tokmaj = lambda width: pl.BlockSpec((1, G, TT, width), lambda b, t: (b, 0, t, 0))
    vtile = pl.BlockSpec((1, G, 1, NSA_DV + ONES, TT), lambda b, t: (b, 0, t, 0, 0))
    vshape = jax.ShapeDtypeStruct((B, G, nt, NSA_DV + ONES, TT), BF16)
    return pl.pallas_call(
        _nsa_in_kernel,
        grid=(B, nt),
        in_specs=[
            pl.BlockSpec((1, TT, D), lambda b, t: (b, t, 0)),
            full((1, D)),
            full(w["wtok"].shape), full(w["wfT"].shape),
            full((NSA_HD, TT)), full((3, NSA_HD)),
            pl.BlockSpec((TT, LANE), tok), pl.BlockSpec((TT, LANE), tok),
            pl.BlockSpec((16, TT), feat), pl.BlockSpec((16, TT), feat),
        ],
        out_specs=[
            pl.BlockSpec((1, NSA_HEADS, r, NSA_HD, TQ_NSA), lambda b, t: (b, 0, t, 0, 0)),
            tokmaj(NSA_HD + nblk), tokmaj(NSA_HD), tokmaj(LANE), tokmaj(LANE),
            vtile, vtile,
            pl.BlockSpec((1, G, r, 16, TQ_NSA), lambda b, t: (b, 0, t, 0, 0)),
            pl.BlockSpec((1, r, NSA_HEADS * NSA_DV, TQ_NSA), lambda b, t: (b, t, 0, 0)),
        ],
        out_shape=[
            jax.ShapeDtypeStruct((B, NSA_HEADS, nq, NSA_HD, TQ_NSA), BF16),
            jax.ShapeDtypeStruct((B, G, S, NSA_HD + nblk), BF16),
            jax.ShapeDtypeStruct((B, G, S, NSA_HD), BF16),
            jax.ShapeDtypeStruct((B, G, S, LANE), F32),
            jax.ShapeDtypeStruct((B, G, S, LANE), F32),
            vshape, vshape,
            jax.ShapeDtypeStruct((B, G, nq, 16, TQ_NSA), F32),
            jax.ShapeDtypeStruct((B, nq, NSA_HEADS * NSA_DV, TQ_NSA), F32),
        ],
        compiler_params=_params(("parallel", "parallel")),
        name="nsa_in_proj",
    )(x, ng, w["wtok"], w["wfT"], w["gq"], w["gk"], tabs["ct_nsa"], tabs["st_nsa"],
      tabs["cosT_nsa"], tabs["sinT_nsa"])


def _nsa_cmp_kernel(kc_ref, vc_ref, pek_ref, pev_ref, w1k_ref, w2k_ref, w1v_ref, w2vT_ref,
                    gk_ref, ct_ref, st_ref, kcmp_ref, vcmpT_ref):
    nrow = kc_ref.shape[2]

    def pre(x, pe_ref, w1_ref):
        lo = _dot((x + pe_ref[0:1, :]).astype(BF16), w1_ref[0])
        hi = _dot((x + pe_ref[1:2, :]).astype(BF16), w1_ref[1])
        return lo + pltpu.roll(hi, nrow - 1, axis=0)

    a = pre(kc_ref[0, 0], pek_ref, w1k_ref)
    kc = _dot((a * _sigmoid(a)).astype(BF16), w2k_ref[...])
    ss = jnp.sum(kc * kc, axis=-1, keepdims=True)
    kn = kc * lax.rsqrt(ss * (1.0 / NSA_DK) + RMS_EPS) * gk_ref[0:1, :]
    kr = kn * ct_ref[0] + pltpu.roll(kn, 64, axis=1) * st_ref[0]
    kcmp_ref[0, 0] = kr.astype(BF16)

    a = pre(vc_ref[0, 0], pev_ref, w1v_ref)
    sv = (a * _sigmoid(a)).astype(BF16)
    vcmpT_ref[0, 0] = _dot_nt(w2vT_ref[...], sv).astype(BF16)


def _nsa_cmp(kc16, vc16, w, tabs):
    B, G, nrow, width = kc16.shape
    full = lambda shape: pl.BlockSpec(shape, lambda b, g: (0,) * len(shape))
    blk = pl.BlockSpec((1, 1, nrow, width), lambda b, g: (b, g, 0, 0))
    tab = pl.BlockSpec((1, nrow, LANE), lambda b, g: (b, 0, 0))
    return pl.pallas_call(
        _nsa_cmp_kernel,
        grid=(B, G),
        in_specs=[blk, blk, full((2, width)), full((2, width)),
                  full(w["w1k"].shape), full(w["w2k"].shape),
                  full(w["w1v"].shape), full(w["w2vT"].shape),
                  full((3, NSA_HD)), tab, tab],
        out_specs=[pl.BlockSpec((1, 1, nrow, NSA_HD), lambda b, g: (b, g, 0, 0)),
                   pl.BlockSpec((1, 1, NSA_DV, nrow), lambda b, g: (b, g, 0, 0))],
        out_shape=[jax.ShapeDtypeStruct((B, G, nrow, NSA_HD), BF16),
                   jax.ShapeDtypeStruct((B, G, NSA_DV, nrow), BF16)],
        compiler_params=_params(("parallel", "parallel")),
        name="nsa_compress",
    )(kc16, vc16, w["pek"], w["pev"], w["w1k"], w["w2k"], w["w1v"], w["w2vT"], w["gk"],
      tabs["ct_cmp"], tabs["st_cmp"])


def _tile4(a):
    return jnp.concatenate([a] * NSA_HPG, axis=1)


def _nsa_pre_kernel(qT_ref, kw_ref, vwT_ref, kc_ref, vcT_ref, ovT_ref, g_ref,
                    csel_ref, ocw_ref, *, k_sel, n_win_tiles):
    ntile, _, tq = qT_ref.shape[2:]
    ncmp = kc_ref.shape[2]
    nblk = ovT_ref.shape[0]

    for u in range(ntile):
        q0 = (pl.program_id(2) * ntile + u) * tq
        qT = jnp.concatenate([qT_ref[0, hh, u] for hh in range(NSA_HPG)], axis=1)
        tok = q0 + lax.broadcasted_iota(jnp.int32, (1, tq), 1)

        sc = _dot(kc_ref[0, 0], qT)
        nidx = lax.broadcasted_iota(jnp.int32, (ncmp, tq), 0)
        validc = (nidx * CMP_STRIDE + (CMP_LEN - 1)) <= tok
        sc = sc + _tile4(jnp.where(validc, 0.0, NEG))
        mc = jnp.max(sc, axis=0, keepdims=True)
        ec = jnp.exp2(sc - mc) * _tile4(jnp.where(validc, 1.0, 0.0))
        dc = jnp.sum(ec, axis=0, keepdims=True)
        pc = ec * (1.0 / jnp.where(dc > 0, dc, 1.0))
        o_c = _dot(vcT_ref[0, 0], pc.astype(BF16))
        psum = pc[:, 0:tq]
        for hh in range(1, NSA_HPG):
            psum = psum + pc[:, hh * tq:(hh + 1) * tq]
        imp = _dot(ovT_ref[...], psum.astype(BF16))

        w0 = jnp.maximum(q0 // TKV - (n_win_tiles - 1), 0)
        kwin = kw_ref[0, 0, pl.ds(pl.multiple_of(w0 * TKV, TKV), n_win_tiles * TKV), :]
        sw = _dot(kwin, qT)
        key = w0 * TKV + lax.broadcasted_iota(jnp.int32, (n_win_tiles * TKV, tq), 0)
        inband = (key <= tok) & (key > tok - WINDOW)
        sw = sw + _tile4(jnp.where(inband, 0.0, NEG))
        mw = jnp.max(sw, axis=0, keepdims=True)
        pw = jnp.exp2(sw - mw).astype(BF16)
        aw = _dot(vwT_ref[0, 0, w0], pw[0:TKV])
        for r in range(1, n_win_tiles):
            aw = aw + _dot(vwT_ref[0, 0, w0 + r], pw[r * TKV:(r + 1) * TKV])
        o_w = aw[0:NSA_DV] * (1.0 / aw[NSA_DV:NSA_DV + 1])

        bidx = lax.broadcasted_iota(jnp.int32, (nblk, tq), 0)
        cur = tok // SEL_LEN
        forced = (bidx == 0) | (bidx == cur) | (bidx == cur - 1)
        valid = (bidx * SEL_LEN) <= tok
        score = jnp.where(forced, SEL_FORCE, jnp.where(valid, imp, -1.0))
        grp = [score[r * SUB:(r + 1) * SUB] for r in range(nblk // SUB)]
        bsub = lax.broadcasted_iota(jnp.int32, (SUB, tq), 0)
        cnt = [jnp.zeros((SUB, tq), F32) for _ in grp]
        for j in range(nblk):
            rowj = score[j:j + 1, :]
            for r in range(nblk // SUB):
                if r * SUB > j:
                    cnt[r] = cnt[r] + jnp.where(rowj >= grp[r], 1.0, 0.0)
                elif r * SUB + SUB - 1 < j:
                    cnt[r] = cnt[r] + jnp.where(rowj > grp[r], 1.0, 0.0)
                else:
                    tie = jnp.where(bsub + r * SUB > j, 1.0, 0.0)
                    cnt[r] = cnt[r] + jnp.where(rowj > grp[r], 1.0, 0.0)
                    cnt[r] = cnt[r] + jnp.where(rowj == grp[r], tie, 0.0)
        cnt = jnp.concatenate(cnt, axis=0)
        csel_ref[0, 0, u] = jnp.where(cnt < k_sel, 0.0, NEG).astype(BF16)

        for hh in range(NSA_HPG):
            sl = slice(hh * tq, (hh + 1) * tq)
            g0 = g_ref[0, 0, u, hh:hh + 1, :]
            g2 = g_ref[0, 0, u, 2 * NSA_HPG + hh:2 * NSA_HPG + hh + 1, :]
            ocw_ref[0, 0, u, hh * NSA_DV:(hh + 1) * NSA_DV, :] = g0 * o_c[:, sl] + g2 * o_w[:, sl]


def _nsa_pre(qT, kw, vwT, kcmp, vcmpT, ovT, gates):
    B, H, nq, _, tq = qT.shape
    G = NSA_GROUPS
    S = kw.shape[2]
    nblk = S // SEL_LEN
    ncmp = kcmp.shape[2]
    assert TKV % tq == 0 and nq % PRE_TILES == 0
    n_win_tiles = (WINDOW - 1 + TKV - 1) // TKV + 1
    assert S // TKV >= n_win_tiles
    kern = functools.partial(_nsa_pre_kernel, k_sel=min(N_SELECT, nblk), n_win_tiles=n_win_tiles)
    per_bg = lambda shape: pl.BlockSpec((1, 1) + shape, lambda b, g, i: (b, g) + (0,) * len(shape))
    tiles = lambda shape: pl.BlockSpec((1, 1, PRE_TILES) + shape, lambda b, g, i: (b, g, i, 0, 0))
    return pl.pallas_call(
        kern,
        grid=(B, G, nq // PRE_TILES),
        in_specs=[
            pl.BlockSpec((1, NSA_HPG, PRE_TILES, NSA_HD, tq), lambda b, g, i: (b, g, i, 0, 0)),
            per_bg((S, NSA_HD)),
            per_bg((S // TKV, NSA_DV + ONES, TKV)),
            per_bg((ncmp, NSA_HD)),
            per_bg((NSA_DV, ncmp)),
            pl.BlockSpec((nblk, ncmp), lambda b, g, i: (0, 0)),
            tiles((16, tq)),
        ],
        out_specs=[tiles((nblk, tq)), tiles((NSA_HPG * NSA_DV, tq))],
        out_shape=[jax.ShapeDtypeStruct((B, G, nq, nblk, tq), BF16),
                   jax.ShapeDtypeStruct((B, G, nq, NSA_HPG * NSA_DV, tq), F32)],
        compiler_params=_params(("parallel", "parallel", "arbitrary")),
        name="nsa_branches",
    )(qT, kw, vwT, kcmp, vcmpT, ovT, gates)


def _nsa_sel_kernel(qT_ref, csel_ref, ks_ref, vsT_ref, ocw_ref, g_ref, sz_ref, o_ref,
                    qaug_ref, s_a, s_b):
    nq, _, tq = qT_ref.shape[2:]
    nblk = csel_ref.shape[3]
    nsub = TSTEP // TKV
    per = TQ_SEL // tq
    wide = NSA_HPG * TQ_SEL

    def qtile(i, _):
        q0 = i * TQ_SEL
        for hh in range(NSA_HPG):
            for u in range(per):
                lanes = slice(hh * TQ_SEL + u * tq, hh * TQ_SEL + (u + 1) * tq)
                qaug_ref[0:NSA_HD, lanes] = qT_ref[0, hh, i * per + u]
                qaug_ref[NSA_HD:NSA_HD + nblk, lanes] = csel_ref[0, 0, i * per + u]
        qaug = qaug_ref[...]
        tok = q0 + lax.broadcasted_iota(jnp.int32, (1, TQ_SEL), 1)

        def scores(j):
            kt = ks_ref[0, 0, pl.ds(pl.multiple_of(j * TSTEP, TSTEP), TSTEP), :]
            return _dot(kt, qaug)

        def vtile(j):
            return lambda r: vsT_ref[0, 0, j * nsub + r]

        nfull = q0 // TSTEP

        def causal(s):
            key = nfull * TSTEP + lax.broadcasted_iota(jnp.int32, (TSTEP, TQ_SEL), 0)
            return s + _tile4(jnp.where(key <= tok, 0.0, NEG))

        init = (jnp.full((1, wide), M_INIT, F32), jnp.zeros((NSA_DV + ONES, wide), F32))
        _, acc = _sweep(scores, vtile, causal, nfull, init, s_a, s_b)
        o_s = acc[0:NSA_DV] * (1.0 / acc[NSA_DV:NSA_DV + 1])

        for hh in range(NSA_HPG):
            rows = slice(hh * NSA_DV, (hh + 1) * NSA_DV)
            for u in range(per):
                t = i * per + u
                lanes = slice(hh * TQ_SEL + u * tq, hh * TQ_SEL + (u + 1) * tq)
                g1 = g_ref[0, 0, t, NSA_HPG + hh:NSA_HPG + hh + 1, :]
                o = ocw_ref[0, 0, t, rows, :] + g1 * o_s[:, lanes]
                o_ref[0, t, rows, :] = (o * sz_ref[0, t, rows, :]).astype(BF16)
        return 0

    lax.fori_loop(0, nq // per, qtile, 0)


def _nsa_sel(qT, csel, ks, vsT, ocw, gates, sz):
    B, H, nq, _, tq = qT.shape
    G = NSA_GROUPS
    S = ks.shape[2]
    nblk = S // SEL_LEN
    assert TQ_SEL % tq == 0 and TSTEP % TQ_SEL == 0 and TSTEP % TKV == 0 and S % TSTEP == 0
    per_bg = lambda shape: pl.BlockSpec((1, 1) + shape, lambda b, g: (b, g) + (0,) * len(shape))
    gate = pl.BlockSpec((1, nq, NSA_HPG * NSA_DV, tq), lambda b, g: (b, 0, g, 0))
    return pl.pallas_call(
        _nsa_sel_kernel,
        grid=(B, G),
        in_specs=[
            pl.BlockSpec((1, NSA_HPG, nq, NSA_HD, tq), lambda b, g: (b, g, 0, 0, 0)),
            per_bg((nq, nblk, tq)),
            per_bg((S, NSA_HD + nblk)),
            per_bg((S // TKV, NSA_DV + ONES, TKV)),
            per_bg((nq, NSA_HPG * NSA_DV, tq)),
            per_bg((nq, 16, tq)),
            gate,
        ],
        out_specs=gate,
        out_shape=jax.ShapeDtypeStruct((B, nq, H * NSA_DV, tq), BF16),
        scratch_shapes=[pltpu.VMEM((NSA_HD + nblk, NSA_HPG * TQ_SEL), BF16)]
        + [pltpu.VMEM((TSTEP, NSA_HPG * TQ_SEL), F32)] * 2,
        compiler_params=_params(("parallel", "parallel")),
        name="nsa_selected",
    )(qT, csel, ks, vsT, ocw, gates, sz)


def _nsa_perm():
    src = np.full((NSA_HD,), -1, np.int64)
    src[0:12] = np.arange(0, 12)
    src[12:16] = np.arange(24, 28)
    src[16:64] = np.arange(28, 76)
    src[64:76] = np.arange(12, 24)
    src[76:80] = np.arange(76, 80)
    src[80:96] = np.arange(80, 96)
    return src


def _take_cols(w, src):
    idx = np.where(src >= 0, src, 0)
    out = jnp.take(w, jnp.asarray(idx), axis=-1)
    return jnp.where(jnp.asarray(src >= 0), out, 0.0)


def _pad_last(w, width):
    return jnp.pad(w, [(0, 0)] * (w.ndim - 1) + [(0, width - w.shape[-1])])


def _prep_mla(w_in, g_cq, w_uq, g_ckv, w_ukv, g_q, g_k, w_out):
    D = w_in.shape[0]
    o1 = MLA_Q_LORA + MLA_KV_LORA
    z32 = jnp.zeros((D, 32), F32)
    wa = jnp.concatenate([w_in[:, :o1], w_in[:, o1:o1 + 32], z32, w_in[:, o1 + 32:o1 + 64], z32], axis=1)
    wq = w_uq.reshape(MLA_Q_LORA, MLA_HEADS, MLA_QK)
    zq = jnp.zeros((MLA_Q_LORA, MLA_HEADS, 32), F32)
    wq = jnp.concatenate([wq[:, :, 64:], wq[:, :, :32], zq, wq[:, :, 32:64], zq], axis=2)
    z1 = jnp.zeros((32,), F32)
    gq = jnp.concatenate([g_q[64:], g_q[:32], z1, g_q[32:64], z1]) * C_MLA
    wkv = w_ukv.reshape(MLA_KV_LORA, MLA_HEADS, MLA_NOPE + MLA_V)
    return {
        "wa": wa.astype(BF16),
        "wzT": w_in[:, o1 + MLA_ROPE:].T.astype(BF16),
        "gcq": g_cq.reshape(1, -1), "gckv": g_ckv.reshape(1, -1),
        "wuqT": wq.reshape(MLA_Q_LORA, MLA_HEADS * MLA_HD).T.astype(BF16),
        "gq": jnp.broadcast_to(gq[:, None], (MLA_HD, TT)),
        "wuk": wkv[:, :, :MLA_NOPE].reshape(MLA_KV_LORA, -1).astype(BF16),
        "wuvT": wkv[:, :, MLA_NOPE:].reshape(MLA_KV_LORA, -1).T.astype(BF16),
        "gkn": g_k[64:].reshape(1, -1),
        "gkp": jnp.concatenate([g_k[:32], z1, g_k[32:64], z1]).reshape(1, -1),
        "woT": w_out.T.astype(BF16),
    }


def _prep_nsa(w_in, g_q, g_k, pe_k, w1_k, w2_k, pe_v, w1_v, w2_v, w_out):
    D = w_in.shape[0]
    G = NSA_GROUPS
    src = _nsa_perm()
    offs = np.concatenate([[0], np.cumsum(NSA_SIZES)])
    part = lambda i: w_in[:, offs[i]:offs[i + 1]]
    q, kc, vc, ks, vs, kw, vw, gl, z = [part(i) for i in range(9)]
    perm_heads = lambda w, nh: _take_cols(w.reshape(D, nh, NSA_DK), src).reshape(D, nh * NSA_HD)
    pad_groups = lambda w, d: _pad_last(w.reshape(D, G, d), LANE).reshape(D, G * LANE)
    wtok = jnp.concatenate([perm_heads(ks, G), perm_heads(kw, G), pad_groups(kc, NSA_DK),
                            pad_groups(vc, NSA_DV)], axis=1)
    glr = gl.reshape(D, G, NSA_HPG, 3).transpose(0, 1, 3, 2).reshape(D, G, 3 * NSA_HPG)
    glr = _pad_last(glr, 16).reshape(D, G * 16)
    wf = jnp.concatenate([perm_heads(q, NSA_HEADS), vs, vw, glr, z], axis=1)
    halves = lambda w, d: _pad_last(w.reshape(2, CMP_LEN // 2, d, -1).transpose(0, 1, 3, 2), LANE) \
        .transpose(0, 1, 3, 2).reshape(2, (CMP_LEN // 2) * LANE, -1)
    pe_flat = lambda pe: _pad_last(pe, LANE).reshape(2, (CMP_LEN // 2) * LANE)
    return {
        "wtok": wtok.astype(BF16),
        "wfT": wf.T.astype(BF16),
        "gq": jnp.broadcast_to((_take_cols(g_q, src) * C_NSA)[:, None], (NSA_HD, TT)),
        "gk": _take_cols(g_k, src),
        "pek": pe_flat(pe_k), "pev": pe_flat(pe_v),
        "w1k": _pad_last(halves(w1_k, NSA_DK), LANE).astype(BF16),
        "w2k": _pad_last(_take_cols(w2_k, src).T, LANE).T.astype(BF16),
        "w1v": _pad_last(halves(w1_v, NSA_DV), LANE).astype(BF16),
        "w2vT": _pad_last(w2_v.T, LANE).astype(BF16),
        "woT": w_out.T.astype(BF16),
    }


def _prep_tables(cos_t, sin_t, B, S):
    cm, sm = cos_t[:32].T, sin_t[:32].T
    cn, sn = cos_t[32:48].T, sin_t[32:48].T
    n = cm.shape[0]
    one = lambda w: jnp.ones((n, w), F32)
    zero = lambda w: jnp.zeros((n, w), F32)
    ct_nsa = jnp.concatenate([cn, one(48), cn, one(48)], axis=1)
    st_nsa = jnp.concatenate([-sn, zero(48), sn, zero(48)], axis=1)
    ncmp = S // CMP_STRIDE
    last = CMP_LEN - 1
    pick = lambda t: jnp.pad(t.reshape(B, S, LANE)[:, last::CMP_STRIDE],
                             ((0, 0), (0, ncmp - (S - last + CMP_STRIDE - 1) // CMP_STRIDE), (0, 0)))
    return {
        "cosT_mla": cos_t[:32], "sinT_mla": sin_t[:32],
        "cosT_nsa": cos_t[32:48], "sinT_nsa": sin_t[32:48],
        "ct_mla": jnp.concatenate([cm, one(32), cm, one(32)], axis=1),
        "st_mla": jnp.concatenate([-sm, zero(32), sm, zero(32)], axis=1),
        "ct_nsa": ct_nsa, "st_nsa": st_nsa,
        "ct_cmp": pick(ct_nsa), "st_cmp": pick(st_nsa),
    }


def _overlap_T(S):
    ncmp = S // CMP_STRIDE
    nblk = S // SEL_LEN
    n_cmp = (S - CMP_LEN) // CMP_STRIDE + 1
    cs = np.arange(ncmp) * CMP_STRIDE
    ss = np.arange(nblk) * SEL_LEN
    ov = (cs[None, :] < ss[:, None] + SEL_LEN) & (cs[None, :] + CMP_LEN > ss[:, None])
    ov = ov & (np.arange(ncmp)[None, :] < n_cmp)
    return jnp.asarray(ov.astype(np.float32)).astype(BF16)


def _mla_layer(x, ng, w, tabs):
    qT, k, vT, sz = _mla_in(x, ng.reshape(1, -1), w, tabs)
    og = _mla_attn(qT, k, vT, sz)
    return _out_proj(og, w["woT"], x)


def _nsa_layer(x, ng, w, tabs, ovT):
    B, S, _ = x.shape
    qT, ks, kw, kc, vc, vsT, vwT, gates, sz = _nsa_in(x, ng.reshape(1, -1), w, tabs)
    ncmp = S // CMP_STRIDE
    kc16 = kc.reshape(B, NSA_GROUPS, ncmp, CMP_STRIDE * LANE)
    vc16 = vc.reshape(B, NSA_GROUPS, ncmp, CMP_STRIDE * LANE)
    kcmp, vcmpT = _nsa_cmp(kc16, vc16, w, tabs)
    csel, ocw = _nsa_pre(qT, kw, vwT, kcmp, vcmpT, ovT, gates)
    og = _nsa_sel(qT, csel, ks, vsT, ocw, gates, sz)
    return _out_proj(og, w["woT"], x)


def kernel(x, positions, norm_g, mla_w_in, mla_g_cq, mla_w_uq, mla_g_ckv, mla_w_ukv, mla_g_q, mla_g_k, mla_w_out, nsa_w_in, nsa_g_q, nsa_g_k, nsa_pe_k, nsa_w1_k, nsa_w2_k, nsa_pe_v, nsa_w1_v, nsa_w2_v, nsa_w_out):
    B, S, _ = x.shape
    cos_t, sin_t = _rope_tables(positions)
    tabs = _prep_tables(cos_t, sin_t, B, S)
    ovT = _overlap_T(S)
    for i in range(DEPTH):
        j = i // N_MIXERS
        if i % N_MIXERS == 0:
            w = _prep_mla(mla_w_in[j], mla_g_cq[j], mla_w_uq[j], mla_g_ckv[j], mla_w_ukv[j],
                          mla_g_q[j], mla_g_k[j], mla_w_out[j])
            x = _mla_layer(x, norm_g[i], w, tabs)
        else:
            w = _prep_nsa(nsa_w_in[j], nsa_g_q[j], nsa_g_k[j], nsa_pe_k[j], nsa_w1_k[j],
                          nsa_w2_k[j], nsa_pe_v[j], nsa_w1_v[j], nsa_w2_v[j], nsa_w_out[j])
            x = _nsa_layer(x, norm_g[i], w, tabs, ovT)
    return x
```

```python
import functools

import numpy as np
import jax
import jax.numpy as jnp
from jax import lax
from jax.experimental import pallas as pl
from jax.experimental.pallas import tpu as pltpu

F32 = jnp.float32
BF16 = jnp.bfloat16

D_MODEL = 1024
DEPTH = 4
N_MIXERS = 2
ROPE_THETA = 500000.0
RMS_EPS = 1e-6
MLA_HEADS = 16
MLA_NOPE = 128
MLA_ROPE = 64
MLA_V = 128
MLA_QK = MLA_NOPE + MLA_ROPE
MLA_Q_LORA = 384
MLA_KV_LORA = 256
MLA_HD = 256
NSA_HEADS = 16
NSA_GROUPS = 4
NSA_HPG = NSA_HEADS // NSA_GROUPS
NSA_DK = 96
NSA_DV = 64
NSA_ROT = NSA_DK // 4
NSA_HALF = NSA_ROT // 2
NSA_HD = 128
CMP_LEN = 32
CMP_STRIDE = 16
SEL_LEN = 64
N_SELECT = 16
WINDOW = 512
SEL_FORCE = 1e4
NSA_SIZES = (NSA_HEADS * NSA_DK,
             NSA_GROUPS * NSA_DK, NSA_GROUPS * NSA_DV,
             NSA_GROUPS * NSA_DK, NSA_GROUPS * NSA_DV,
             NSA_GROUPS * NSA_DK, NSA_GROUPS * NSA_DV,
             3 * NSA_HEADS, NSA_HEADS * NSA_DV)

LANE = 128
SUB = 8
TT = 256
TO = 512
TKV = 256
TSTEP = 512
TQ_MLA = 512
TQ_NSA = 128
TQ_SEL = 256
PRE_TILES = 2
RANK_CLASSES = 3
CHUNK = 64
ONES = 16
NEG = -30000.0
M_INIT = -1e30
LOG2E = 1.4426950408889634
C_MLA = (MLA_QK ** -0.5) * LOG2E
C_NSA = (NSA_DK ** -0.5) * LOG2E
VMEM_LIMIT = 56 * 1024 * 1024


def _dot(a, b):
    return jnp.dot(a, b, preferred_element_type=F32)


def _dot_nt(a, b):
    return lax.dot_general(a, b, (((1,), (1,)), ((), ())), preferred_element_type=F32)


def _sigmoid(x):
    return 1.0 / (1.0 + jnp.exp(-x))


def _rms_rows(x, g):
    ms = jnp.mean(x * x, axis=-1, keepdims=True)
    return x * lax.rsqrt(ms + RMS_EPS) * g


def _params(sem):
    return pltpu.CompilerParams(dimension_semantics=sem, vmem_limit_bytes=VMEM_LIMIT)


def _rope_kernel(pos_ref, invf_ref, cos_ref, sin_ref):
    ang = invf_ref[...] * pos_ref[...]
    cos_ref[...] = jnp.cos(ang)
    sin_ref[...] = jnp.sin(ang)


def _rope_tables(positions):
    n = positions.size
    tn = 512
    pos = positions.reshape(1, n).astype(F32)
    half_m = MLA_ROPE // 2
    inv_m = ROPE_THETA ** (-jnp.arange(half_m, dtype=F32) / half_m)
    inv_n = ROPE_THETA ** (-jnp.arange(NSA_HALF, dtype=F32) / NSA_HALF)
    invf = jnp.concatenate([inv_m, inv_n, jnp.zeros((16 - NSA_HALF,), F32)])
    rows = invf.shape[0]
    invf = jnp.broadcast_to(invf[:, None], (rows, tn))
    cos_t, sin_t = pl.pallas_call(
        _rope_kernel,
        grid=(n // tn,),
        in_specs=[pl.BlockSpec((1, tn), lambda i: (0, i)),
                  pl.BlockSpec((rows, tn), lambda i: (0, 0))],
        out_specs=[pl.BlockSpec((rows, tn), lambda i: (0, i)),
                   pl.BlockSpec((rows, tn), lambda i: (0, i))],
        out_shape=[jax.ShapeDtypeStruct((rows, n), F32)] * 2,
        compiler_params=_params(("parallel",)),
        name="rope_tables",
    )(pos, invf)
    return cos_t, sin_t


def _mla_in_kernel(x_ref, ng_ref, wa_ref, wzT_ref, gcq_ref, gckv_ref, wuqT_ref, gq_ref,
                   wuk_ref, wuvT_ref, gkn_ref, gkp_ref, ct_ref, st_ref, cosT_ref, sinT_ref,
                   qT_ref, k_ref, vT_ref, sz_ref):
    x = x_ref[0]
    n = x.shape[0]
    h = _rms_rows(x, ng_ref[...]).astype(BF16)
    pa = _dot(h, wa_ref[...])
    z = _dot_nt(wzT_ref[...], h)
    sz_ref[0, 0] = z * _sigmoid(z)

    cqn = _rms_rows(pa[:, :MLA_Q_LORA], gcq_ref[...]).astype(BF16)
    ckvn = _rms_rows(pa[:, MLA_Q_LORA:MLA_Q_LORA + MLA_KV_LORA], gckv_ref[...]).astype(BF16)
    kpe = pa[:, MLA_Q_LORA + MLA_KV_LORA:]

    qa = _dot_nt(wuqT_ref[...], cqn)
    cos = cosT_ref[...]
    sin = sinT_ref[...]
    gq = gq_ref[...]
    zeros32 = jnp.zeros((32, n), BF16)
    for hd in range(MLA_HEADS):
        blk = qa[hd * MLA_HD:(hd + 1) * MLA_HD]
        ss = jnp.sum(blk * blk, axis=0, keepdims=True)
        qn = blk * lax.rsqrt(ss * (1.0 / MLA_QK) + RMS_EPS) * gq
        x1 = qn[128:160]
        x2 = qn[192:224]
        qT_ref[0, hd, 0, 0:128, :] = qn[0:128].astype(BF16)
        qT_ref[0, hd, 0, 128:160, :] = (x1 * cos - x2 * sin).astype(BF16)
        qT_ref[0, hd, 0, 160:192, :] = zeros32
        qT_ref[0, hd, 0, 192:224, :] = (x2 * cos + x1 * sin).astype(BF16)
        qT_ref[0, hd, 0, 224:256, :] = zeros32

    kn = _dot(ckvn, wuk_ref[...])
    ss_pe = jnp.sum(kpe * kpe, axis=-1, keepdims=True)
    kpg = kpe * gkp_ref[...]
    prot = kpg * ct_ref[...] + pltpu.roll(kpg, 64, axis=1) * st_ref[...]
    gkn = gkn_ref[...]
    for hd in range(MLA_HEADS):
        kb = kn[:, hd * MLA_NOPE:(hd + 1) * MLA_NOPE]
        ss = jnp.sum(kb * kb, axis=-1, keepdims=True) + ss_pe
        r = lax.rsqrt(ss * (1.0 / MLA_QK) + RMS_EPS)
        k_ref[0, hd, :, 0:128] = (kb * r * gkn).astype(BF16)
        k_ref[0, hd, :, 128:256] = (prot * r).astype(BF16)

    va = _dot_nt(wuvT_ref[...], ckvn)
    ones = jnp.ones((ONES, n), BF16)
    for hd in range(MLA_HEADS):
        vT_ref[0, hd, 0, 0:MLA_V, :] = va[hd * MLA_V:(hd + 1) * MLA_V].astype(BF16)
        vT_ref[0, hd, 0, MLA_V:MLA_V + ONES, :] = ones


def _mla_in(x, ng, w, tabs):
    B, S, D = x.shape
    nt = S // TT
    r = TQ_MLA // TT
    assert TT == TKV and TQ_MLA % TT == 0
    full = lambda shape: pl.BlockSpec(shape, lambda b, t: (0,) * len(shape))
    tok = lambda b, t: (b * nt + t, 0)
    feat = lambda b, t: (0, b * nt + t)
    return pl.pallas_call(
        _mla_in_kernel,
        grid=(B, nt),
        in_specs=[
            pl.BlockSpec((1, TT, D), lambda b, t: (b, t, 0)),
            full((1, D)),
            full(w["wa"].shape), full(w["wzT"].shape),
            full((1, MLA_Q_LORA)), full((1, MLA_KV_LORA)),
            full(w["wuqT"].shape), full((MLA_HD, TT)),
            full(w["wuk"].shape), full(w["wuvT"].shape),
            full((1, MLA_NOPE)), full((1, LANE)),
            pl.BlockSpec((TT, LANE), tok), pl.BlockSpec((TT, LANE), tok),
            pl.BlockSpec((32, TT), feat), pl.BlockSpec((32, TT), feat),
        ],
        out_specs=[
            pl.BlockSpec((1, MLA_HEADS, 1, MLA_HD, TT), lambda b, t: (b, 0, t // r, 0, t % r)),
            pl.BlockSpec((1, MLA_HEADS, TT, MLA_HD), lambda b, t: (b, 0, t, 0)),
            pl.BlockSpec((1, MLA_HEADS, 1, MLA_V + ONES, TT), lambda b, t: (b, 0, t, 0, 0)),
            pl.BlockSpec((1, 1, MLA_HEADS * MLA_V, TT), lambda b, t: (b, t // r, 0, t % r)),
        ],
        out_shape=[
            jax.ShapeDtypeStruct((B, MLA_HEADS, S // TQ_MLA, MLA_HD, TQ_MLA), BF16),
            jax.ShapeDtypeStruct((B, MLA_HEADS, S, MLA_HD), BF16),
            jax.ShapeDtypeStruct((B, MLA_HEADS, nt, MLA_V + ONES, TT), BF16),
            jax.ShapeDtypeStruct((B, S // TQ_MLA, MLA_HEADS * MLA_V, TQ_MLA), F32),
        ],
        compiler_params=_params(("parallel", "parallel")),
        name="mla_in_proj",
    )(x, ng, w["wa"], w["wzT"], w["gcq"], w["gckv"], w["wuqT"], w["gq"], w["wuk"],
      w["wuvT"], w["gkn"], w["gkp"], tabs["ct_mla"], tabs["st_mla"], tabs["cosT_mla"],
      tabs["sinT_mla"])


def _softmax_step_ref(s_ref, carry, vtile=None, segs=None, bias=None):
    m, acc = carry
    n, w = s_ref.shape
    segs = segs or [(t * TKV, (t + 1) * TKV, functools.partial(vtile, t)) for t in range(n // TKV)]
    bias = bias or {}
    fold = lambda a, op: op(a.reshape(a.shape[0] // SUB, SUB, w), axis=0)
    mx = None
    for lo in range(0, n, CHUNK):
        blk = s_ref[lo:lo + CHUNK, :]
        if lo in bias:
            blk = bias[lo](blk) if callable(bias[lo]) else blk + bias[lo]
            s_ref[lo:lo + CHUNK, :] = blk
        mx = fold(blk, jnp.max) if mx is None else jnp.maximum(mx, fold(blk, jnp.max))
    m_new = jnp.maximum(m, jnp.max(mx, axis=0, keepdims=True))
    alpha = jnp.exp2(m - m_new)
    pv = None
    for seg_lo, seg_hi, value_tile in segs:
        parts = [jnp.exp2(s_ref[lo:lo + CHUNK, :] - m_new).astype(BF16)
                 for lo in range(seg_lo, seg_hi, CHUNK)]
        d = _dot(value_tile(), jnp.concatenate(parts, axis=0))
        pv = d if pv is None else pv + d
    return m_new, alpha * acc + pv


def _sweep(scores, vtile, last_bias, last_rows, n_full, carry, s_a, s_b):
    def step(src, dst, j, carry):
        dst[...] = scores(j + 1)
        return _softmax_step_ref(src, carry, vtile(j))

    def last(src, j, carry):
        return _softmax_step_ref(src.at[0:last_rows], carry, vtile(j), bias=last_bias)

    def pair(t, carry):
        carry = step(s_a, s_b, 2 * t, carry)
        return step(s_b, s_a, 2 * t + 1, carry)

    s_a[...] = scores(0)
    if isinstance(n_full, int):
        bufs = (s_a, s_b)
        for j in range(n_full):
            carry = step(bufs[j % 2], bufs[(j + 1) % 2], j, carry)
        return last(bufs[n_full % 2], n_full, carry)
    carry = lax.fori_loop(0, n_full // 2, pair, carry)
    odd = lambda carry: last(s_b, n_full, step(s_a, s_b, n_full - 1, carry))
    even = lambda carry: last(s_a, n_full, carry)
    return lax.cond(n_full % 2 == 1, odd, even, carry)


def _mla_attn_kernel(qT_ref, k_ref, vT_ref, sz_ref, o_ref, s_a, s_b):
    nq, _, tq = qT_ref.shape[2:]
    nsub = TSTEP // TKV

    row = lax.broadcasted_iota(jnp.int32, (CHUNK, tq), 0)
    col = lax.broadcasted_iota(jnp.int32, (CHUNK, tq), 1)
    causal = {lo: functools.partial(lambda lo, blk: jnp.where(lo + row <= col, blk, NEG), lo)
              for lo in range(0, TSTEP, CHUNK)}

    def qtile(i, _):
        q = qT_ref[0, 0, i]

        def scores(j):
            kt = k_ref[0, 0, j * TSTEP:(j + 1) * TSTEP, :]
            return _dot(kt, q)

        def vtile(j):
            return lambda r: vT_ref[0, 0, j * nsub + r]

        init = (jnp.full((1, tq), M_INIT, F32), jnp.zeros((MLA_V + ONES, tq), F32))
        _, acc = _sweep(scores, vtile, causal, TSTEP, i, init, s_a, s_b)
        o = acc[0:MLA_V] / acc[MLA_V:MLA_V + 1]
        o_ref[0, i] = (o * sz_ref[0, i]).astype(BF16)
        return 0

    for i in range(nq):
        qtile(i, 0)


def _mla_attn(qT, k, vT, sz):
    B, H, nq, _, tq = qT.shape
    S = k.shape[2]
    assert tq == TSTEP and TSTEP % TKV == 0
    per_bh = lambda shape: pl.BlockSpec((1, 1) + shape, lambda b, h: (b, h) + (0,) * len(shape))
    gate = pl.BlockSpec((1, nq, MLA_V, tq), lambda b, h: (b, 0, h, 0))
    return pl.pallas_call(
        _mla_attn_kernel,
        grid=(B, H),
        in_specs=[per_bh((nq, MLA_HD, tq)), per_bh((S, MLA_HD)),
                  per_bh((S // TKV, MLA_V + ONES, TKV)), gate],
        out_specs=gate,
        out_shape=jax.ShapeDtypeStruct((B, nq, H * MLA_V, tq), BF16),
        scratch_shapes=[pltpu.VMEM((TSTEP, tq), F32)] * 2,
        compiler_params=_params(("parallel", "parallel")),
        name="mla_attention",
    )(qT, k, vT, sz)


def _out_proj_kernel(og_ref, wT_ref, x_ref, o_ref):
    ntile = og_ref.shape[1]
    og = og_ref[0, 0] if ntile == 1 else jnp.concatenate([og_ref[0, r] for r in range(ntile)], axis=1)
    yT = _dot(wT_ref[...], og)
    o_ref[0] = x_ref[0] + yT.T


def _out_proj(og, wT, x):
    B, S, D = x.shape
    _, _, K, W = og.shape
    r = TO // W
    return pl.pallas_call(
        _out_proj_kernel,
        grid=(B, S // TO),
        in_specs=[
            pl.BlockSpec((1, r, K, W), lambda b, t: (b, t, 0, 0)),
            pl.BlockSpec((D, K), lambda b, t: (0, 0)),
            pl.BlockSpec((1, TO, D), lambda b, t: (b, t, 0)),
        ],
        out_specs=pl.BlockSpec((1, TO, D), lambda b, t: (b, t, 0)),
        out_shape=jax.ShapeDtypeStruct((B, S, D), F32),
        compiler_params=_params(("parallel", "parallel")),
        name="out_proj",
    )(og, wT, x)


def _nsa_in_kernel(x_ref, ng_ref, wtok_ref, wfT_ref, gq_ref, gk_ref, ct_ref, st_ref,
                   cosT_ref, sinT_ref,
                   qT_ref, ks_ref, kw_ref, kc_ref, vc_ref, vsT_ref, vwT_ref, g_ref, sz_ref):
    t = pl.program_id(1)
    x = x_ref[0]
    n = x.shape[0]
    tq = qT_ref.shape[-1]
    h = _rms_rows(x, ng_ref[...]).astype(BF16)
    G = NSA_GROUPS
    lanes = [slice(u * tq, (u + 1) * tq) for u in range(n // tq)]

    fa = _dot_nt(wfT_ref[...], h)
    cos = cosT_ref[...]
    sin = sinT_ref[...]
    gq = gq_ref[...]
    for hd in range(NSA_HEADS):
        blk = fa[hd * NSA_HD:(hd + 1) * NSA_HD]
        ss = jnp.sum(blk * blk, axis=0, keepdims=True)
        qn = blk * lax.rsqrt(ss * (1.0 / NSA_DK) + RMS_EPS) * gq
        x1 = qn[0:16]
        x2 = qn[64:80]
        r1 = (x1 * cos - x2 * sin).astype(BF16)
        r2 = (x2 * cos + x1 * sin).astype(BF16)
        qb = qn.astype(BF16)
        for u, sl in enumerate(lanes):
            qT_ref[0, hd, u, 0:16, :] = r1[:, sl]
            qT_ref[0, hd, u, 16:64, :] = qb[16:64, sl]
            qT_ref[0, hd, u, 64:80, :] = r2[:, sl]
            qT_ref[0, hd, u, 80:128, :] = qb[80:128, sl]
    off = NSA_HEADS * NSA_HD
    ones = jnp.ones((ONES, n), BF16)
    for ref in (vsT_ref, vwT_ref):
        for g in range(G):
            ref[0, g, 0, 0:NSA_DV, :] = fa[off + g * NSA_DV:off + (g + 1) * NSA_DV].astype(BF16)
            ref[0, g, 0, NSA_DV:NSA_DV + ONES, :] = ones
        off += G * NSA_DV
    sg = _sigmoid(fa[off:off + 16 * G])
    for g in range(G):
        for u, sl in enumerate(lanes):
            g_ref[0, g, u] = sg[g * 16:(g + 1) * 16, sl]
    off += 16 * G
    z = fa[off:off + NSA_HEADS * NSA_DV]
    sz = z * _sigmoid(z)
    for u, sl in enumerate(lanes):
        sz_ref[0, u] = sz[:, sl]

    pt = _dot(h, wtok_ref[...])
    ct = ct_ref[...]
    st = st_ref[...]
    nblk = ks_ref.shape[-1] - NSA_HD
    row = lax.broadcasted_iota(jnp.int32, (n, nblk), 0) + t * n
    col = lax.broadcasted_iota(jnp.int32, (n, nblk), 1)
    onehot = jnp.where((row // SEL_LEN) == col, 1.0, 0.0).astype(BF16)
    for br, ref in ((0, ks_ref), (1, kw_ref)):
        gk = gk_ref[br + 1:br + 2, :]
        for g in range(G):
            kb = pt[:, (br * G + g) * NSA_HD:(br * G + g + 1) * NSA_HD]
            ss = jnp.sum(kb * kb, axis=-1, keepdims=True)
            kn = kb * lax.rsqrt(ss * (1.0 / NSA_DK) + RMS_EPS) * gk
            kr = kn * ct + pltpu.roll(kn, 64, axis=1) * st
            ref[0, g, :, 0:NSA_HD] = kr.astype(BF16)
            if br == 0:
                ref[0, g, :, NSA_HD:NSA_HD + nblk] = onehot
    for g in range(G):
        kc_ref[0, g] = pt[:, (2 * G + g) * LANE:(2 * G + g + 1) * LANE]
        vc_ref[0, g] = pt[:, (3 * G + g) * LANE:(3 * G + g + 1) * LANE]


def _nsa_in(x, ng, w, tabs):
    B, S, D = x.shape
    nt = S // TT
    G = NSA_GROUPS
    nblk = S // SEL_LEN
    r = TT // TQ_NSA
    nq = S // TQ_NSA
    assert TT == TKV and TT % TQ_NSA == 0
    full = lambda shape: pl.BlockSpec(shape, lambda b, t: (0,) * len(shape))
    tok = lambda b, t: (b * nt + t, 0)
    feat = lambda b, t: (0, b * nt + t)
    tokmaj = lambda width: pl.BlockSpec((1, G, TT, width), lambda b, t: (b, 0, t, 0))
    vtile = pl.BlockSpec((1, G, 1, NSA_DV + ONES, TT), lambda b, t: (b, 0, t, 0, 0))
    vshape = jax.ShapeDtypeStruct((B, G, nt, NSA_DV + ONES, TT), BF16)
    return pl.pallas_call(
        _nsa_in_kernel,
        grid=(B, nt),
        in_specs=[
            pl.BlockSpec((1, TT, D), lambda b, t: (b, t, 0)),
            full((1, D)),
            full(w["wtok"].shape), full(w["wfT"].shape),
            full((NSA_HD, TT)), full((3, NSA_HD)),
            pl.BlockSpec((TT, LANE), tok), pl.BlockSpec((TT, LANE), tok),
            pl.BlockSpec((16, TT), feat), pl.BlockSpec((16, TT), feat),
        ],
        out_specs=[
            pl.BlockSpec((1, NSA_HEADS, r, NSA_HD, TQ_NSA), lambda b, t: (b, 0, t, 0, 0)),
            tokmaj(NSA_HD + nblk), tokmaj(NSA_HD), tokmaj(LANE), tokmaj(LANE),
            vtile, vtile,
            pl.BlockSpec((1, G, r, 16, TQ_NSA), lambda b, t: (b, 0, t, 0, 0)),
            pl.BlockSpec((1, r, NSA_HEADS * NSA_DV, TQ_NSA), lambda b, t: (b, t, 0, 0)),
        ],
        out_shape=[
            jax.ShapeDtypeStruct((B, NSA_HEADS, nq, NSA_HD, TQ_NSA), BF16),
            jax.ShapeDtypeStruct((B, G, S, NSA_HD + nblk), BF16),
            jax.ShapeDtypeStruct((B, G, S, NSA_HD), BF16),
            jax.ShapeDtypeStruct((B, G, S, LANE), F32),
            jax.ShapeDtypeStruct((B, G, S, LANE), F32),
            vshape, vshape,
            jax.ShapeDtypeStruct((B, G, nq, 16, TQ_NSA), F32),
            jax.ShapeDtypeStruct((B, nq, NSA_HEADS * NSA_DV, TQ_NSA), F32),
        ],
        compiler_params=_params(("parallel", "parallel")),
        name="nsa_in_proj",
    )(x, ng, w["wtok"], w["wfT"], w["gq"], w["gk"], tabs["ct_nsa"], tabs["st_nsa"],
      tabs["cosT_nsa"], tabs["sinT_nsa"])


def _nsa_cmp_kernel(kc_ref, vc_ref, pek_ref, pev_ref, w1k_ref, w2k_ref, w1v_ref, w2vT_ref,
                    gk_ref, ct_ref, st_ref, kcmp_ref, vcmpT_ref):
    nrow = kc_ref.shape[2]

    def pre(x, pe_ref, w1_ref):
        lo = _dot((x + pe_ref[0:1, :]).astype(BF16), w1_ref[0])
        hi = _dot((x + pe_ref[1:2, :]).astype(BF16), w1_ref[1])
        return lo + pltpu.roll(hi, nrow - 1, axis=0)

    a = pre(kc_ref[0, 0], pek_ref, w1k_ref)
    kc = _dot((a * _sigmoid(a)).astype(BF16), w2k_ref[...])
    ss = jnp.sum(kc * kc, axis=-1, keepdims=True)
    kn = kc * lax.rsqrt(ss * (1.0 / NSA_DK) + RMS_EPS) * gk_ref[0:1, :]
    kr = kn * ct_ref[0] + pltpu.roll(kn, 64, axis=1) * st_ref[0]
    kcmp_ref[0, 0] = kr.astype(BF16)

    a = pre(vc_ref[0, 0], pev_ref, w1v_ref)
    sv = (a * _sigmoid(a)).astype(BF16)
    vcmpT_ref[0, 0, 0:NSA_DV, :] = _dot_nt(w2vT_ref[...], sv).astype(BF16)
    vcmpT_ref[0, 0, NSA_DV:NSA_DV + ONES, :] = jnp.ones((ONES, nrow), BF16)


def _nsa_cmp(kc16, vc16, w, tabs):
    B, G, nrow, width = kc16.shape
    full = lambda shape: pl.BlockSpec(shape, lambda b, g: (0,) * len(shape))
    blk = pl.BlockSpec((1, 1, nrow, width), lambda b, g: (b, g, 0, 0))
    tab = pl.BlockSpec((1, nrow, LANE), lambda b, g: (b, 0, 0))
    return pl.pallas_call(
        _nsa_cmp_kernel,
        grid=(B, G),
        in_specs=[blk, blk, full((2, width)), full((2, width)),
                  full(w["w1k"].shape), full(w["w2k"].shape),
                  full(w["w1v"].shape), full(w["w2vT"].shape),
                  full((3, NSA_HD)), tab, tab],
        out_specs=[pl.BlockSpec((1, 1, nrow, NSA_HD), lambda b, g: (b, g, 0, 0)),
                   pl.BlockSpec((1, 1, NSA_DV + ONES, nrow), lambda b, g: (b, g, 0, 0))],
        out_shape=[jax.ShapeDtypeStruct((B, G, nrow, NSA_HD), BF16),
                   jax.ShapeDtypeStruct((B, G, NSA_DV + ONES, nrow), BF16)],
        compiler_params=_params(("parallel", "parallel")),
        name="nsa_compress",
    )(kc16, vc16, w["pek"], w["pev"], w["w1k"], w["w2k"], w["w1v"], w["w2vT"], w["gk"],
      tabs["ct_cmp"], tabs["st_cmp"])


def _tile4(a):
    return jnp.concatenate([a] * NSA_HPG, axis=1)


def _cmp_exp(s_ref, bias):
    n, w = s_ref.shape
    fold = lambda a, op: op(a.reshape(a.shape[0] // SUB, SUB, w), axis=0)
    rows = [slice(lo, lo + CHUNK) for lo in range(0, n, CHUNK)]
    mx = None
    for sl in rows:
        blk = bias[sl.start](s_ref[sl, :])
        s_ref[sl, :] = blk
        mx = fold(blk, jnp.max) if mx is None else jnp.maximum(mx, fold(blk, jnp.max))
    mc = jnp.maximum(jnp.max(mx, axis=0, keepdims=True), 0.5 * NEG)
    return jnp.concatenate([jnp.exp2(s_ref[sl, :] - mc).astype(BF16) for sl in rows], axis=0)


def _nsa_pre_kernel(qT_ref, kw_ref, vwT_ref, kc_ref, vcT_ref, ovT_ref, g_ref,
                    csel_ref, ocw_ref, *scratch, k_sel):
    nq, _, tq = qT_ref.shape[2:]
    ncmp = kc_ref.shape[2]
    nblk = ovT_ref.shape[0]

    wk = WINDOW + tq
    rloc = lax.broadcasted_iota(jnp.int32, (CHUNK, tq), 0)
    cloc = lax.broadcasted_iota(jnp.int32, (CHUNK, tq), 1)
    mask = lambda ok, lo: (lambda blk: blk + _tile4(jnp.where(ok(lo), 0.0, NEG)))

    def one_tile(t, u, clamped, jlim):
        q0 = t * tq
        qT = jnp.concatenate([qT_ref[0, hh, t] for hh in range(NSA_HPG)], axis=1)
        tok = q0 + lax.broadcasted_iota(jnp.int32, (1, tq), 1)

        sc_scr, sw_scr = scratch[2 * u], scratch[2 * u + 1]
        sc_scr[...] = _dot(kc_ref[0, 0], qT)
        yield
        cvalid = lambda lo: ((lo + rloc) * CMP_STRIDE + (CMP_LEN - 1)) <= tok
        cbias = {lo: mask(cvalid, lo) for lo in range(0, ncmp, CHUNK)}
        ec = _cmp_exp(sc_scr, cbias)
        ac = _dot(vcT_ref[0, 0], ec)
        dc = ac[NSA_DV:NSA_DV + 1]
        inv = 1.0 / jnp.where(dc > 0, dc, 1.0)
        o_c = ac[0:NSA_DV] * inv

        if clamped:
            start, off = 0, 0
            wbias = {lo: mask(lambda lo: lo + rloc <= tok, lo) for lo in range(0, wk, CHUNK)}
        else:
            start, off = q0 - WINDOW, (u % (TKV // tq)) * tq
            wbias = {lo: mask(lambda lo: lo + rloc > cloc, lo) for lo in range(0, tq, CHUNK)}
            wbias.update({lo: mask(lambda lo: lo - WINDOW + rloc <= cloc, lo)
                          for lo in range(WINDOW, wk, CHUNK)})
        sw_scr[...] = _dot(kw_ref[0, 0, pl.ds(pl.multiple_of(start, tq), wk), :], qT)
        yield
        w0 = start // TKV
        segs, pos, tile = [], 0, 0
        while pos < wk:
            width = min(TKV - off, wk - pos)
            segs.append((pos, pos + width, functools.partial(
                lambda tile, off, width: vwT_ref[0, 0, w0 + tile, :, off:off + width], tile, off, width)))
            pos, tile, off = pos + width, tile + 1, 0
        init = (jnp.full((1, NSA_HPG * tq), M_INIT, F32),
                jnp.zeros((NSA_DV + ONES, NSA_HPG * tq), F32))
        _, aw = _softmax_step_ref(sw_scr, init, segs=segs, bias=wbias)
        o_w = aw[0:NSA_DV] * (1.0 / aw[NSA_DV:NSA_DV + 1])
        yield

        if jlim:
            iw = _dot(ovT_ref[0:jlim, :], ec) * inv
            imp = iw[:, 0:tq]
            for hh in range(1, NSA_HPG):
                imp = imp + iw[:, hh * tq:(hh + 1) * tq]
            bidx = lax.broadcasted_iota(jnp.int32, (jlim, tq), 0)
            cur = tok // SEL_LEN
            forced = (bidx == 0) | (bidx == cur) | (bidx == cur - 1)
            valid = (bidx * SEL_LEN) <= tok
            score = jnp.where(forced, SEL_FORCE, jnp.where(valid, imp, -1.0))
            grp = [score[r * SUB:(r + 1) * SUB] for r in range(jlim // SUB)]
            bsub = lax.broadcasted_iota(jnp.int32, (SUB, tq), 0)
            cnt = [jnp.zeros((SUB, tq), F32) for _ in grp]
            for j in range(jlim):
                rowj = score[j:j + 1, :]
                for r in range(jlim // SUB):
                    if r * SUB > j:
                        cnt[r] = cnt[r] + jnp.where(rowj >= grp[r], 1.0, 0.0)
                    elif r * SUB + SUB - 1 < j:
                        cnt[r] = cnt[r] + jnp.where(rowj > grp[r], 1.0, 0.0)
                    else:
                        tie = jnp.where(bsub + r * SUB > j, 1.0, 0.0)
                        cnt[r] = cnt[r] + jnp.where(rowj > grp[r], 1.0, 0.0)
                        cnt[r] = cnt[r] + jnp.where(rowj == grp[r], tie, 0.0)
            cnt = jnp.concatenate(cnt, axis=0)
            csel_ref[0, 0, t, 0:jlim, :] = jnp.where(cnt < k_sel, 0.0, NEG).astype(BF16)
        if jlim < nblk:
            csel_ref[0, 0, t, jlim:nblk, :] = jnp.zeros((nblk - jlim, tq), BF16)

        for hh in range(NSA_HPG):
            sl = slice(hh * tq, (hh + 1) * tq)
            g0 = g_ref[0, 0, t, hh:hh + 1, :]
            g2 = g_ref[0, 0, t, 2 * NSA_HPG + hh:2 * NSA_HPG + hh + 1, :]
            ocw_ref[0, 0, t, hh * NSA_DV:(hh + 1) * NSA_DV, :] = g0 * o_c[:, sl] + g2 * o_w[:, sl]

    def make_trip(clamped, jlim):
        def trip(i, _):
            live = [one_tile(i * PRE_TILES + u, u, clamped, jlim) for u in range(PRE_TILES)]
            while live:
                live = [g for g in live if next(g, "done") != "done"]
            return 0
        return trip

    step = tq * PRE_TILES
    n_trips = nq // PRE_TILES
    n_clamped = min(WINDOW // step, n_trips)
    n_norank = min(max((k_sel * SEL_LEN) // step, n_clamped), n_trips)
    cuts = sorted({0, n_clamped, n_norank, n_trips}
                  | {n_norank + (k * (n_trips - n_norank)) // RANK_CLASSES for k in range(RANK_CLASSES)})
    for lo, hi in zip(cuts[:-1], cuts[1:]):
        causal_blocks = -(-(hi * step) // SEL_LEN)
        jlim = 0 if hi <= n_norank else min(nblk, -(-causal_blocks // SUB) * SUB)
        lax.fori_loop(lo, hi, make_trip(lo < n_clamped, jlim), 0)


def _nsa_pre(qT, kw, vwT, kcmp, vcmpT, ovT, gates):
    B, H, nq, _, tq = qT.shape
    G = NSA_GROUPS
    S = kw.shape[2]
    nblk = S // SEL_LEN
    ncmp = kcmp.shape[2]
    assert TKV % tq == 0 and nq % PRE_TILES == 0 and PRE_TILES % (TKV // tq) == 0
    assert WINDOW % (tq * PRE_TILES) == 0 and WINDOW % TKV == 0 and S >= WINDOW + tq
    kern = functools.partial(_nsa_pre_kernel, k_sel=min(N_SELECT, nblk))
    per_bg = lambda shape: pl.BlockSpec((1, 1) + shape, lambda b, g: (b, g) + (0,) * len(shape))
    wide = NSA_HPG * tq
    return pl.pallas_call(
        kern,
        grid=(B, G),
        in_specs=[
            pl.BlockSpec((1, NSA_HPG, nq, NSA_HD, tq), lambda b, g: (b, g, 0, 0, 0)),
            per_bg((S, NSA_HD)),
            per_bg((S // TKV, NSA_DV + ONES, TKV)),
            per_bg((ncmp, NSA_HD)),
            per_bg((NSA_DV + ONES, ncmp)),
            pl.BlockSpec((nblk, ncmp), lambda b, g: (0, 0)),
            per_bg((nq, 16, tq)),
        ],
        out_specs=[per_bg((nq, nblk, tq)), per_bg((nq, NSA_HPG * NSA_DV, tq))],
        out_shape=[jax.ShapeDtypeStruct((B, G, nq, nblk, tq), BF16),
                   jax.ShapeDtypeStruct((B, G, nq, NSA_HPG * NSA_DV, tq), F32)],
        scratch_shapes=[pltpu.VMEM((ncmp, wide), F32), pltpu.VMEM((WINDOW + tq, wide), F32)] * PRE_TILES,
        compiler_params=_params(("parallel", "parallel")),
        name="nsa_branches",
    )(qT, kw, vwT, kcmp, vcmpT, ovT, gates)


def _nsa_sel_kernel(qT_ref, csel_ref, ks_ref, vsT_ref, ocw_ref, g_ref, sz_ref, o_ref,
                    qaug_ref, s_a, s_b):
    nq, _, tq = qT_ref.shape[2:]
    nblk = csel_ref.shape[3]
    nsub = TSTEP // TKV
    per = TQ_SEL // tq
    wide = NSA_HPG * TQ_SEL

    row = lax.broadcasted_iota(jnp.int32, (CHUNK, TQ_SEL), 0)
    col = lax.broadcasted_iota(jnp.int32, (CHUNK, TQ_SEL), 1)

    def qtile(sup, p):
        i = sup * (TSTEP // TQ_SEL) + p
        q0 = i * TQ_SEL
        for hh in range(NSA_HPG):
            for u in range(per):
                lanes = slice(hh * TQ_SEL + u * tq, hh * TQ_SEL + (u + 1) * tq)
                qaug_ref[0:NSA_HD, lanes] = qT_ref[0, hh, i * per + u]
                qaug_ref[NSA_HD:NSA_HD + nblk, lanes] = csel_ref[0, 0, i * per + u]
        qaug = qaug_ref[...]

        def scores(j):
            kt = ks_ref[0, 0, pl.ds(pl.multiple_of(j * TSTEP, TSTEP), TSTEP), :]
            return _dot(kt, qaug)

        def vtile(j):
            return lambda r: vsT_ref[0, 0, j * nsub + r]

        own = p * TQ_SEL
        causal = {own + lo: functools.partial(
            lambda lo, blk: blk + _tile4(jnp.where(lo + row <= col, 0.0, NEG)), lo)
            for lo in range(0, TQ_SEL, CHUNK)}
        init = (jnp.full((1, wide), M_INIT, F32), jnp.zeros((NSA_DV + ONES, wide), F32))
        _, acc = _sweep(scores, vtile, causal, own + TQ_SEL, sup, init, s_a, s_b)
        o_s = acc[0:NSA_DV] * (1.0 / acc[NSA_DV:NSA_DV + 1])

        for hh in range(NSA_HPG):
            rows = slice(hh * NSA_DV, (hh + 1) * NSA_DV)
            for u in range(per):
                t = i * per + u
                lanes = slice(hh * TQ_SEL + u * tq, hh * TQ_SEL + (u + 1) * tq)
                g1 = g_ref[0, 0, t, NSA_HPG + hh:NSA_HPG + hh + 1, :]
                o = ocw_ref[0, 0, t, rows, :] + g1 * o_s[:, lanes]
                o_ref[0, t, rows, :] = (o * sz_ref[0, t, rows, :]).astype(BF16)

    def super_tile(sup, _):
        for p in range(TSTEP // TQ_SEL):
            qtile(sup, p)
        return 0

    lax.fori_loop(0, (nq * tq) // TSTEP, super_tile, 0)


def _nsa_sel(qT, csel, ks, vsT, ocw, gates, sz):
    B, H, nq, _, tq = qT.shape
    G = NSA_GROUPS
    S = ks.shape[2]
    nblk = S // SEL_LEN
    assert TQ_SEL % tq == 0 and TSTEP % TQ_SEL == 0 and TSTEP % TKV == 0 and S % TSTEP == 0
    per_bg = lambda shape: pl.BlockSpec((1, 1) + shape, lambda b, g: (b, g) + (0,) * len(shape))
    gate = pl.BlockSpec((1, nq, NSA_HPG * NSA_DV, tq), lambda b, g: (b, 0, g, 0))
    return pl.pallas_call(
        _nsa_sel_kernel,
        grid=(B, G),
        in_specs=[
            pl.BlockSpec((1, NSA_HPG, nq, NSA_HD, tq), lambda b, g: (b, g, 0, 0, 0)),
            per_bg((nq, nblk, tq)),
            per_bg((S, NSA_HD + nblk)),
            per_bg((S // TKV, NSA_DV + ONES, TKV)),
            per_bg((nq, NSA_HPG * NSA_DV, tq)),
            per_bg((nq, 16, tq)),
            gate,
        ],
        out_specs=gate,
        out_shape=jax.ShapeDtypeStruct((B, nq, H * NSA_DV, tq), BF16),
        scratch_shapes=[pltpu.VMEM((NSA_HD + nblk, NSA_HPG * TQ_SEL), BF16)]
        + [pltpu.VMEM((TSTEP, NSA_HPG * TQ_SEL), F32)] * 2,
        compiler_params=_params(("parallel", "parallel")),
        name="nsa_selected",
    )(qT, csel, ks, vsT, ocw, gates, sz)


def _nsa_perm():
    src = np.full((NSA_HD,), -1, np.int64)
    src[0:12] = np.arange(0, 12)
    src[12:16] = np.arange(24, 28)
    src[16:64] = np.arange(28, 76)
    src[64:76] = np.arange(12, 24)
    src[76:80] = np.arange(76, 80)
    src[80:96] = np.arange(80, 96)
    return src


def _take_cols(w, src):
    idx = np.where(src >= 0, src, 0)
    out = jnp.take(w, jnp.asarray(idx), axis=-1)
    return jnp.where(jnp.asarray(src >= 0), out, 0.0)


def _pad_last(w, width):
    return jnp.pad(w, [(0, 0)] * (w.ndim - 1) + [(0, width - w.shape[-1])])


def _prep_mla(w_in, g_cq, w_uq, g_ckv, w_ukv, g_q, g_k, w_out):
    D = w_in.shape[0]
    o1 = MLA_Q_LORA + MLA_KV_LORA
    z32 = jnp.zeros((D, 32), F32)
    wa = jnp.concatenate([w_in[:, :o1], w_in[:, o1:o1 + 32], z32, w_in[:, o1 + 32:o1 + 64], z32], axis=1)
    wq = w_uq.reshape(MLA_Q_LORA, MLA_HEADS, MLA_QK)
    zq = jnp.zeros((MLA_Q_LORA, MLA_HEADS, 32), F32)
    wq = jnp.concatenate([wq[:, :, 64:], wq[:, :, :32], zq, wq[:, :, 32:64], zq], axis=2)
    z1 = jnp.zeros((32,), F32)
    gq = jnp.concatenate([g_q[64:], g_q[:32], z1, g_q[32:64], z1]) * C_MLA
    wkv = w_ukv.reshape(MLA_KV_LORA, MLA_HEADS, MLA_NOPE + MLA_V)
    return {
        "wa": wa.astype(BF16),
        "wzT": w_in[:, o1 + MLA_ROPE:].T.astype(BF16),
        "gcq": g_cq.reshape(1, -1), "gckv": g_ckv.reshape(1, -1),
        "wuqT": wq.reshape(MLA_Q_LORA, MLA_HEADS * MLA_HD).T.astype(BF16),
        "gq": jnp.broadcast_to(gq[:, None], (MLA_HD, TT)),
        "wuk": wkv[:, :, :MLA_NOPE].reshape(MLA_KV_LORA, -1).astype(BF16),
        "wuvT": wkv[:, :, MLA_NOPE:].reshape(MLA_KV_LORA, -1).T.astype(BF16),
        "gkn": g_k[64:].reshape(1, -1),
        "gkp": jnp.concatenate([g_k[:32], z1, g_k[32:64], z1]).reshape(1, -1),
        "woT": w_out.T.astype(BF16),
    }


def _prep_nsa(w_in, g_q, g_k, pe_k, w1_k, w2_k, pe_v, w1_v, w2_v, w_out):
    D = w_in.shape[0]
    G = NSA_GROUPS
    src = _nsa_perm()
    offs = np.concatenate([[0], np.cumsum(NSA_SIZES)])
    part = lambda i: w_in[:, offs[i]:offs[i + 1]]
    q, kc, vc, ks, vs, kw, vw, gl, z = [part(i) for i in range(9)]
    perm_heads = lambda w, nh: _take_cols(w.reshape(D, nh, NSA_DK), src).reshape(D, nh * NSA_HD)
    pad_groups = lambda w, d: _pad_last(w.reshape(D, G, d), LANE).reshape(D, G * LANE)
    wtok = jnp.concatenate([perm_heads(ks, G), perm_heads(kw, G), pad_groups(kc, NSA_DK),
                            pad_groups(vc, NSA_DV)], axis=1)
    glr = gl.reshape(D, G, NSA_HPG, 3).transpose(0, 1, 3, 2).reshape(D, G, 3 * NSA_HPG)
    glr = _pad_last(glr, 16).reshape(D, G * 16)
    wf = jnp.concatenate([perm_heads(q, NSA_HEADS), vs, vw, glr, z], axis=1)
    halves = lambda w, d: _pad_last(w.reshape(2, CMP_LEN // 2, d, -1).transpose(0, 1, 3, 2), LANE) \
        .transpose(0, 1, 3, 2).reshape(2, (CMP_LEN // 2) * LANE, -1)
    pe_flat = lambda pe: _pad_last(pe, LANE).reshape(2, (CMP_LEN // 2) * LANE)
    return {
        "wtok": wtok.astype(BF16),
        "wfT": wf.T.astype(BF16),
        "gq": jnp.broadcast_to((_take_cols(g_q, src) * C_NSA)[:, None], (NSA_HD, TT)),
        "gk": _take_cols(g_k, src),
        "pek": pe_flat(pe_k), "pev": pe_flat(pe_v),
        "w1k": _pad_last(halves(w1_k, NSA_DK), LANE).astype(BF16),
        "w2k": _pad_last(_take_cols(w2_k, src).T, LANE).T.astype(BF16),
        "w1v": _pad_last(halves(w1_v, NSA_DV), LANE).astype(BF16),
        "w2vT": _pad_last(w2_v.T, LANE).astype(BF16),
        "woT": w_out.T.astype(BF16),
    }


def _prep_tables(cos_t, sin_t, B, S):
    cm, sm = cos_t[:32].T, sin_t[:32].T
    cn, sn = cos_t[32:48].T, sin_t[32:48].T
    n = cm.shape[0]
    one = lambda w: jnp.ones((n, w), F32)
    zero = lambda w: jnp.zeros((n, w), F32)
    ct_nsa = jnp.concatenate([cn, one(48), cn, one(48)], axis=1)
    st_nsa = jnp.concatenate([-sn, zero(48), sn, zero(48)], axis=1)
    ncmp = S // CMP_STRIDE
    last = CMP_LEN - 1
    pick = lambda t: jnp.pad(t.reshape(B, S, LANE)[:, last::CMP_STRIDE],
                             ((0, 0), (0, ncmp - (S - last + CMP_STRIDE - 1) // CMP_STRIDE), (0, 0)))
    return {
        "cosT_mla": cos_t[:32], "sinT_mla": sin_t[:32],
        "cosT_nsa": cos_t[32:48], "sinT_nsa": sin_t[32:48],
        "ct_mla": jnp.concatenate([cm, one(32), cm, one(32)], axis=1),
        "st_mla": jnp.concatenate([-sm, zero(32), sm, zero(32)], axis=1),
        "ct_nsa": ct_nsa, "st_nsa": st_nsa,
        "ct_cmp": pick(ct_nsa), "st_cmp": pick(st_nsa),
    }


def _overlap_T(S):
    ncmp = S // CMP_STRIDE
    nblk = S // SEL_LEN
    n_cmp = (S - CMP_LEN) // CMP_STRIDE + 1
    cs = np.arange(ncmp) * CMP_STRIDE
    ss = np.arange(nblk) * SEL_LEN
    ov = (cs[None, :] < ss[:, None] + SEL_LEN) & (cs[None, :] + CMP_LEN > ss[:, None])
    ov = ov & (np.arange(ncmp)[None, :] < n_cmp)
    return jnp.asarray(ov.astype(np.float32)).astype(BF16)


def _mla_layer(x, ng, w, tabs):
    qT, k, vT, sz = _mla_in(x, ng.reshape(1, -1), w, tabs)
    og = _mla_attn(qT, k, vT, sz)
    return _out_proj(og, w["woT"], x)


def _nsa_layer(x, ng, w, tabs, ovT):
    B, S, _ = x.shape
    qT, ks, kw, kc, vc, vsT, vwT, gates, sz = _nsa_in(x, ng.reshape(1, -1), w, tabs)
    ncmp = S // CMP_STRIDE
    kc16 = kc.reshape(B, NSA_GROUPS, ncmp, CMP_STRIDE * LANE)
    vc16 = vc.reshape(B, NSA_GROUPS, ncmp, CMP_STRIDE * LANE)
    kcmp, vcmpT = _nsa_cmp(kc16, vc16, w, tabs)
    csel, ocw = _nsa_pre(qT, kw, vwT, kcmp, vcmpT, ovT, gates)
    og = _nsa_sel(qT, csel, ks, vsT, ocw, gates, sz)
    return _out_proj(og, w["woT"], x)


def kernel(x, positions, norm_g, mla_w_in, mla_g_cq, mla_w_uq, mla_g_ckv, mla_w_ukv, mla_g_q, mla_g_k, mla_w_out, nsa_w_in, nsa_g_q, nsa_g_k, nsa_pe_k, nsa_w1_k, nsa_w2_k, nsa_pe_v, nsa_w1_v, nsa_w2_v, nsa_w_out):
    B, S, _ = x.shape
    cos_t, sin_t = _rope_tables(positions)
    tabs = _prep_tables(cos_t, sin_t, B, S)
    ovT = _overlap_T(S)
    for i in range(DEPTH):
        j = i // N_MIXERS
        if i % N_MIXERS == 0:
            w = _prep_mla(mla_w_in[j], mla_g_cq[j], mla_w_uq[j], mla_g_ckv[j], mla_w_ukv[j],
                          mla_g_q[j], mla_g_k[j], mla_w_out[j])
            x = _mla_layer(x, norm_g[i], w, tabs)
        else:
            w = _prep_nsa(nsa_w_in[j], nsa_g_q[j], nsa_g_k[j], nsa_pe_k[j], nsa_w1_k[j],
                          nsa_w2_k[j], nsa_pe_v[j], nsa_w1_v[j], nsa_w2_v[j], nsa_w_out[j])
            x = _nsa_layer(x, norm_g[i], w, tabs, ovT)
    return x
```

```python
import functools

import numpy as np
import jax
import jax.numpy as jnp
from jax import lax
from jax.experimental import pallas as pl
from jax.experimental.pallas import tpu as pltpu

F32 = jnp.float32
BF16 = jnp.bfloat16

D_MODEL = 1024
DEPTH = 4
N_MIXERS = 2
ROPE_THETA = 500000.0
RMS_EPS = 1e-6
MLA_HEADS = 16
MLA_NOPE = 128
MLA_ROPE = 64
MLA_V = 128
MLA_QK = MLA_NOPE + MLA_ROPE
MLA_Q_LORA = 384
MLA_KV_LORA = 256
MLA_HD = 256
NSA_HEADS = 16
NSA_GROUPS = 4
NSA_HPG = NSA_HEADS // NSA_GROUPS
NSA_DK = 96
NSA_DV = 64
NSA_ROT = NSA_DK // 4
NSA_HALF = NSA_ROT // 2
NSA_HD = 128
CMP_LEN = 32
CMP_STRIDE = 16
SEL_LEN = 64
N_SELECT = 16
WINDOW = 512
SEL_FORCE = 1e4
NSA_SIZES = (NSA_HEADS * NSA_DK,
             NSA_GROUPS * NSA_DK, NSA_GROUPS * NSA_DV,
             NSA_GROUPS * NSA_DK, NSA_GROUPS * NSA_DV,
             NSA_GROUPS * NSA_DK, NSA_GROUPS * NSA_DV,
             3 * NSA_HEADS, NSA_HEADS * NSA_DV)

LANE = 128
SUB = 8
TT = 256
TO = 512
TKV = 256
TSTEP = 512
TQ_MLA = 512
TQ_NSA = 128
TQ_SEL = 256
PRE_TILES = 2
RANK_CLASSES = 3
CHUNK = 64
ONES = 16
NEG = -30000.0
M_INIT = -1e30
LOG2E = 1.4426950408889634
C_MLA = (MLA_QK ** -0.5) * LOG2E
C_NSA = (NSA_DK ** -0.5) * LOG2E
VMEM_LIMIT = 56 * 1024 * 1024


def _dot(a, b):
    return jnp.dot(a, b, preferred_element_type=F32)


def _dot_nt(a, b):
    return lax.dot_general(a, b, (((1,), (1,)), ((), ())), preferred_element_type=F32)


def _sigmoid(x):
    return 1.0 / (1.0 + jnp.exp(-x))


def _rms_rows(x, g):
    ms = jnp.mean(x * x, axis=-1, keepdims=True)
    return x * lax.rsqrt(ms + RMS_EPS) * g


def _params(sem):
    return pltpu.CompilerParams(dimension_semantics=sem, vmem_limit_bytes=VMEM_LIMIT)


def _rope_kernel(pos_ref, invf_ref, cm_ref, sm_ref, cn_ref, sn_ref, ctm_ref, stm_ref, ctn_ref, stn_ref):
    ang = invf_ref[...] * pos_ref[...]
    c, s = jnp.cos(ang), jnp.sin(ang)
    hm = MLA_ROPE // 2
    cm, sm, cn, sn = c[:hm], s[:hm], c[hm:hm + 16], s[hm:hm + 16]
    cm_ref[...], sm_ref[...], cn_ref[...], sn_ref[...] = cm, sm, cn, sn
    tn = ang.shape[1]
    one = lambda r: jnp.ones((r, tn), F32)
    zero = lambda r: jnp.zeros((r, tn), F32)
    ctm_ref[...] = jnp.concatenate([cm, one(32), cm, one(32)], axis=0).T
    stm_ref[...] = jnp.concatenate([-sm, zero(32), sm, zero(32)], axis=0).T
    ctn_ref[...] = jnp.concatenate([cn, one(48), cn, one(48)], axis=0).T
    stn_ref[...] = jnp.concatenate([-sn, zero(48), sn, zero(48)], axis=0).T


def _rope_tables(positions):
    B, S = positions.shape
    n = positions.size
    tn = 512
    pos = positions.reshape(1, n).astype(F32)
    half_m = MLA_ROPE // 2
    inv_m = ROPE_THETA ** (-jnp.arange(half_m, dtype=F32) / half_m)
    inv_n = ROPE_THETA ** (-jnp.arange(NSA_HALF, dtype=F32) / NSA_HALF)
    invf = jnp.concatenate([inv_m, inv_n, jnp.zeros((16 - NSA_HALF,), F32)])
    rows = invf.shape[0]
    invf = jnp.broadcast_to(invf[:, None], (rows, tn))
    feat = lambda r: pl.BlockSpec((r, tn), lambda i: (0, i))
    tok = pl.BlockSpec((tn, LANE), lambda i: (i, 0))
    cm, sm, cn, sn, ctm, stm, ctn, stn = pl.pallas_call(
        _rope_kernel,
        grid=(n // tn,),
        in_specs=[pl.BlockSpec((1, tn), lambda i: (0, i)),
                  pl.BlockSpec((rows, tn), lambda i: (0, 0))],
        out_specs=[feat(half_m), feat(half_m), feat(16), feat(16), tok, tok, tok, tok],
        out_shape=[jax.ShapeDtypeStruct((half_m, n), F32)] * 2
        + [jax.ShapeDtypeStruct((16, n), F32)] * 2 + [jax.ShapeDtypeStruct((n, LANE), F32)] * 4,
        compiler_params=_params(("parallel",)),
        name="rope_tables",
    )(pos, invf)
    ncmp = S // CMP_STRIDE
    last = CMP_LEN - 1
    pick = lambda t: jnp.pad(t.reshape(B, S, LANE)[:, last::CMP_STRIDE],
                             ((0, 0), (0, ncmp - (S - last + CMP_STRIDE - 1) // CMP_STRIDE), (0, 0)))
    return {"cosT_mla": cm, "sinT_mla": sm, "cosT_nsa": cn, "sinT_nsa": sn,
            "ct_mla": ctm, "st_mla": stm, "ct_nsa": ctn, "st_nsa": stn,
            "ct_cmp": pick(ctn), "st_cmp": pick(stn)}


def _mla_in_kernel(x_ref, ng_ref, wa_ref, wzT_ref, gcq_ref, gckv_ref, wuqT_ref, gq_ref,
                   wuk_ref, wuvT_ref, gkn_ref, gkp_ref, ct_ref, st_ref, cosT_ref, sinT_ref,
                   qT_ref, k_ref, vT_ref, sz_ref):
    x = x_ref[0]
    n = x.shape[0]
    h = _rms_rows(x, ng_ref[...]).astype(BF16)
    pa = _dot(h, wa_ref[...])
    z = _dot_nt(wzT_ref[...], h)
    sz_ref[0, 0] = z * _sigmoid(z)

    cqn = _rms_rows(pa[:, :MLA_Q_LORA], gcq_ref[...]).astype(BF16)
    ckvn = _rms_rows(pa[:, MLA_Q_LORA:MLA_Q_LORA + MLA_KV_LORA], gckv_ref[...]).astype(BF16)
    kpe = pa[:, MLA_Q_LORA + MLA_KV_LORA:]

    qa = _dot_nt(wuqT_ref[...], cqn)
    cos = cosT_ref[...]
    sin = sinT_ref[...]
    gq = gq_ref[...]
    zeros32 = jnp.zeros((32, n), BF16)
    for hd in range(MLA_HEADS):
        blk = qa[hd * MLA_HD:(hd + 1) * MLA_HD]
        ss = jnp.sum(blk * blk, axis=0, keepdims=True)
        qn = blk * lax.rsqrt(ss * (1.0 / MLA_QK) + RMS_EPS) * gq
        x1 = qn[128:160]
        x2 = qn[192:224]
        qT_ref[0, hd, 0, 0:128, :] = qn[0:128].astype(BF16)
        qT_ref[0, hd, 0, 128:160, :] = (x1 * cos - x2 * sin).astype(BF16)
        qT_ref[0, hd, 0, 160:192, :] = zeros32
        qT_ref[0, hd, 0, 192:224, :] = (x2 * cos + x1 * sin).astype(BF16)
        qT_ref[0, hd, 0, 224:256, :] = zeros32

    kn = _dot(ckvn, wuk_ref[...])
    ss_pe = jnp.sum(kpe * kpe, axis=-1, keepdims=True)
    kpg = kpe * gkp_ref[...]
    prot = kpg * ct_ref[...] + pltpu.roll(kpg, 64, axis=1) * st_ref[...]
    gkn = gkn_ref[...]
    for hd in range(MLA_HEADS):
        kb = kn[:, hd * MLA_NOPE:(hd + 1) * MLA_NOPE]
        ss = jnp.sum(kb * kb, axis=-1, keepdims=True) + ss_pe
        r = lax.rsqrt(ss * (1.0 / MLA_QK) + RMS_EPS)
        k_ref[0, hd, :, 0:128] = (kb * r * gkn).astype(BF16)
        k_ref[0, hd, :, 128:256] = (prot * r).astype(BF16)

    va = _dot_nt(wuvT_ref[...], ckvn)
    ones = jnp.ones((ONES, n), BF16)
    for hd in range(MLA_HEADS):
        vT_ref[0, hd, 0, 0:MLA_V, :] = va[hd * MLA_V:(hd + 1) * MLA_V].astype(BF16)
        vT_ref[0, hd, 0, MLA_V:MLA_V + ONES, :] = ones


def _mla_in(x, ng, w, tabs):
    B, S, D = x.shape
    nt = S // TT
    r = TQ_MLA // TT
    assert TT == TKV and TQ_MLA % TT == 0
    full = lambda shape: pl.BlockSpec(shape, lambda b, t: (0,) * len(shape))
    tok = lambda b, t: (b * nt + t, 0)
    feat = lambda b, t: (0, b * nt + t)
    return pl.pallas_call(
        _mla_in_kernel,
        grid=(B, nt),
        in_specs=[
            pl.BlockSpec((1, TT, D), lambda b, t: (b, t, 0)),
            full((1, D)),
            full(w["wa"].shape), full(w["wzT"].shape),
            full((1, MLA_Q_LORA)), full((1, MLA_KV_LORA)),
            full(w["wuqT"].shape), full((MLA_HD, TT)),
            full(w["wuk"].shape), full(w["wuvT"].shape),
            full((1, MLA_NOPE)), full((1, LANE)),
            pl.BlockSpec((TT, LANE), tok), pl.BlockSpec((TT, LANE), tok),
            pl.BlockSpec((32, TT), feat), pl.BlockSpec((32, TT), feat),
        ],
        out_specs=[
            pl.BlockSpec((1, MLA_HEADS, 1, MLA_HD, TT), lambda b, t: (b, 0, t // r, 0, t % r)),
            pl.BlockSpec((1, MLA_HEADS, TT, MLA_HD), lambda b, t: (b, 0, t, 0)),
            pl.BlockSpec((1, MLA_HEADS, 1, MLA_V + ONES, TT), lambda b, t: (b, 0, t, 0, 0)),
            pl.BlockSpec((1, 1, MLA_HEADS * MLA_V, TT), lambda b, t: (b, t // r, 0, t % r)),
        ],
        out_shape=[
            jax.ShapeDtypeStruct((B, MLA_HEADS, S // TQ_MLA, MLA_HD, TQ_MLA), BF16),
            jax.ShapeDtypeStruct((B, MLA_HEADS, S, MLA_HD), BF16),
            jax.ShapeDtypeStruct((B, MLA_HEADS, nt, MLA_V + ONES, TT), BF16),
            jax.ShapeDtypeStruct((B, S // TQ_MLA, MLA_HEADS * MLA_V, TQ_MLA), F32),
        ],
        compiler_params=_params(("parallel", "parallel")),
        name="mla_in_proj",
    )(x, ng, w["wa"], w["wzT"], w["gcq"], w["gckv"], w["wuqT"], w["gq"], w["wuk"],
      w["wuvT"], w["gkn"], w["gkp"], tabs["ct_mla"], tabs["st_mla"], tabs["cosT_mla"],
      tabs["sinT_mla"])


def _softmax_step_ref(s_ref, carry, vtile=None, segs=None, bias=None):
    m, acc = carry
    n, w = s_ref.shape
    segs = segs or [(t * TKV, (t + 1) * TKV, functools.partial(vtile, t)) for t in range(n // TKV)]
    bias = bias or {}
    fold = lambda a, op: op(a.reshape(a.shape[0] // SUB, SUB, w), axis=0)
    mx = None
    for lo in range(0, n, CHUNK):
        blk = s_ref[lo:lo + CHUNK, :]
        if lo in bias:
            blk = bias[lo](blk) if callable(bias[lo]) else blk + bias[lo]
            s_ref[lo:lo + CHUNK, :] = blk
        mx = fold(blk, jnp.max) if mx is None else jnp.maximum(mx, fold(blk, jnp.max))
    m_new = jnp.maximum(m, jnp.max(mx, axis=0, keepdims=True))
    alpha = jnp.exp2(m - m_new)
    pv = None
    for seg_lo, seg_hi, value_tile in segs:
        parts = [jnp.exp2(s_ref[lo:lo + CHUNK, :] - m_new).astype(BF16)
                 for lo in range(seg_lo, seg_hi, CHUNK)]
        d = _dot(value_tile(), jnp.concatenate(parts, axis=0))
        pv = d if pv is None else pv + d
    return m_new, alpha * acc + pv


def _sweep(scores, vtile, last_bias, last_rows, n_full, carry, s_a, s_b):
    def step(src, dst, j, carry):
        dst[...] = scores(j + 1)
        return _softmax_step_ref(src, carry, vtile(j))

    def last(src, j, carry):
        return _softmax_step_ref(src.at[0:last_rows], carry, vtile(j), bias=last_bias)

    def pair(t, carry):
        carry = step(s_a, s_b, 2 * t, carry)
        return step(s_b, s_a, 2 * t + 1, carry)

    s_a[...] = scores(0)
    if isinstance(n_full, int):
        bufs = (s_a, s_b)
        for j in range(n_full):
            carry = step(bufs[j % 2], bufs[(j + 1) % 2], j, carry)
        return last(bufs[n_full % 2], n_full, carry)
    carry = lax.fori_loop(0, n_full // 2, pair, carry)
    odd = lambda carry: last(s_b, n_full, step(s_a, s_b, n_full - 1, carry))
    even = lambda carry: last(s_a, n_full, carry)
    return lax.cond(n_full % 2 == 1, odd, even, carry)


def _mla_attn_kernel(qT_ref, k_ref, vT_ref, sz_ref, o_ref, s_a, s_b):
    nq, _, tq = qT_ref.shape[2:]
    nsub = TSTEP // TKV

    row = lax.broadcasted_iota(jnp.int32, (CHUNK, tq), 0)
    col = lax.broadcasted_iota(jnp.int32, (CHUNK, tq), 1)
    causal = {lo: functools.partial(lambda lo, blk: jnp.where(lo + row <= col, blk, NEG), lo)
              for lo in range(0, TSTEP, CHUNK)}

    def qtile(i, _):
        q = qT_ref[0, 0, i]

        def scores(j):
            kt = k_ref[0, 0, j * TSTEP:(j + 1) * TSTEP, :]
            return _dot(kt, q)

        def vtile(j):
            return lambda r: vT_ref[0, 0, j * nsub + r]

        init = (jnp.full((1, tq), M_INIT, F32), jnp.zeros((MLA_V + ONES, tq), F32))
        _, acc = _sweep(scores, vtile, causal, TSTEP, i, init, s_a, s_b)
        o = acc[0:MLA_V] / acc[MLA_V:MLA_V + 1]
        o_ref[0, i] = (o * sz_ref[0, i]).astype(BF16)
        return 0

    for i in range(nq):
        qtile(i, 0)


def _mla_attn(qT, k, vT, sz):
    B, H, nq, _, tq = qT.shape
    S = k.shape[2]
    assert tq == TSTEP and TSTEP % TKV == 0
    per_bh = lambda shape: pl.BlockSpec((1, 1) + shape, lambda b, h: (b, h) + (0,) * len(shape))
    gate = pl.BlockSpec((1, nq, MLA_V, tq), lambda b, h: (b, 0, h, 0))
    return pl.pallas_call(
        _mla_attn_kernel,
        grid=(B, H),
        in_specs=[per_bh((nq, MLA_HD, tq)), per_bh((S, MLA_HD)),
                  per_bh((S // TKV, MLA_V + ONES, TKV)), gate],
        out_specs=gate,
        out_shape=jax.ShapeDtypeStruct((B, nq, H * MLA_V, tq), BF16),
        scratch_shapes=[pltpu.VMEM((TSTEP, tq), F32)] * 2,
        compiler_params=_params(("parallel", "parallel")),
        name="mla_attention",
    )(qT, k, vT, sz)


def _out_proj_kernel(og_ref, wT_ref, x_ref, o_ref):
    ntile = og_ref.shape[1]
    og = og_ref[0, 0] if ntile == 1 else jnp.concatenate([og_ref[0, r] for r in range(ntile)], axis=1)
    yT = _dot(wT_ref[...], og)
    o_ref[0] = x_ref[0] + yT.T


def _out_proj(og, wT, x):
    B, S, D = x.shape
    _, _, K, W = og.shape
    r = TO // W
    return pl.pallas_call(
        _out_proj_kernel,
        grid=(B, S // TO),
        in_specs=[
            pl.BlockSpec((1, r, K, W), lambda b, t: (b, t, 0, 0)),
            pl.BlockSpec((D, K), lambda b, t: (0, 0)),
            pl.BlockSpec((1, TO, D), lambda b, t: (b, t, 0)),
        ],
        out_specs=pl.BlockSpec((1, TO, D), lambda b, t: (b, t, 0)),
        out_shape=jax.ShapeDtypeStruct((B, S, D), F32),
        compiler_params=_params(("parallel", "parallel")),
        name="out_proj",
    )(og, wT, x)


def _nsa_in_kernel(x_ref, ng_ref, wtok_ref, wfT_ref, gq_ref, gk_ref, ct_ref, st_ref,
                   cosT_ref, sinT_ref,
                   qT_ref, ks_ref, kw_ref, kc_ref, vc_ref, vsT_ref, vwT_ref, g_ref, sz_ref, cv_scr):
    t = pl.program_id(1)
    x = x_ref[0]
    n = x.shape[0]
    tq = qT_ref.shape[-1]
    h = _rms_rows(x, ng_ref[...]).astype(BF16)
    G = NSA_GROUPS
    lanes = [slice(u * tq, (u + 1) * tq) for u in range(n // tq)]

    fa = _dot_nt(wfT_ref[...], h)
    cos = cosT_ref[...]
    sin = sinT_ref[...]
    gq = gq_ref[...]
    for hd in range(NSA_HEADS):
        blk = fa[hd * NSA_HD:(hd + 1) * NSA_HD]
        ss = jnp.sum(blk * blk, axis=0, keepdims=True)
        qn = blk * lax.rsqrt(ss * (1.0 / NSA_DK) + RMS_EPS) * gq
        x1 = qn[0:16]
        x2 = qn[64:80]
        r1 = (x1 * cos - x2 * sin).astype(BF16)
        r2 = (x2 * cos + x1 * sin).astype(BF16)
        qb = qn.astype(BF16)
        for u, sl in enumerate(lanes):
            qT_ref[0, hd, u, 0:16, :] = r1[:, sl]
            qT_ref[0, hd, u, 16:64, :] = qb[16:64, sl]
            qT_ref[0, hd, u, 64:80, :] = r2[:, sl]
            qT_ref[0, hd, u, 80:128, :] = qb[80:128, sl]
    off = NSA_HEADS * NSA_HD
    ones = jnp.ones((ONES, n), BF16)
    for ref in (vsT_ref, vwT_ref):
        for g in range(G):
            ref[0, g, 0, 0:NSA_DV, :] = fa[off + g * NSA_DV:off + (g + 1) * NSA_DV].astype(BF16)
            ref[0, g, 0, NSA_DV:NSA_DV + ONES, :] = ones
        off += G * NSA_DV
    sg = _sigmoid(fa[off:off + 16 * G])
    for g in range(G):
        for u, sl in enumerate(lanes):
            g_ref[0, g, u] = sg[g * 16:(g + 1) * 16, sl]
    off += 16 * G
    z = fa[off:off + NSA_HEADS * NSA_DV]
    sz = z * _sigmoid(z)
    for u, sl in enumerate(lanes):
        sz_ref[0, u] = sz[:, sl]

    pt = _dot(h, wtok_ref[...])
    ct = ct_ref[...]
    st = st_ref[...]
    nblk = ks_ref.shape[-1] - NSA_HD
    row = lax.broadcasted_iota(jnp.int32, (n, nblk), 0) + t * n
    col = lax.broadcasted_iota(jnp.int32, (n, nblk), 1)
    onehot = jnp.where((row // SEL_LEN) == col, 1.0, 0.0).astype(BF16)
    for br, ref in ((0, ks_ref), (1, kw_ref)):
        gk = gk_ref[br + 1:br + 2, :]
        for g in range(G):
            kb = pt[:, (br * G + g) * NSA_HD:(br * G + g + 1) * NSA_HD]
            ss = jnp.sum(kb * kb, axis=-1, keepdims=True)
            kn = kb * lax.rsqrt(ss * (1.0 / NSA_DK) + RMS_EPS) * gk
            kr = kn * ct + pltpu.roll(kn, 64, axis=1) * st
            ref[0, g, :, 0:NSA_HD] = kr.astype(BF16)
            if br == 0:
                ref[0, g, :, NSA_HD:NSA_HD + nblk] = onehot
    for br, ref in ((2, kc_ref), (3, vc_ref)):
        for g in range(G):
            stage = cv_scr.at[(br - 2) * G + g]
            stage[...] = pt[:, (br * G + g) * LANE:(br * G + g + 1) * LANE]
            for l in range(CMP_STRIDE):
                ref[0, g, :, l * LANE:(l + 1) * LANE] = stage[pl.ds(l, n // CMP_STRIDE, stride=CMP_STRIDE), :]


def _nsa_in(x, ng, w, tabs):
    B, S, D = x.shape
    nt = S // TT
    G = NSA_GROUPS
    nblk = S // SEL_LEN
    r = TT // TQ_NSA
    nq = S // TQ_NSA
    assert TT == TKV and TT % TQ_NSA == 0
    full = lambda shape: pl.BlockSpec(shape, lambda b, t: (0,) * len(shape))
    tok = lambda b, t: (b * nt + t, 0)
    feat = lambda b, t: (0, b * nt + t)
    tokmaj = lambda width: pl.BlockSpec((1, G, TT, width), lambda b, t: (b, 0, t, 0))
    vtile = pl.BlockSpec((1, G, 1, NSA_DV + ONES, TT), lambda b, t: (b, 0, t, 0, 0))
    cmp_in = pl.BlockSpec((1, G, TT // CMP_STRIDE, CMP_STRIDE * LANE), lambda b, t: (b, 0, t, 0))
    vshape = jax.ShapeDtypeStruct((B, G, nt, NSA_DV + ONES, TT), BF16)
    return pl.pallas_call(
        _nsa_in_kernel,
        grid=(B, nt),
        in_specs=[
            pl.BlockSpec((1, TT, D), lambda b, t: (b, t, 0)),
            full((1, D)),
            full(w["wtok"].shape), full(w["wfT"].shape),
            full((NSA_HD, TT)), full((3, NSA_HD)),
            pl.BlockSpec((TT, LANE), tok), pl.BlockSpec((TT, LANE), tok),
            pl.BlockSpec((16, TT), feat), pl.BlockSpec((16, TT), feat),
        ],
        out_specs=[
            pl.BlockSpec((1, NSA_HEADS, r, NSA_HD, TQ_NSA), lambda b, t: (b, 0, t, 0, 0)),
            tokmaj(NSA_HD + nblk), tokmaj(NSA_HD), cmp_in, cmp_in,
            vtile, vtile,
            pl.BlockSpec((1, G, r, 16, TQ_NSA), lambda b, t: (b, 0, t, 0, 0)),
            pl.BlockSpec((1, r, NSA_HEADS * NSA_DV, TQ_NSA), lambda b, t: (b, t, 0, 0)),
        ],
        out_shape=[
            jax.ShapeDtypeStruct((B, NSA_HEADS, nq, NSA_HD, TQ_NSA), BF16),
            jax.ShapeDtypeStruct((B, G, S, NSA_HD + nblk), BF16),
            jax.ShapeDtypeStruct((B, G, S, NSA_HD), BF16),
            jax.ShapeDtypeStruct((B, G, S // CMP_STRIDE, CMP_STRIDE * LANE), F32),
            jax.ShapeDtypeStruct((B, G, S // CMP_STRIDE, CMP_STRIDE * LANE), F32),
            vshape, vshape,
            jax.ShapeDtypeStruct((B, G, nq, 16, TQ_NSA), F32),
            jax.ShapeDtypeStruct((B, nq, NSA_HEADS * NSA_DV, TQ_NSA), F32),
        ],
        scratch_shapes=[pltpu.VMEM((2 * G, TT, LANE), F32)],
        compiler_params=_params(("parallel", "parallel")),
        name="nsa_in_proj",
    )(x, ng, w["wtok"], w["wfT"], w["gq"], w["gk"], tabs["ct_nsa"], tabs["st_nsa"],
      tabs["cosT_nsa"], tabs["sinT_nsa"])


def _nsa_cmp_kernel(kc_ref, vc_ref, pek_ref, pev_ref, w1k_ref, w2k_ref, w1v_ref, w2vT_ref,
                    gk_ref, ct_ref, st_ref, kcmp_ref, vcmpT_ref):
    nrow = kc_ref.shape[2]

    def pre(x, pe_ref, w1_ref):
        lo = _dot((x + pe_ref[0:1, :]).astype(BF16), w1_ref[0])
        hi = _dot((x + pe_ref[1:2, :]).astype(BF16), w1_ref[1])
        return lo + pltpu.roll(hi, nrow - 1, axis=0)

    a = pre(kc_ref[0, 0], pek_ref, w1k_ref)
    kc = _dot((a * _sigmoid(a)).astype(BF16), w2k_ref[...])
    ss = jnp.sum(kc * kc, axis=-1, keepdims=True)
    kn = kc * lax.rsqrt(ss * (1.0 / NSA_DK) + RMS_EPS) * gk_ref[0:1, :]
    kr = kn * ct_ref[0] + pltpu.roll(kn, 64, axis=1) * st_ref[0]
    kcmp_ref[0, 0] = kr.astype(BF16)

    a = pre(vc_ref[0, 0], pev_ref, w1v_ref)
    sv = (a * _sigmoid(a)).astype(BF16)
    vcmpT_ref[0, 0, 0:NSA_DV, :] = _dot_nt(w2vT_ref[...], sv).astype(BF16)
    vcmpT_ref[0, 0, NSA_DV:NSA_DV + ONES, :] = jnp.ones((ONES, nrow), BF16)


def _nsa_cmp(kc16, vc16, w, tabs):
    B, G, nrow, width = kc16.shape
    full = lambda shape: pl.BlockSpec(shape, lambda b, g: (0,) * len(shape))
    blk = pl.BlockSpec((1, 1, nrow, width), lambda b, g: (b, g, 0, 0))
    tab = pl.BlockSpec((1, nrow, LANE), lambda b, g: (b, 0, 0))
    return pl.pallas_call(
        _nsa_cmp_kernel,
        grid=(B, G),
        in_specs=[blk, blk, full((2, width)), full((2, width)),
                  full(w["w1k"].shape), full(w["w2k"].shape),
                  full(w["w1v"].shape), full(w["w2vT"].shape),
                  full((3, NSA_HD)), tab, tab],
        out_specs=[pl.BlockSpec((1, 1, nrow, NSA_HD), lambda b, g: (b, g, 0, 0)),
                   pl.BlockSpec((1, 1, NSA_DV + ONES, nrow), lambda b, g: (b, g, 0, 0))],
        out_shape=[jax.ShapeDtypeStruct((B, G, nrow, NSA_HD), BF16),
                   jax.ShapeDtypeStruct((B, G, NSA_DV + ONES, nrow), BF16)],
        compiler_params=_params(("parallel", "parallel")),
        name="nsa_compress",
    )(kc16, vc16, w["pek"], w["pev"], w["w1k"], w["w2k"], w["w1v"], w["w2vT"], w["gk"],
      tabs["ct_cmp"], tabs["st_cmp"])


def _tile4(a):
    return jnp.concatenate([a] * NSA_HPG, axis=1)


def _cmp_exp(s_ref, bias):
    n, w = s_ref.shape
    fold = lambda a, op: op(a.reshape(a.shape[0] // SUB, SUB, w), axis=0)
    rows = [slice(lo, lo + CHUNK) for lo in range(0, n, CHUNK)]
    mx = None
    for sl in rows:
        blk = bias[sl.start](s_ref[sl, :])
        s_ref[sl, :] = blk
        mx = fold(blk, jnp.max) if mx is None else jnp.maximum(mx, fold(blk, jnp.max))
    mc = jnp.maximum(jnp.max(mx, axis=0, keepdims=True), 0.5 * NEG)
    return jnp.concatenate([jnp.exp2(s_ref[sl, :] - mc).astype(BF16) for sl in rows], axis=0)


def _nsa_pre_kernel(qT_ref, kw_ref, vwT_ref, kc_ref, vcT_ref, ovT_ref, g_ref,
                    csel_ref, ocw_ref, *scratch, k_sel):
    nq, _, tq = qT_ref.shape[2:]
    ncmp = kc_ref.shape[2]
    nblk = ovT_ref.shape[0]

    wk = WINDOW + tq
    rloc = lax.broadcasted_iota(jnp.int32, (CHUNK, tq), 0)
    cloc = lax.broadcasted_iota(jnp.int32, (CHUNK, tq), 1)
    mask = lambda ok, lo: (lambda blk: blk + _tile4(jnp.where(ok(lo), 0.0, NEG)))

    def one_tile(t, u, clamped, jlim):
        q0 = t * tq
        qT = jnp.concatenate([qT_ref[0, hh, t] for hh in range(NSA_HPG)], axis=1)
        tok = q0 + lax.broadcasted_iota(jnp.int32, (1, tq), 1)

        sc_scr, sw_scr = scratch[2 * u], scratch[2 * u + 1]
        sc_scr[...] = _dot(kc_ref[0, 0], qT)
        yield
        cvalid = lambda lo: ((lo + rloc) * CMP_STRIDE + (CMP_LEN - 1)) <= tok
        cbias = {lo: mask(cvalid, lo) for lo in range(0, ncmp, CHUNK)}
        ec = _cmp_exp(sc_scr, cbias)
        ac = _dot(vcT_ref[0, 0], ec)
        dc = ac[NSA_DV:NSA_DV + 1]
        inv = 1.0 / jnp.where(dc > 0, dc, 1.0)
        o_c = ac[0:NSA_DV] * inv

        if clamped:
            start, off = 0, 0
            wbias = {lo: mask(lambda lo: lo + rloc <= tok, lo) for lo in range(0, wk, CHUNK)}
        else:
            start, off = q0 - WINDOW, (u % (TKV // tq)) * tq
            wbias = {lo: mask(lambda lo: lo + rloc > cloc, lo) for lo in range(0, tq, CHUNK)}
            wbias.update({lo: mask(lambda lo: lo - WINDOW + rloc <= cloc, lo)
                          for lo in range(WINDOW, wk, CHUNK)})
        sw_scr[...] = _dot(kw_ref[0, 0, pl.ds(pl.multiple_of(start, tq), wk), :], qT)
        yield
        w0 = start // TKV
        segs, pos, tile = [], 0, 0
        while pos < wk:
            width = min(TKV - off, wk - pos)
            segs.append((pos, pos + width, functools.partial(
                lambda tile, off, width: vwT_ref[0, 0, w0 + tile, :, off:off + width], tile, off, width)))
            pos, tile, off = pos + width, tile + 1, 0
        init = (jnp.full((1, NSA_HPG * tq), M_INIT, F32),
                jnp.zeros((NSA_DV + ONES, NSA_HPG * tq), F32))
        _, aw = _softmax_step_ref(sw_scr, init, segs=segs, bias=wbias)
        o_w = aw[0:NSA_DV] * (1.0 / aw[NSA_DV:NSA_DV + 1])
        yield

        if jlim:
            iw = _dot(ovT_ref[0:jlim, :], ec) * inv
            imp = iw[:, 0:tq]
            for hh in range(1, NSA_HPG):
                imp = imp + iw[:, hh * tq:(hh + 1) * tq]
            bidx = lax.broadcasted_iota(jnp.int32, (jlim, tq), 0)
            cur = tok // SEL_LEN
            forced = (bidx == 0) | (bidx == cur) | (bidx == cur - 1)
            valid = (bidx * SEL_LEN) <= tok
            score = jnp.where(forced, SEL_FORCE, jnp.where(valid, imp, -1.0))
            grp = [score[r * SUB:(r + 1) * SUB] for r in range(jlim // SUB)]
            bsub = lax.broadcasted_iota(jnp.int32, (SUB, tq), 0)
            cnt = [jnp.zeros((SUB, tq), F32) for _ in grp]
            for j in range(jlim):
                rowj = score[j:j + 1, :]
                for r in range(jlim // SUB):
                    if r * SUB > j:
                        cnt[r] = cnt[r] + jnp.where(rowj >= grp[r], 1.0, 0.0)
                    elif r * SUB + SUB - 1 < j:
                        cnt[r] = cnt[r] + jnp.where(rowj > grp[r], 1.0, 0.0)
                    else:
                        tie = jnp.where(bsub + r * SUB > j, 1.0, 0.0)
                        cnt[r] = cnt[r] + jnp.where(rowj > grp[r], 1.0, 0.0)
                        cnt[r] = cnt[r] + jnp.where(rowj == grp[r], tie, 0.0)
            cnt = jnp.concatenate(cnt, axis=0)
            csel_ref[0, 0, t, 0:jlim, :] = jnp.where(cnt < k_sel, 0.0, NEG).astype(BF16)
        if jlim < nblk:
            csel_ref[0, 0, t, jlim:nblk, :] = jnp.zeros((nblk - jlim, tq), BF16)

        for hh in range(NSA_HPG):
            sl = slice(hh * tq, (hh + 1) * tq)
            g0 = g_ref[0, 0, t, hh:hh + 1, :]
            g2 = g_ref[0, 0, t, 2 * NSA_HPG + hh:2 * NSA_HPG + hh + 1, :]
            ocw_ref[0, 0, t, hh * NSA_DV:(hh + 1) * NSA_DV, :] = g0 * o_c[:, sl] + g2 * o_w[:, sl]

    def make_trip(clamped, jlim):
        def trip(i, _):
            live = [one_tile(i * PRE_TILES + u, u, clamped, jlim) for u in range(PRE_TILES)]
            while live:
                live = [g for g in live if next(g, "done") != "done"]
            return 0
        return trip

    step = tq * PRE_TILES
    n_trips = nq // PRE_TILES
    n_clamped = min(WINDOW // step, n_trips)
    n_norank = min(max((k_sel * SEL_LEN) // step, n_clamped), n_trips)
    cuts = sorted({0, n_clamped, n_norank, n_trips}
                  | {n_norank + (k * (n_trips - n_norank)) // RANK_CLASSES for k in range(RANK_CLASSES)})
    for lo, hi in zip(cuts[:-1], cuts[1:]):
        causal_blocks = -(-(hi * step) // SEL_LEN)
        jlim = 0 if hi <= n_norank else min(nblk, -(-causal_blocks // SUB) * SUB)
        lax.fori_loop(lo, hi, make_trip(lo < n_clamped, jlim), 0)


def _nsa_pre(qT, kw, vwT, kcmp, vcmpT, ovT, gates):
    B, H, nq, _, tq = qT.shape
    G = NSA_GROUPS
    S = kw.shape[2]
    nblk = S // SEL_LEN
    ncmp = kcmp.shape[2]
    assert TKV % tq == 0 and nq % PRE_TILES == 0 and PRE_TILES % (TKV // tq) == 0
    assert WINDOW % (tq * PRE_TILES) == 0 and WINDOW % TKV == 0 and S >= WINDOW + tq
    kern = functools.partial(_nsa_pre_kernel, k_sel=min(N_SELECT, nblk))
    per_bg = lambda shape: pl.BlockSpec((1, 1) + shape, lambda b, g: (b, g) + (0,) * len(shape))
    wide = NSA_HPG * tq
    return pl.pallas_call(
        kern,
        grid=(B, G),
        in_specs=[
            pl.BlockSpec((1, NSA_HPG, nq, NSA_HD, tq), lambda b, g: (b, g, 0, 0, 0)),
            per_bg((S, NSA_HD)),
            per_bg((S // TKV, NSA_DV + ONES, TKV)),
            per_bg((ncmp, NSA_HD)),
            per_bg((NSA_DV + ONES, ncmp)),
            pl.BlockSpec((nblk, ncmp), lambda b, g: (0, 0)),
            per_bg((nq, 16, tq)),
        ],
        out_specs=[per_bg((nq, nblk, tq)), per_bg((nq, NSA_HPG * NSA_DV, tq))],
        out_shape=[jax.ShapeDtypeStruct((B, G, nq, nblk, tq), BF16),
                   jax.ShapeDtypeStruct((B, G, nq, NSA_HPG * NSA_DV, tq), F32)],
        scratch_shapes=[pltpu.VMEM((ncmp, wide), F32), pltpu.VMEM((WINDOW + tq, wide), F32)] * PRE_TILES,
        compiler_params=_params(("parallel", "parallel")),
        name="nsa_branches",
    )(qT, kw, vwT, kcmp, vcmpT, ovT, gates)


def _nsa_sel_kernel(qT_ref, csel_ref, ks_ref, vsT_ref, ocw_ref, g_ref, sz_ref, o_ref,
                    qaug_ref, s_a, s_b):
    nq, _, tq = qT_ref.shape[2:]
    nblk = csel_ref.shape[3]
    nsub = TSTEP // TKV
    per = TQ_SEL // tq
    wide = NSA_HPG * TQ_SEL

    row = lax.broadcasted_iota(jnp.int32, (CHUNK, TQ_SEL), 0)
    col = lax.broadcasted_iota(jnp.int32, (CHUNK, TQ_SEL), 1)

    def qtile(sup, p):
        i = sup * (TSTEP // TQ_SEL) + p
        q0 = i * TQ_SEL
        for hh in range(NSA_HPG):
            for u in range(per):
                lanes = slice(hh * TQ_SEL + u * tq, hh * TQ_SEL + (u + 1) * tq)
                qaug_ref[0:NSA_HD, lanes] = qT_ref[0, hh, i * per + u]
                qaug_ref[NSA_HD:NSA_HD + nblk, lanes] = csel_ref[0, 0, i * per + u]
        qaug = qaug_ref[...]

        def scores(j):
            kt = ks_ref[0, 0, pl.ds(pl.multiple_of(j * TSTEP, TSTEP), TSTEP), :]
            return _dot(kt, qaug)

        def vtile(j):
            return lambda r: vsT_ref[0, 0, j * nsub + r]

        own = p * TQ_SEL
        causal = {own + lo: functools.partial(
            lambda lo, blk: blk + _tile4(jnp.where(lo + row <= col, 0.0, NEG)), lo)
            for lo in range(0, TQ_SEL, CHUNK)}
        init = (jnp.full((1, wide), M_INIT, F32), jnp.zeros((NSA_DV + ONES, wide), F32))
        _, acc = _sweep(scores, vtile, causal, own + TQ_SEL, sup, init, s_a, s_b)
        o_s = acc[0:NSA_DV] * (1.0 / acc[NSA_DV:NSA_DV + 1])

        for hh in range(NSA_HPG):
            rows = slice(hh * NSA_DV, (hh + 1) * NSA_DV)
            for u in range(per):
                t = i * per + u
                lanes = slice(hh * TQ_SEL + u * tq, hh * TQ_SEL + (u + 1) * tq)
                g1 = g_ref[0, 0, t, NSA_HPG + hh:NSA_HPG + hh + 1, :]
                o = ocw_ref[0, 0, t, rows, :] + g1 * o_s[:, lanes]
                o_ref[0, t, rows, :] = (o * sz_ref[0, t, rows, :]).astype(BF16)

    def super_tile(sup, _):
        for p in range(TSTEP // TQ_SEL):
            qtile(sup, p)
        return 0

    lax.fori_loop(0, (nq * tq) // TSTEP, super_tile, 0)


def _nsa_sel(qT, csel, ks, vsT, ocw, gates, sz):
    B, H, nq, _, tq = qT.shape
    G = NSA_GROUPS
    S = ks.shape[2]
    nblk = S // SEL_LEN
    assert TQ_SEL % tq == 0 and TSTEP % TQ_SEL == 0 and TSTEP % TKV == 0 and S % TSTEP == 0
    per_bg = lambda shape: pl.BlockSpec((1, 1) + shape, lambda b, g: (b, g) + (0,) * len(shape))
    gate = pl.BlockSpec((1, nq, NSA_HPG * NSA_DV, tq), lambda b, g: (b, 0, g, 0))
    return pl.pallas_call(
        _nsa_sel_kernel,
        grid=(B, G),
        in_specs=[
            pl.BlockSpec((1, NSA_HPG, nq, NSA_HD, tq), lambda b, g: (b, g, 0, 0, 0)),
            per_bg((nq, nblk, tq)),
            per_bg((S, NSA_HD + nblk)),
            per_bg((S // TKV, NSA_DV + ONES, TKV)),
            per_bg((nq, NSA_HPG * NSA_DV, tq)),
            per_bg((nq, 16, tq)),
            gate,
        ],
        out_specs=gate,
        out_shape=jax.ShapeDtypeStruct((B, nq, H * NSA_DV, tq), BF16),
        scratch_shapes=[pltpu.VMEM((NSA_HD + nblk, NSA_HPG * TQ_SEL), BF16)]
        + [pltpu.VMEM((TSTEP, NSA_HPG * TQ_SEL), F32)] * 2,
        compiler_params=_params(("parallel", "parallel")),
        name="nsa_selected",
    )(qT, csel, ks, vsT, ocw, gates, sz)


def _nsa_perm():
    src = np.full((NSA_HD,), -1, np.int64)
    src[0:12] = np.arange(0, 12)
    src[12:16] = np.arange(24, 28)
    src[16:64] = np.arange(28, 76)
    src[64:76] = np.arange(12, 24)
    src[76:80] = np.arange(76, 80)
    src[80:96] = np.arange(80, 96)
    return src


def _take_cols(w, src):
    idx = np.where(src >= 0, src, 0)
    out = jnp.take(w, jnp.asarray(idx), axis=-1)
    return jnp.where(jnp.asarray(src >= 0), out, 0.0)


def _pad_last(w, width):
    return jnp.pad(w, [(0, 0)] * (w.ndim - 1) + [(0, width - w.shape[-1])])


def _prep_mla(w_in, g_cq, w_uq, g_ckv, w_ukv, g_q, g_k, w_out):
    D = w_in.shape[0]
    o1 = MLA_Q_LORA + MLA_KV_LORA
    z32 = jnp.zeros((D, 32), F32)
    wa = jnp.concatenate([w_in[:, :o1], w_in[:, o1:o1 + 32], z32, w_in[:, o1 + 32:o1 + 64], z32], axis=1)
    wq = w_uq.reshape(MLA_Q_LORA, MLA_HEADS, MLA_QK)
    zq = jnp.zeros((MLA_Q_LORA, MLA_HEADS, 32), F32)
    wq = jnp.concatenate([wq[:, :, 64:], wq[:, :, :32], zq, wq[:, :, 32:64], zq], axis=2)
    z1 = jnp.zeros((32,), F32)
    gq = jnp.concatenate([g_q[64:], g_q[:32], z1, g_q[32:64], z1]) * C_MLA
    wkv = w_ukv.reshape(MLA_KV_LORA, MLA_HEADS, MLA_NOPE + MLA_V)
    return {
        "wa": wa.astype(BF16),
        "wzT": w_in[:, o1 + MLA_ROPE:].T.astype(BF16),
        "gcq": g_cq.reshape(1, -1), "gckv": g_ckv.reshape(1, -1),
        "wuqT": wq.reshape(MLA_Q_LORA, MLA_HEADS * MLA_HD).T.astype(BF16),
        "gq": jnp.broadcast_to(gq[:, None], (MLA_HD, TT)),
        "wuk": wkv[:, :, :MLA_NOPE].reshape(MLA_KV_LORA, -1).astype(BF16),
        "wuvT": wkv[:, :, MLA_NOPE:].reshape(MLA_KV_LORA, -1).T.astype(BF16),
        "gkn": g_k[64:].reshape(1, -1),
        "gkp": jnp.concatenate([g_k[:32], z1, g_k[32:64], z1]).reshape(1, -1),
        "woT": w_out.T.astype(BF16),
    }


def _prep_nsa(w_in, g_q, g_k, pe_k, w1_k, w2_k, pe_v, w1_v, w2_v, w_out):
    D = w_in.shape[0]
    G = NSA_GROUPS
    src = _nsa_perm()
    offs = np.concatenate([[0], np.cumsum(NSA_SIZES)])
    part = lambda i: w_in[:, offs[i]:offs[i + 1]]
    q, kc, vc, ks, vs, kw, vw, gl, z = [part(i) for i in range(9)]
    perm_heads = lambda w, nh: _take_cols(w.reshape(D, nh, NSA_DK), src).reshape(D, nh * NSA_HD)
    pad_groups = lambda w, d: _pad_last(w.reshape(D, G, d), LANE).reshape(D, G * LANE)
    wtok = jnp.concatenate([perm_heads(ks, G), perm_heads(kw, G), pad_groups(kc, NSA_DK),
                            pad_groups(vc, NSA_DV)], axis=1)
    glr = gl.reshape(D, G, NSA_HPG, 3).transpose(0, 1, 3, 2).reshape(D, G, 3 * NSA_HPG)
    glr = _pad_last(glr, 16).reshape(D, G * 16)
    wf = jnp.concatenate([perm_heads(q, NSA_HEADS), vs, vw, glr, z], axis=1)
    halves = lambda w, d: _pad_last(w.reshape(2, CMP_LEN // 2, d, -1).transpose(0, 1, 3, 2), LANE) \
        .transpose(0, 1, 3, 2).reshape(2, (CMP_LEN // 2) * LANE, -1)
    pe_flat = lambda pe: _pad_last(pe, LANE).reshape(2, (CMP_LEN // 2) * LANE)
    return {
        "wtok": wtok.astype(BF16),
        "wfT": wf.T.astype(BF16),
        "gq": jnp.broadcast_to((_take_cols(g_q, src) * C_NSA)[:, None], (NSA_HD, TT)),
        "gk": _take_cols(g_k, src),
        "pek": pe_flat(pe_k), "pev": pe_flat(pe_v),
        "w1k": _pad_last(halves(w1_k, NSA_DK), LANE).astype(BF16),
        "w2k": _pad_last(_take_cols(w2_k, src).T, LANE).T.astype(BF16),
        "w1v": _pad_last(halves(w1_v, NSA_DV), LANE).astype(BF16),
        "w2vT": _pad_last(w2_v.T, LANE).astype(BF16),
        "woT": w_out.T.astype(BF16),
    }


def _overlap_T(S):
    ncmp = S // CMP_STRIDE
    nblk = S // SEL_LEN
    n_cmp = (S - CMP_LEN) // CMP_STRIDE + 1
    cs = np.arange(ncmp) * CMP_STRIDE
    ss = np.arange(nblk) * SEL_LEN
    ov = (cs[None, :] < ss[:, None] + SEL_LEN) & (cs[None, :] + CMP_LEN > ss[:, None])
    ov = ov & (np.arange(ncmp)[None, :] < n_cmp)
    return jnp.asarray(ov.astype(np.float32)).astype(BF16)


def _mla_layer(x, ng, w, tabs):
    qT, k, vT, sz = _mla_in(x, ng.reshape(1, -1), w, tabs)
    og = _mla_attn(qT, k, vT, sz)
    return _out_proj(og, w["woT"], x)


def _nsa_layer(x, ng, w, tabs, ovT):
    B, S, _ = x.shape
    qT, ks, kw, kc16, vc16, vsT, vwT, gates, sz = _nsa_in(x, ng.reshape(1, -1), w, tabs)
    kcmp, vcmpT = _nsa_cmp(kc16, vc16, w, tabs)
    csel, ocw = _nsa_pre(qT, kw, vwT, kcmp, vcmpT, ovT, gates)
    og = _nsa_sel(qT, csel, ks, vsT, ocw, gates, sz)
    return _out_proj(og, w["woT"], x)


def kernel(x, positions, norm_g, mla_w_in, mla_g_cq, mla_w_uq, mla_g_ckv, mla_w_ukv, mla_g_q, mla_g_k, mla_w_out, nsa_w_in, nsa_g_q, nsa_g_k, nsa_pe_k, nsa_w1_k, nsa_w2_k, nsa_pe_v, nsa_w1_v, nsa_w2_v, nsa_w_out):
    B, S, _ = x.shape
    tabs = _rope_tables(positions)
    ovT = _overlap_T(S)
    w_mla = jax.vmap(_prep_mla)(mla_w_in, mla_g_cq, mla_w_uq, mla_g_ckv, mla_w_ukv, mla_g_q,
                                mla_g_k, mla_w_out)
    w_nsa = jax.vmap(_prep_nsa)(nsa_w_in, nsa_g_q, nsa_g_k, nsa_pe_k, nsa_w1_k, nsa_w2_k,
                                nsa_pe_v, nsa_w1_v, nsa_w2_v, nsa_w_out)
    for i in range(DEPTH):
        j = i // N_MIXERS
        if i % N_MIXERS == 0:
            x = _mla_layer(x, norm_g[i], {k: v[j] for k, v in w_mla.items()}, tabs)
        else:
            x = _nsa_layer(x, norm_g[i], {k: v[j] for k, v in w_nsa.items()}, tabs, ovT)
    return x
```

```python
import functools

import numpy as np
import jax
import jax.numpy as jnp
from jax import lax
from jax.experimental import pallas as pl
from jax.experimental.pallas import tpu as pltpu

F32 = jnp.float32
BF16 = jnp.bfloat16

D_MODEL = 1024
DEPTH = 4
N_MIXERS = 2
ROPE_THETA = 500000.0
RMS_EPS = 1e-6
MLA_HEADS = 16
MLA_NOPE = 128
MLA_ROPE = 64
MLA_V = 128
MLA_QK = MLA_NOPE + MLA_ROPE
MLA_Q_LORA = 384
MLA_KV_LORA = 256
MLA_HD = 256
NSA_HEADS = 16
NSA_GROUPS = 4
NSA_HPG = NSA_HEADS // NSA_GROUPS
NSA_DK = 96
NSA_DV = 64
NSA_ROT = NSA_DK // 4
NSA_HALF = NSA_ROT // 2
NSA_HD = 128
CMP_LEN = 32
CMP_STRIDE = 16
SEL_LEN = 64
N_SELECT = 16
WINDOW = 512
SEL_FORCE = 1e4
NSA_SIZES = (NSA_HEADS * NSA_DK,
             NSA_GROUPS * NSA_DK, NSA_GROUPS * NSA_DV,
             NSA_GROUPS * NSA_DK, NSA_GROUPS * NSA_DV,
             NSA_GROUPS * NSA_DK, NSA_GROUPS * NSA_DV,
             3 * NSA_HEADS, NSA_HEADS * NSA_DV)

LANE = 128
SUB = 8
TT = 256
TO = 512
TKV = 256
TSTEP = 512
TQ_MLA = 512
TQ_NSA = 128
TQ_SEL = 256
PRE_TILES = 2
RANK_CLASSES = 3
CHUNK = 64
ONES = 16
NEG = -30000.0
M_INIT = -1e30
LOG2E = 1.4426950408889634
C_MLA = (MLA_QK ** -0.5) * LOG2E
C_NSA = (NSA_DK ** -0.5) * LOG2E
VMEM_LIMIT = 56 * 1024 * 1024


def _dot(a, b):
    return jnp.dot(a, b, preferred_element_type=F32)


def _dot_nt(a, b):
    return lax.dot_general(a, b, (((1,), (1,)), ((), ())), preferred_element_type=F32)


def _sigmoid(x):
    return 1.0 / (1.0 + jnp.exp(-x))


def _rms_rows(x, g):
    ms = jnp.mean(x * x, axis=-1, keepdims=True)
    return x * lax.rsqrt(ms + RMS_EPS) * g


def _params(sem):
    return pltpu.CompilerParams(dimension_semantics=sem, vmem_limit_bytes=VMEM_LIMIT)


def _rope_kernel(pos_ref, invf_ref, cm_ref, sm_ref, cn_ref, sn_ref, ctm_ref, stm_ref, ctn_ref, stn_ref):
    ang = invf_ref[...] * pos_ref[...]
    c, s = jnp.cos(ang), jnp.sin(ang)
    hm = MLA_ROPE // 2
    cm, sm, cn, sn = c[:hm], s[:hm], c[hm:hm + 16], s[hm:hm + 16]
    cm_ref[...], sm_ref[...], cn_ref[...], sn_ref[...] = cm, sm, cn, sn
    tn = ang.shape[1]
    one = lambda r: jnp.ones((r, tn), F32)
    zero = lambda r: jnp.zeros((r, tn), F32)
    ctm_ref[...] = jnp.concatenate([cm, one(32), cm, one(32)], axis=0).T
    stm_ref[...] = jnp.concatenate([-sm, zero(32), sm, zero(32)], axis=0).T
    ctn_ref[...] = jnp.concatenate([cn, one(48), cn, one(48)], axis=0).T
    stn_ref[...] = jnp.concatenate([-sn, zero(48), sn, zero(48)], axis=0).T


def _rope_tables(positions):
    B, S = positions.shape
    n = positions.size
    tn = 512
    pos = positions.reshape(1, n).astype(F32)
    half_m = MLA_ROPE // 2
    inv_m = ROPE_THETA ** (-jnp.arange(half_m, dtype=F32) / half_m)
    inv_n = ROPE_THETA ** (-jnp.arange(NSA_HALF, dtype=F32) / NSA_HALF)
    invf = jnp.concatenate([inv_m, inv_n, jnp.zeros((16 - NSA_HALF,), F32)])
    rows = invf.shape[0]
    invf = jnp.broadcast_to(invf[:, None], (rows, tn))
    feat = lambda r: pl.BlockSpec((r, tn), lambda i: (0, i))
    tok = pl.BlockSpec((tn, LANE), lambda i: (i, 0))
    cm, sm, cn, sn, ctm, stm, ctn, stn = pl.pallas_call(
        _rope_kernel,
        grid=(n // tn,),
        in_specs=[pl.BlockSpec((1, tn), lambda i: (0, i)),
                  pl.BlockSpec((rows, tn), lambda i: (0, 0))],
        out_specs=[feat(half_m), feat(half_m), feat(16), feat(16), tok, tok, tok, tok],
        out_shape=[jax.ShapeDtypeStruct((half_m, n), F32)] * 2
        + [jax.ShapeDtypeStruct((16, n), F32)] * 2 + [jax.ShapeDtypeStruct((n, LANE), F32)] * 4,
        compiler_params=_params(("parallel",)),
        name="rope_tables",
    )(pos, invf)
    ncmp = S // CMP_STRIDE
    last = CMP_LEN - 1
    pick = lambda t: jnp.pad(t.reshape(B, S, LANE)[:, last::CMP_STRIDE],
                             ((0, 0), (0, ncmp - (S - last + CMP_STRIDE - 1) // CMP_STRIDE), (0, 0)))
    return {"cosT_mla": cm, "sinT_mla": sm, "cosT_nsa": cn, "sinT_nsa": sn,
            "ct_mla": ctm, "st_mla": stm, "ct_nsa": ctn, "st_nsa": stn,
            "ct_cmp": pick(ctn), "st_cmp": pick(stn)}


def _mla_in_kernel(x_ref, ng_ref, wa_ref, wzT_ref, gcq_ref, gckv_ref, wuqT_ref, gq_ref,
                   wuk_ref, wuvT_ref, gkn_ref, gkp_ref, ct_ref, st_ref, cosT_ref, sinT_ref,
                   qT_ref, k_ref, vT_ref, sz_ref):
    x = x_ref[0]
    n = x.shape[0]
    h = _rms_rows(x, ng_ref[...]).astype(BF16)
    pa = _dot(h, wa_ref[...])
    z = _dot_nt(wzT_ref[...], h)
    sz_ref[0, 0] = z * _sigmoid(z)

    cqn = _rms_rows(pa[:, :MLA_Q_LORA], gcq_ref[...]).astype(BF16)
    ckvn = _rms_rows(pa[:, MLA_Q_LORA:MLA_Q_LORA + MLA_KV_LORA], gckv_ref[...]).astype(BF16)
    kpe = pa[:, MLA_Q_LORA + MLA_KV_LORA:]

    qa = _dot_nt(wuqT_ref[...], cqn)
    cos = cosT_ref[...]
    sin = sinT_ref[...]
    gq = gq_ref[...]
    zeros32 = jnp.zeros((32, n), BF16)
    for hd in range(MLA_HEADS):
        blk = qa[hd * MLA_HD:(hd + 1) * MLA_HD]
        ss = jnp.sum(blk * blk, axis=0, keepdims=True)
        qn = blk * lax.rsqrt(ss * (1.0 / MLA_QK) + RMS_EPS) * gq
        x1 = qn[128:160]
        x2 = qn[192:224]
        qT_ref[0, hd, 0, 0:128, :] = qn[0:128].astype(BF16)
        qT_ref[0, hd, 0, 128:160, :] = (x1 * cos - x2 * sin).astype(BF16)
        qT_ref[0, hd, 0, 160:192, :] = zeros32
        qT_ref[0, hd, 0, 192:224, :] = (x2 * cos + x1 * sin).astype(BF16)
        qT_ref[0, hd, 0, 224:256, :] = zeros32

    kn = _dot(ckvn, wuk_ref[...])
    ss_pe = jnp.sum(kpe * kpe, axis=-1, keepdims=True)
    kpg = kpe * gkp_ref[...]
    prot = kpg * ct_ref[...] + pltpu.roll(kpg, 64, axis=1) * st_ref[...]
    gkn = gkn_ref[...]
    for hd in range(MLA_HEADS):
        kb = kn[:, hd * MLA_NOPE:(hd + 1) * MLA_NOPE]
        ss = jnp.sum(kb * kb, axis=-1, keepdims=True) + ss_pe
        r = lax.rsqrt(ss * (1.0 / MLA_QK) + RMS_EPS)
        k_ref[0, hd, :, 0:128] = (kb * r * gkn).astype(BF16)
        k_ref[0, hd, :, 128:256] = (prot * r).astype(BF16)

    va = _dot_nt(wuvT_ref[...], ckvn)
    ones = jnp.ones((ONES, n), BF16)
    for hd in range(MLA_HEADS):
        vT_ref[0, hd, 0, 0:MLA_V, :] = va[hd * MLA_V:(hd + 1) * MLA_V].astype(BF16)
        vT_ref[0, hd, 0, MLA_V:MLA_V + ONES, :] = ones


def _mla_in(x, ng, w, tabs):
    B, S, D = x.shape
    nt = S // TT
    r = TQ_MLA // TT
    assert TT == TKV and TQ_MLA % TT == 0
    full = lambda shape: pl.BlockSpec(shape, lambda b, t: (0,) * len(shape))
    tok = lambda b, t: (b * nt + t, 0)
    feat = lambda b, t: (0, b * nt + t)
    return pl.pallas_call(
        _mla_in_kernel,
        grid=(B, nt),
        in_specs=[
            pl.BlockSpec((1, TT, D), lambda b, t: (b, t, 0)),
            full((1, D)),
            full(w["wa"].shape), full(w["wzT"].shape),
            full((1, MLA_Q_LORA)), full((1, MLA_KV_LORA)),
            full(w["wuqT"].shape), full((MLA_HD, TT)),
            full(w["wuk"].shape), full(w["wuvT"].shape),
            full((1, MLA_NOPE)), full((1, LANE)),
            pl.BlockSpec((TT, LANE), tok), pl.BlockSpec((TT, LANE), tok),
            pl.BlockSpec((32, TT), feat), pl.BlockSpec((32, TT), feat),
        ],
        out_specs=[
            pl.BlockSpec((1, MLA_HEADS, 1, MLA_HD, TT), lambda b, t: (b, 0, t // r, 0, t % r)),
            pl.BlockSpec((1, MLA_HEADS, TT, MLA_HD), lambda b, t: (b, 0, t, 0)),
            pl.BlockSpec((1, MLA_HEADS, 1, MLA_V + ONES, TT), lambda b, t: (b, 0, t, 0, 0)),
            pl.BlockSpec((1, 1, MLA_HEADS * MLA_V, TT), lambda b, t: (b, t // r, 0, t % r)),
        ],
        out_shape=[
            jax.ShapeDtypeStruct((B, MLA_HEADS, S // TQ_MLA, MLA_HD, TQ_MLA), BF16),
            jax.ShapeDtypeStruct((B, MLA_HEADS, S, MLA_HD), BF16),
            jax.ShapeDtypeStruct((B, MLA_HEADS, nt, MLA_V + ONES, TT), BF16),
            jax.ShapeDtypeStruct((B, S // TQ_MLA, MLA_HEADS * MLA_V, TQ_MLA), F32),
        ],
        compiler_params=_params(("parallel", "parallel")),
        name="mla_in_proj",
    )(x, ng, w["wa"], w["wzT"], w["gcq"], w["gckv"], w["wuqT"], w["gq"], w["wuk"],
      w["wuvT"], w["gkn"], w["gkp"], tabs["ct_mla"], tabs["st_mla"], tabs["cosT_mla"],
      tabs["sinT_mla"])


def _softmax_step_ref(s_ref, carry, vtile=None, segs=None, bias=None):
    m, acc = carry
    n, w = s_ref.shape
    segs = segs or [(t * TKV, (t + 1) * TKV, functools.partial(vtile, t)) for t in range(n // TKV)]
    bias = bias or {}
    fold = lambda a, op: op(a.reshape(a.shape[0] // SUB, SUB, w), axis=0)
    mx = None
    for lo in range(0, n, CHUNK):
        blk = s_ref[lo:lo + CHUNK, :]
        if lo in bias:
            blk = bias[lo](blk) if callable(bias[lo]) else blk + bias[lo]
            s_ref[lo:lo + CHUNK, :] = blk
        mx = fold(blk, jnp.max) if mx is None else jnp.maximum(mx, fold(blk, jnp.max))
    m_new = jnp.maximum(m, jnp.max(mx, axis=0, keepdims=True))
    alpha = jnp.exp2(m - m_new)
    pv = None
    for seg_lo, seg_hi, value_tile in segs:
        parts = [jnp.exp2(s_ref[lo:lo + CHUNK, :] - m_new).astype(BF16)
                 for lo in range(seg_lo, seg_hi, CHUNK)]
        d = _dot(value_tile(), jnp.concatenate(parts, axis=0))
        pv = d if pv is None else pv + d
    return m_new, alpha * acc + pv


def _sweep(scores, vtile, last_bias, last_rows, n_full, carry, s_a, s_b):
    def step(src, dst, j, carry):
        dst[...] = scores(j + 1)
        return _softmax_step_ref(src, carry, vtile(j))

    def last(src, j, carry):
        return _softmax_step_ref(src.at[0:last_rows], carry, vtile(j), bias=last_bias)

    def pair(t, carry):
        carry = step(s_a, s_b, 2 * t, carry)
        return step(s_b, s_a, 2 * t + 1, carry)

    s_a[...] = scores(0)
    if isinstance(n_full, int):
        bufs = (s_a, s_b)
        for j in range(n_full):
            carry = step(bufs[j % 2], bufs[(j + 1) % 2], j, carry)
        return last(bufs[n_full % 2], n_full, carry)
    carry = lax.fori_loop(0, n_full // 2, pair, carry)
    odd = lambda carry: last(s_b, n_full, step(s_a, s_b, n_full - 1, carry))
    even = lambda carry: last(s_a, n_full, carry)
    return lax.cond(n_full % 2 == 1, odd, even, carry)


def _mla_attn_kernel(qT_ref, k_ref, vT_ref, sz_ref, o_ref, s_a, s_b):
    nq, _, tq = qT_ref.shape[2:]
    nsub = TSTEP // TKV

    row = lax.broadcasted_iota(jnp.int32, (CHUNK, tq), 0)
    col = lax.broadcasted_iota(jnp.int32, (CHUNK, tq), 1)
    causal = {lo: functools.partial(lambda lo, blk: jnp.where(lo + row <= col, blk, NEG), lo)
              for lo in range(0, TSTEP, CHUNK)}

    def qtile(i, _):
        q = qT_ref[0, 0, i]

        def scores(j):
            kt = k_ref[0, 0, j * TSTEP:(j + 1) * TSTEP, :]
            return _dot(kt, q)

        def vtile(j):
            return lambda r: vT_ref[0, 0, j * nsub + r]

        init = (jnp.full((1, tq), M_INIT, F32), jnp.zeros((MLA_V + ONES, tq), F32))
        _, acc = _sweep(scores, vtile, causal, TSTEP, i, init, s_a, s_b)
        o = acc[0:MLA_V] / acc[MLA_V:MLA_V + 1]
        o_ref[0, i] = (o * sz_ref[0, i]).astype(BF16)
        return 0

    for i in range(nq):
        qtile(i, 0)


def _mla_attn(qT, k, vT, sz):
    B, H, nq, _, tq = qT.shape
    S = k.shape[2]
    assert tq == TSTEP and TSTEP % TKV == 0
    per_bh = lambda shape: pl.BlockSpec((1, 1) + shape, lambda b, h: (b, h) + (0,) * len(shape))
    gate = pl.BlockSpec((1, nq, MLA_V, tq), lambda b, h: (b, 0, h, 0))
    return pl.pallas_call(
        _mla_attn_kernel,
        grid=(B, H),
        in_specs=[per_bh((nq, MLA_HD, tq)), per_bh((S, MLA_HD)),
                  per_bh((S // TKV, MLA_V + ONES, TKV)), gate],
        out_specs=gate,
        out_shape=jax.ShapeDtypeStruct((B, nq, H * MLA_V, tq), BF16),
        scratch_shapes=[pltpu.VMEM((TSTEP, tq), F32)] * 2,
        compiler_params=_params(("parallel", "parallel")),
        name="mla_attention",
    )(qT, k, vT, sz)


def _out_proj_kernel(og_ref, wT_ref, x_ref, o_ref):
    ntile = og_ref.shape[1]
    og = og_ref[0, 0] if ntile == 1 else jnp.concatenate([og_ref[0, r] for r in range(ntile)], axis=1)
    yT = _dot(wT_ref[...], og)
    o_ref[0] = x_ref[0] + yT.T


def _out_proj(og, wT, x):
    B, S, D = x.shape
    _, _, K, W = og.shape
    r = TO // W
    return pl.pallas_call(
        _out_proj_kernel,
        grid=(B, S // TO),
        in_specs=[
            pl.BlockSpec((1, r, K, W), lambda b, t: (b, t, 0, 0)),
            pl.BlockSpec((D, K), lambda b, t: (0, 0)),
            pl.BlockSpec((1, TO, D), lambda b, t: (b, t, 0)),
        ],
        out_specs=pl.BlockSpec((1, TO, D), lambda b, t: (b, t, 0)),
        out_shape=jax.ShapeDtypeStruct((B, S, D), F32),
        compiler_params=_params(("parallel", "parallel")),
        name="out_proj",
    )(og, wT, x)


def _nsa_in_kernel(x_ref, ng_ref, wtok_ref, wfT_ref, gq_ref, gk_ref, ct_ref, st_ref,
                   cosT_ref, sinT_ref,
                   qT_ref, ks_ref, kw_ref, kc_ref, vc_ref, vsT_ref, vwT_ref, g_ref, sz_ref, cv_scr):
    t = pl.program_id(1)
    x = x_ref[0]
    n = x.shape[0]
    tq = qT_ref.shape[-1]
    h = _rms_rows(x, ng_ref[...]).astype(BF16)
    G = NSA_GROUPS
    lanes = [slice(u * tq, (u + 1) * tq) for u in range(n // tq)]

    fa = _dot_nt(wfT_ref[...], h)
    cos = cosT_ref[...]
    sin = sinT_ref[...]
    gq = gq_ref[...]
    for hd in range(NSA_HEADS):
        blk = fa[hd * NSA_HD:(hd + 1) * NSA_HD]
        ss = jnp.sum(blk * blk, axis=0, keepdims=True)
        qn = blk * lax.rsqrt(ss * (1.0 / NSA_DK) + RMS_EPS) * gq
        x1 = qn[0:16]
        x2 = qn[64:80]
        r1 = (x1 * cos - x2 * sin).astype(BF16)
        r2 = (x2 * cos + x1 * sin).astype(BF16)
        qb = qn.astype(BF16)
        for u, sl in enumerate(lanes):
            qT_ref[0, hd, u, 0:16, :] = r1[:, sl]
            qT_ref[0, hd, u, 16:64, :] = qb[16:64, sl]
            qT_ref[0, hd, u, 64:80, :] = r2[:, sl]
            qT_ref[0, hd, u, 80:128, :] = qb[80:128, sl]
    off = NSA_HEADS * NSA_HD
    ones = jnp.ones((ONES, n), BF16)
    for ref in (vsT_ref, vwT_ref):
        for g in range(G):
            ref[0, g, 0, 0:NSA_DV, :] = fa[off + g * NSA_DV:off + (g + 1) * NSA_DV].astype(BF16)
            ref[0, g, 0, NSA_DV:NSA_DV + ONES, :] = ones
        off += G * NSA_DV
    sg = _sigmoid(fa[off:off + 16 * G])
    for g in range(G):
        for u, sl in enumerate(lanes):
            g_ref[0, g, u] = sg[g * 16:(g + 1) * 16, sl]
    off += 16 * G
    z = fa[off:off + NSA_HEADS * NSA_DV]
    sz = z * _sigmoid(z)
    for u, sl in enumerate(lanes):
        sz_ref[0, u] = sz[:, sl]

    pt = _dot(h, wtok_ref[...])
    ct = ct_ref[...]
    st = st_ref[...]
    nblk = ks_ref.shape[-1] - NSA_HD
    row = lax.broadcasted_iota(jnp.int32, (n, nblk), 0) + t * n
    col = lax.broadcasted_iota(jnp.int32, (n, nblk), 1)
    onehot = jnp.where((row // SEL_LEN) == col, 1.0, 0.0).astype(BF16)
    for br, ref in ((0, ks_ref), (1, kw_ref)):
        gk = gk_ref[br + 1:br + 2, :]
        for g in range(G):
            kb = pt[:, (br * G + g) * NSA_HD:(br * G + g + 1) * NSA_HD]
            ss = jnp.sum(kb * kb, axis=-1, keepdims=True)
            kn = kb * lax.rsqrt(ss * (1.0 / NSA_DK) + RMS_EPS) * gk
            kr = kn * ct + pltpu.roll(kn, 64, axis=1) * st
            ref[0, g, :, 0:NSA_HD] = kr.astype(BF16)
            if br == 0:
                ref[0, g, :, NSA_HD:NSA_HD + nblk] = onehot
    for br, ref in ((2, kc_ref), (3, vc_ref)):
        for g in range(G):
            stage = cv_scr.at[(br - 2) * G + g]
            stage[...] = pt[:, (br * G + g) * LANE:(br * G + g + 1) * LANE]
            for l in range(CMP_STRIDE):
                ref[0, g, :, l * LANE:(l + 1) * LANE] = stage[pl.ds(l, n // CMP_STRIDE, stride=CMP_STRIDE), :]


def _nsa_in(x, ng, w, tabs):
    B, S, D = x.shape
    nt = S // TT
    G = NSA_GROUPS
    nblk = S // SEL_LEN
    r = TT // TQ_NSA
    nq = S // TQ_NSA
    assert TT == TKV and TT % TQ_NSA == 0
    full = lambda shape: pl.BlockSpec(shape, lambda b, t: (0,) * len(shape))
    tok = lambda b, t: (b * nt + t, 0)
    feat = lambda b, t: (0, b * nt + t)
    tokmaj = lambda width: pl.BlockSpec((1, G, TT, width), lambda b, t: (b, 0, t, 0))
    vtile = pl.BlockSpec((1, G, 1, NSA_DV + ONES, TT), lambda b, t: (b, 0, t, 0, 0))
    cmp_in = pl.BlockSpec((1, G, TT // CMP_STRIDE, CMP_STRIDE * LANE), lambda b, t: (b, 0, t, 0))
    vshape = jax.ShapeDtypeStruct((B, G, nt, NSA_DV + ONES, TT), BF16)
    return pl.pallas_call(
        _nsa_in_kernel,
        grid=(B, nt),
        in_specs=[
            pl.BlockSpec((1, TT, D), lambda b, t: (b, t, 0)),
            full((1, D)),
            full(w["wtok"].shape), full(w["wfT"].shape),
            full((NSA_HD, TT)), full((3, NSA_HD)),
            pl.BlockSpec((TT, LANE), tok), pl.BlockSpec((TT, LANE), tok),
            pl.BlockSpec((16, TT), feat), pl.BlockSpec((16, TT), feat),
        ],
        out_specs=[
            pl.BlockSpec((1, NSA_HEADS, r, NSA_HD, TQ_NSA), lambda b, t: (b, 0, t, 0, 0)),
            tokmaj(NSA_HD + nblk), tokmaj(NSA_HD), cmp_in, cmp_in,
            vtile, vtile,
            pl.BlockSpec((1, G, r, 16, TQ_NSA), lambda b, t: (b, 0, t, 0, 0)),
            pl.BlockSpec((1, r, NSA_HEADS * NSA_DV, TQ_NSA), lambda b, t: (b, t, 0, 0)),
        ],
        out_shape=[
            jax.ShapeDtypeStruct((B, NSA_HEADS, nq, NSA_HD, TQ_NSA), BF16),
            jax.ShapeDtypeStruct((B, G, S, NSA_HD + nblk), BF16),
            jax.ShapeDtypeStruct((B, G, S, NSA_HD), BF16),
            jax.ShapeDtypeStruct((B, G, S // CMP_STRIDE, CMP_STRIDE * LANE), F32),
            jax.ShapeDtypeStruct((B, G, S // CMP_STRIDE, CMP_STRIDE * LANE), F32),
            vshape, vshape,
            jax.ShapeDtypeStruct((B, G, nq, 16, TQ_NSA), F32),
            jax.ShapeDtypeStruct((B, nq, NSA_HEADS * NSA_DV, TQ_NSA), F32),
        ],
        scratch_shapes=[pltpu.VMEM((2 * G, TT, LANE), F32)],
        compiler_params=_params(("parallel", "parallel")),
        name="nsa_in_proj",
    )(x, ng, w["wtok"], w["wfT"], w["gq"], w["gk"], tabs["ct_nsa"], tabs["st_nsa"],
      tabs["cosT_nsa"], tabs["sinT_nsa"])


def _nsa_cmp_kernel(kc_ref, vc_ref, pek_ref, pev_ref, w1k_ref, w2k_ref, w1v_ref, w2vT_ref,
                    gk_ref, ct_ref, st_ref, kcmp_ref, vcmpT_ref):
    nrow = kc_ref.shape[2]

    def pre(x, pe_ref, w1_ref):
        lo = _dot((x + pe_ref[0:1, :]).astype(BF16), w1_ref[0])
        hi = _dot((x + pe_ref[1:2, :]).astype(BF16), w1_ref[1])
        return lo + pltpu.roll(hi, nrow - 1, axis=0)

    a = pre(kc_ref[0, 0], pek_ref, w1k_ref)
    kc = _dot((a * _sigmoid(a)).astype(BF16), w2k_ref[...])
    ss = jnp.sum(kc * kc, axis=-1, keepdims=True)
    kn = kc * lax.rsqrt(ss * (1.0 / NSA_DK) + RMS_EPS) * gk_ref[0:1, :]
    kr = kn * ct_ref[0] + pltpu.roll(kn, 64, axis=1) * st_ref[0]
    kcmp_ref[0, 0] = kr.astype(BF16)

    a = pre(vc_ref[0, 0], pev_ref, w1v_ref)
    sv = (a * _sigmoid(a)).astype(BF16)
    vcmpT_ref[0, 0, 0:NSA_DV, :] = _dot_nt(w2vT_ref[...], sv).astype(BF16)
    vcmpT_ref[0, 0, NSA_DV:NSA_DV + ONES, :] = jnp.ones((ONES, nrow), BF16)


def _nsa_cmp(kc16, vc16, w, tabs):
    B, G, nrow, width = kc16.shape
    full = lambda shape: pl.BlockSpec(shape, lambda b, g: (0,) * len(shape))
    blk = pl.BlockSpec((1, 1, nrow, width), lambda b, g: (b, g, 0, 0))
    tab = pl.BlockSpec((1, nrow, LANE), lambda b, g: (b, 0, 0))
    return pl.pallas_call(
        _nsa_cmp_kernel,
        grid=(B, G),
        in_specs=[blk, blk, full((2, width)), full((2, width)),
                  full(w["w1k"].shape), full(w["w2k"].shape),
                  full(w["w1v"].shape), full(w["w2vT"].shape),
                  full((3, NSA_HD)), tab, tab],
        out_specs=[pl.BlockSpec((1, 1, nrow, NSA_HD), lambda b, g: (b, g, 0, 0)),
                   pl.BlockSpec((1, 1, NSA_DV + ONES, nrow), lambda b, g: (b, g, 0, 0))],
        out_shape=[jax.ShapeDtypeStruct((B, G, nrow, NSA_HD), BF16),
                   jax.ShapeDtypeStruct((B, G, NSA_DV + ONES, nrow), BF16)],
        compiler_params=_params(("parallel", "parallel")),
        name="nsa_compress",
    )(kc16, vc16, w["pek"], w["pev"], w["w1k"], w["w2k"], w["w1v"], w["w2vT"], w["gk"],
      tabs["ct_cmp"], tabs["st_cmp"])


def _tile4(a):
    return jnp.concatenate([a] * NSA_HPG, axis=1)


def _cmp_exp(s_ref, bias):
    n, w = s_ref.shape
    fold = lambda a, op: op(a.reshape(a.shape[0] // SUB, SUB, w), axis=0)
    rows = [slice(lo, lo + CHUNK) for lo in range(0, n, CHUNK)]
    mx = None
    for sl in rows:
        blk = bias[sl.start](s_ref[sl, :])
        s_ref[sl, :] = blk
        mx = fold(blk, jnp.max) if mx is None else jnp.maximum(mx, fold(blk, jnp.max))
    mc = jnp.maximum(jnp.max(mx, axis=0, keepdims=True), 0.5 * NEG)
    return jnp.concatenate([jnp.exp2(s_ref[sl, :] - mc).astype(BF16) for sl in rows], axis=0)


def _nsa_pre_kernel(qT_ref, kw_ref, vwT_ref, kc_ref, vcT_ref, ovT_ref, g_ref,
                    csel_ref, ocw_ref, *scratch, k_sel):
    nq, _, tq = qT_ref.shape[2:]
    ncmp = kc_ref.shape[2]
    nblk = ovT_ref.shape[0]

    wk = WINDOW + tq
    rloc = lax.broadcasted_iota(jnp.int32, (CHUNK, tq), 0)
    cloc = lax.broadcasted_iota(jnp.int32, (CHUNK, tq), 1)
    mask = lambda ok, lo: (lambda blk: blk + _tile4(jnp.where(ok(lo), 0.0, NEG)))

    def one_tile(t, u, clamped, jlim):
        q0 = t * tq
        qT = jnp.concatenate([qT_ref[0, hh, t] for hh in range(NSA_HPG)], axis=1)
        tok = q0 + lax.broadcasted_iota(jnp.int32, (1, tq), 1)

        sc_scr, sw_scr = scratch[2 * u], scratch[2 * u + 1]
        sc_scr[...] = _dot(kc_ref[0, 0], qT)
        yield
        cvalid = lambda lo: ((lo + rloc) * CMP_STRIDE + (CMP_LEN - 1)) <= tok
        cbias = {lo: mask(cvalid, lo) for lo in range(0, ncmp, CHUNK)}
        ec = _cmp_exp(sc_scr, cbias)
        ac = _dot(vcT_ref[0, 0], ec)
        dc = ac[NSA_DV:NSA_DV + 1]
        inv = 1.0 / jnp.where(dc > 0, dc, 1.0)
        o_c = ac[0:NSA_DV] * inv

        if clamped:
            start, off = 0, 0
            wbias = {lo: mask(lambda lo: lo + rloc <= tok, lo) for lo in range(0, wk, CHUNK)}
        else:
            start, off = q0 - WINDOW, (u % (TKV // tq)) * tq
            wbias = {lo: mask(lambda lo: lo + rloc > cloc, lo) for lo in range(0, tq, CHUNK)}
            wbias.update({lo: mask(lambda lo: lo - WINDOW + rloc <= cloc, lo)
                          for lo in range(WINDOW, wk, CHUNK)})
        sw_scr[...] = _dot(kw_ref[0, 0, pl.ds(pl.multiple_of(start, tq), wk), :], qT)
        yield
        w0 = start // TKV
        segs, pos, tile = [], 0, 0
        while pos < wk:
            width = min(TKV - off, wk - pos)
            segs.append((pos, pos + width, functools.partial(
                lambda tile, off, width: vwT_ref[0, 0, w0 + tile, :, off:off + width], tile, off, width)))
            pos, tile, off = pos + width, tile + 1, 0
        init = (jnp.full((1, NSA_HPG * tq), M_INIT, F32),
                jnp.zeros((NSA_DV + ONES, NSA_HPG * tq), F32))
        _, aw = _softmax_step_ref(sw_scr, init, segs=segs, bias=wbias)
        o_w = aw[0:NSA_DV] * (1.0 / aw[NSA_DV:NSA_DV + 1])
        yield

        if jlim:
            iw = _dot(ovT_ref[0:jlim, :], ec) * inv
            imp = iw[:, 0:tq]
            for hh in range(1, NSA_HPG):
                imp = imp + iw[:, hh * tq:(hh + 1) * tq]
            bidx = lax.broadcasted_iota(jnp.int32, (jlim, tq), 0)
            cur = tok // SEL_LEN
            forced = (bidx == 0) | (bidx == cur) | (bidx == cur - 1)
            valid = (bidx * SEL_LEN) <= tok
            score = jnp.where(forced, SEL_FORCE, jnp.where(valid, imp, -1.0))
            grp = [score[r * SUB:(r + 1) * SUB] for r in range(jlim // SUB)]
            bsub = lax.broadcasted_iota(jnp.int32, (SUB, tq), 0)
            cnt = [jnp.zeros((SUB, tq), F32) for _ in grp]
            for j in range(jlim):
                rowj = score[j:j + 1, :]
                for r in range(jlim // SUB):
                    if r * SUB > j:
                        cnt[r] = cnt[r] + jnp.where(rowj >= grp[r], 1.0, 0.0)
                    elif r * SUB + SUB - 1 < j:
                        cnt[r] = cnt[r] + jnp.where(rowj > grp[r], 1.0, 0.0)
                    else:
                        tie = jnp.where(bsub + r * SUB > j, 1.0, 0.0)
                        cnt[r] = cnt[r] + jnp.where(rowj > grp[r], 1.0, 0.0)
                        cnt[r] = cnt[r] + jnp.where(rowj == grp[r], tie, 0.0)
            cnt = jnp.concatenate(cnt, axis=0)
            csel_ref[0, 0, t, 0:jlim, :] = jnp.where(cnt < k_sel, 0.0, NEG).astype(BF16)
        if jlim < nblk:
            csel_ref[0, 0, t, jlim:nblk, :] = jnp.zeros((nblk - jlim, tq), BF16)

        for hh in range(NSA_HPG):
            sl = slice(hh * tq, (hh + 1) * tq)
            g0 = g_ref[0, 0, t, hh:hh + 1, :]
            g2 = g_ref[0, 0, t, 2 * NSA_HPG + hh:2 * NSA_HPG + hh + 1, :]
            ocw_ref[0, 0, t, hh * NSA_DV:(hh + 1) * NSA_DV, :] = g0 * o_c[:, sl] + g2 * o_w[:, sl]

    def make_trip(clamped, jlim):
        def trip(i, _):
            live = [one_tile(i * PRE_TILES + u, u, clamped, jlim) for u in range(PRE_TILES)]
            while live:
                live = [g for g in live if next(g, "done") != "done"]
            return 0
        return trip

    step = tq * PRE_TILES
    n_trips = nq // PRE_TILES
    n_clamped = min(WINDOW // step, n_trips)
    n_norank = min(max((k_sel * SEL_LEN) // step, n_clamped), n_trips)
    cuts = sorted({0, n_clamped, n_norank, n_trips}
                  | {n_norank + (k * (n_trips - n_norank)) // RANK_CLASSES for k in range(RANK_CLASSES)})
    for lo, hi in zip(cuts[:-1], cuts[1:]):
        causal_blocks = -(-(hi * step) // SEL_LEN)
        jlim = 0 if hi <= n_norank else min(nblk, -(-causal_blocks // SUB) * SUB)
        lax.fori_loop(lo, hi, make_trip(lo < n_clamped, jlim), 0)


def _nsa_pre(qT, kw, vwT, kcmp, vcmpT, ovT, gates):
    B, H, nq, _, tq = qT.shape
    G = NSA_GROUPS
    S = kw.shape[2]
    nblk = S // SEL_LEN
    ncmp = kcmp.shape[2]
    assert TKV % tq == 0 and nq % PRE_TILES == 0 and PRE_TILES % (TKV // tq) == 0
    assert WINDOW % (tq * PRE_TILES) == 0 and WINDOW % TKV == 0 and S >= WINDOW + tq
    kern = functools.partial(_nsa_pre_kernel, k_sel=min(N_SELECT, nblk))
    per_bg = lambda shape: pl.BlockSpec((1, 1) + shape, lambda b, g: (b, g) + (0,) * len(shape))
    wide = NSA_HPG * tq
    return pl.pallas_call(
        kern,
        grid=(B, G),
        in_specs=[
            pl.BlockSpec((1, NSA_HPG, nq, NSA_HD, tq), lambda b, g: (b, g, 0, 0, 0)),
            per_bg((S, NSA_HD)),
            per_bg((S // TKV, NSA_DV + ONES, TKV)),
            per_bg((ncmp, NSA_HD)),
            per_bg((NSA_DV + ONES, ncmp)),
            pl.BlockSpec((nblk, ncmp), lambda b, g: (0, 0)),
            per_bg((nq, 16, tq)),
        ],
        out_specs=[per_bg((nq, nblk, tq)), per_bg((nq, NSA_HPG * NSA_DV, tq))],
        out_shape=[jax.ShapeDtypeStruct((B, G, nq, nblk, tq), BF16),
                   jax.ShapeDtypeStruct((B, G, nq, NSA_HPG * NSA_DV, tq), F32)],
        scratch_shapes=[pltpu.VMEM((ncmp, wide), F32), pltpu.VMEM((WINDOW + tq, wide), F32)] * PRE_TILES,
        compiler_params=_params(("parallel", "parallel")),
        name="nsa_branches",
    )(qT, kw, vwT, kcmp, vcmpT, ovT, gates)


def _nsa_sel_kernel(qT_ref, csel_ref, ks_ref, vsT_ref, ocw_ref, g_ref, sz_ref, o_ref,
                    qaug_ref, s_a, s_b):
    nq, _, tq = qT_ref.shape[2:]
    nblk = csel_ref.shape[3]
    nsub = TSTEP // TKV
    per = TQ_SEL // tq
    wide = NSA_HPG * TQ_SEL

    row = lax.broadcasted_iota(jnp.int32, (CHUNK, TQ_SEL), 0)
    col = lax.broadcasted_iota(jnp.int32, (CHUNK, TQ_SEL), 1)

    def qtile(sup, p):
        i = sup * (TSTEP // TQ_SEL) + p
        q0 = i * TQ_SEL
        for hh in range(NSA_HPG):
            for u in range(per):
                lanes = slice(hh * TQ_SEL + u * tq, hh * TQ_SEL + (u + 1) * tq)
                qaug_ref[0:NSA_HD, lanes] = qT_ref[0, hh, i * per + u]
                qaug_ref[NSA_HD:NSA_HD + nblk, lanes] = csel_ref[0, 0, i * per + u]
        qaug = qaug_ref[...]

        def scores(j):
            kt = ks_ref[0, 0, j * TSTEP:(j + 1) * TSTEP, :]
            return _dot(kt, qaug)

        def vtile(j):
            return lambda r: vsT_ref[0, 0, j * nsub + r]

        own = p * TQ_SEL
        causal = {own + lo: functools.partial(
            lambda lo, blk: blk + _tile4(jnp.where(lo + row <= col, 0.0, NEG)), lo)
            for lo in range(0, TQ_SEL, CHUNK)}
        init = (jnp.full((1, wide), M_INIT, F32), jnp.zeros((NSA_DV + ONES, wide), F32))
        _, acc = _sweep(scores, vtile, causal, own + TQ_SEL, sup, init, s_a, s_b)
        o_s = acc[0:NSA_DV] * (1.0 / acc[NSA_DV:NSA_DV + 1])

        for hh in range(NSA_HPG):
            rows = slice(hh * NSA_DV, (hh + 1) * NSA_DV)
            for u in range(per):
                t = i * per + u
                lanes = slice(hh * TQ_SEL + u * tq, hh * TQ_SEL + (u + 1) * tq)
                g1 = g_ref[0, 0, t, NSA_HPG + hh:NSA_HPG + hh + 1, :]
                o = ocw_ref[0, 0, t, rows, :] + g1 * o_s[:, lanes]
                o_ref[0, t, rows, :] = (o * sz_ref[0, t, rows, :]).astype(BF16)

    once = jnp.minimum(pl.program_id(0), 0) + 1
    for sup in range((nq * tq) // TSTEP):
        for p in range(TSTEP // TQ_SEL):
            lax.fori_loop(0, once, lambda _, c, sup=sup, p=p: (qtile(sup, p), c)[1], 0)


def _nsa_sel(qT, csel, ks, vsT, ocw, gates, sz):
    B, H, nq, _, tq = qT.shape
    G = NSA_GROUPS
    S = ks.shape[2]
    nblk = S // SEL_LEN
    assert TQ_SEL % tq == 0 and TSTEP % TQ_SEL == 0 and TSTEP % TKV == 0 and S % TSTEP == 0
    per_bg = lambda shape: pl.BlockSpec((1, 1) + shape, lambda b, g: (b, g) + (0,) * len(shape))
    gate = pl.BlockSpec((1, nq, NSA_HPG * NSA_DV, tq), lambda b, g: (b, 0, g, 0))
    return pl.pallas_call(
        _nsa_sel_kernel,
        grid=(B, G),
        in_specs=[
            pl.BlockSpec((1, NSA_HPG, nq, NSA_HD, tq), lambda b, g: (b, g, 0, 0, 0)),
            per_bg((nq, nblk, tq)),
            per_bg((S, NSA_HD + nblk)),
            per_bg((S // TKV, NSA_DV + ONES, TKV)),
            per_bg((nq, NSA_HPG * NSA_DV, tq)),
            per_bg((nq, 16, tq)),
            gate,
        ],
        out_specs=gate,
        out_shape=jax.ShapeDtypeStruct((B, nq, H * NSA_DV, tq), BF16),
        scratch_shapes=[pltpu.VMEM((NSA_HD + nblk, NSA_HPG * TQ_SEL), BF16)]
        + [pltpu.VMEM((TSTEP, NSA_HPG * TQ_SEL), F32)] * 2,
        compiler_params=_params(("parallel", "parallel")),
        name="nsa_selected",
    )(qT, csel, ks, vsT, ocw, gates, sz)


def _nsa_perm():
    src = np.full((NSA_HD,), -1, np.int64)
    src[0:12] = np.arange(0, 12)
    src[12:16] = np.arange(24, 28)
    src[16:64] = np.arange(28, 76)
    src[64:76] = np.arange(12, 24)
    src[76:80] = np.arange(76, 80)
    src[80:96] = np.arange(80, 96)
    return src


def _take_cols(w, src):
    idx = np.where(src >= 0, src, 0)
    out = jnp.take(w, jnp.asarray(idx), axis=-1)
    return jnp.where(jnp.asarray(src >= 0), out, 0.0)


def _pad_last(w, width):
    return jnp.pad(w, [(0, 0)] * (w.ndim - 1) + [(0, width - w.shape[-1])])


def _prep_mla(w_in, g_cq, w_uq, g_ckv, w_ukv, g_q, g_k, w_out):
    D = w_in.shape[0]
    o1 = MLA_Q_LORA + MLA_KV_LORA
    z32 = jnp.zeros((D, 32), F32)
    wa = jnp.concatenate([w_in[:, :o1], w_in[:, o1:o1 + 32], z32, w_in[:, o1 + 32:o1 + 64], z32], axis=1)
    wq = w_uq.reshape(MLA_Q_LORA, MLA_HEADS, MLA_QK)
    zq = jnp.zeros((MLA_Q_LORA, MLA_HEADS, 32), F32)
    wq = jnp.concatenate([wq[:, :, 64:], wq[:, :, :32], zq, wq[:, :, 32:64], zq], axis=2)
    z1 = jnp.zeros((32,), F32)
    gq = jnp.concatenate([g_q[64:], g_q[:32], z1, g_q[32:64], z1]) * C_MLA
    wkv = w_ukv.reshape(MLA_KV_LORA, MLA_HEADS, MLA_NOPE + MLA_V)
    return {
        "wa": wa.astype(BF16),
        "wzT": w_in[:, o1 + MLA_ROPE:].T.astype(BF16),
        "gcq": g_cq.reshape(1, -1), "gckv": g_ckv.reshape(1, -1),
        "wuqT": wq.reshape(MLA_Q_LORA, MLA_HEADS * MLA_HD).T.astype(BF16),
        "gq": jnp.broadcast_to(gq[:, None], (MLA_HD, TT)),
        "wuk": wkv[:, :, :MLA_NOPE].reshape(MLA_KV_LORA, -1).astype(BF16),
        "wuvT": wkv[:, :, MLA_NOPE:].reshape(MLA_KV_LORA, -1).T.astype(BF16),
        "gkn": g_k[64:].reshape(1, -1),
        "gkp": jnp.concatenate([g_k[:32], z1, g_k[32:64], z1]).reshape(1, -1),
        "woT": w_out.T.astype(BF16),
    }


def _prep_nsa(w_in, g_q, g_k, pe_k, w1_k, w2_k, pe_v, w1_v, w2_v, w_out):
    D = w_in.shape[0]
    G = NSA_GROUPS
    src = _nsa_perm()
    offs = np.concatenate([[0], np.cumsum(NSA_SIZES)])
    part = lambda i: w_in[:, offs[i]:offs[i + 1]]
    q, kc, vc, ks, vs, kw, vw, gl, z = [part(i) for i in range(9)]
    perm_heads = lambda w, nh: _take_cols(w.reshape(D, nh, NSA_DK), src).reshape(D, nh * NSA_HD)
    pad_groups = lambda w, d: _pad_last(w.reshape(D, G, d), LANE).reshape(D, G * LANE)
    wtok = jnp.concatenate([perm_heads(ks, G), perm_heads(kw, G), pad_groups(kc, NSA_DK),
                            pad_groups(vc, NSA_DV)], axis=1)
    glr = gl.reshape(D, G, NSA_HPG, 3).transpose(0, 1, 3, 2).reshape(D, G, 3 * NSA_HPG)
    glr = _pad_last(glr, 16).reshape(D, G * 16)
    wf = jnp.concatenate([perm_heads(q, NSA_HEADS), vs, vw, glr, z], axis=1)
    halves = lambda w, d: _pad_last(w.reshape(2, CMP_LEN // 2, d, -1).transpose(0, 1, 3, 2), LANE) \
        .transpose(0, 1, 3, 2).reshape(2, (CMP_LEN // 2) * LANE, -1)
    pe_flat = lambda pe: _pad_last(pe, LANE).reshape(2, (CMP_LEN // 2) * LANE)
    return {
        "wtok": wtok.astype(BF16),
        "wfT": wf.T.astype(BF16),
        "gq": jnp.broadcast_to((_take_cols(g_q, src) * C_NSA)[:, None], (NSA_HD, TT)),
        "gk": _take_cols(g_k, src),
        "pek": pe_flat(pe_k), "pev": pe_flat(pe_v),
        "w1k": _pad_last(halves(w1_k, NSA_DK), LANE).astype(BF16),
        "w2k": _pad_last(_take_cols(w2_k, src).T, LANE).T.astype(BF16),
        "w1v": _pad_last(halves(w1_v, NSA_DV), LANE).astype(BF16),
        "w2vT": _pad_last(w2_v.T, LANE).astype(BF16),
        "woT": w_out.T.astype(BF16),
    }


def _overlap_T(S):
    ncmp = S // CMP_STRIDE
    nblk = S // SEL_LEN
    n_cmp = (S - CMP_LEN) // CMP_STRIDE + 1
    cs = np.arange(ncmp) * CMP_STRIDE
    ss = np.arange(nblk) * SEL_LEN
    ov = (cs[None, :] < ss[:, None] + SEL_LEN) & (cs[None, :] + CMP_LEN > ss[:, None])
    ov = ov & (np.arange(ncmp)[None, :] < n_cmp)
    return jnp.asarray(ov.astype(np.float32)).astype(BF16)


def _mla_layer(x, ng, w, tabs):
    qT, k, vT, sz = _mla_in(x, ng.reshape(1, -1), w, tabs)
    og = _mla_attn(qT, k, vT, sz)
    return _out_proj(og, w["woT"], x)


def _nsa_layer(x, ng, w, tabs, ovT):
    B, S, _ = x.shape
    qT, ks, kw, kc16, vc16, vsT, vwT, gates, sz = _nsa_in(x, ng.reshape(1, -1), w, tabs)
    kcmp, vcmpT = _nsa_cmp(kc16, vc16, w, tabs)
    csel, ocw = _nsa_pre(qT, kw, vwT, kcmp, vcmpT, ovT, gates)
    og = _nsa_sel(qT, csel, ks, vsT, ocw, gates, sz)
    return _out_proj(og, w["woT"], x)


def kernel(x, positions, norm_g, mla_w_in, mla_g_cq, mla_w_uq, mla_g_ckv, mla_w_ukv, mla_g_q, mla_g_k, mla_w_out, nsa_w_in, nsa_g_q, nsa_g_k, nsa_pe_k, nsa_w1_k, nsa_w2_k, nsa_pe_v, nsa_w1_v, nsa_w2_v, nsa_w_out):
    B, S, _ = x.shape
    tabs = _rope_tables(positions)
    ovT = _overlap_T(S)
    w_mla = jax.vmap(_prep_mla)(mla_w_in, mla_g_cq, mla_w_uq, mla_g_ckv, mla_w_ukv, mla_g_q,
                                mla_g_k, mla_w_out)
    w_nsa = jax.vmap(_prep_nsa)(nsa_w_in, nsa_g_q, nsa_g_k, nsa_pe_k, nsa_w1_k, nsa_w2_k,
                                nsa_pe_v, nsa_w1_v, nsa_w2_v, nsa_w_out)
    for i in range(DEPTH):
        j = i // N_MIXERS
        if i % N_MIXERS == 0:
            x = _mla_layer(x, norm_g[i], {k: v[j] for k, v in w_mla.items()}, tabs)
        else:
            x = _nsa_layer(x, norm_g[i], {k: v[j] for k, v in w_nsa.items()}, tabs, ovT)
    return x
```

```python
import functools

import numpy as np
import jax
import jax.numpy as jnp
from jax import lax
from jax.experimental import pallas as pl
from jax.experimental.pallas import tpu as pltpu

F32 = jnp.float32
BF16 = jnp.bfloat16

D_MODEL = 1024
DEPTH = 4
N_MIXERS = 2
ROPE_THETA = 500000.0
RMS_EPS = 1e-6
MLA_HEADS = 16
MLA_NOPE = 128
MLA_ROPE = 64
MLA_V = 128
MLA_QK = MLA_NOPE + MLA_ROPE
MLA_Q_LORA = 384
MLA_KV_LORA = 256
MLA_HD = 256
NSA_HEADS = 16
NSA_GROUPS = 4
NSA_HPG = NSA_HEADS // NSA_GROUPS
NSA_DK = 96
NSA_DV = 64
NSA_ROT = NSA_DK // 4
NSA_HALF = NSA_ROT // 2
NSA_HD = 128
CMP_LEN = 32
CMP_STRIDE = 16
SEL_LEN = 64
N_SELECT = 16
WINDOW = 512
SEL_FORCE = 1e4
NSA_SIZES = (NSA_HEADS * NSA_DK,
             NSA_GROUPS * NSA_DK, NSA_GROUPS * NSA_DV,
             NSA_GROUPS * NSA_DK, NSA_GROUPS * NSA_DV,
             NSA_GROUPS * NSA_DK, NSA_GROUPS * NSA_DV,
             3 * NSA_HEADS, NSA_HEADS * NSA_DV)

LANE = 128
SUB = 8
TT = 256
TO = 512
TKV = 256
TSTEP = 512
TQ_MLA = 512
TQ_NSA = 128
TQ_SEL = 256
PRE_TILES = 4
RANK_CLASSES = 3
CHUNK = 64
ONES = 16
NEG = -30000.0
M_INIT = -1e30
LOG2E = 1.4426950408889634
C_MLA = (MLA_QK ** -0.5) * LOG2E
C_NSA = (NSA_DK ** -0.5) * LOG2E
VMEM_LIMIT = 56 * 1024 * 1024


def _dot(a, b):
    return jnp.dot(a, b, preferred_element_type=F32)


def _dot_nt(a, b):
    return lax.dot_general(a, b, (((1,), (1,)), ((), ())), preferred_element_type=F32)


def _sigmoid(x):
    return 1.0 / (1.0 + jnp.exp(-x))


def _rms_rows(x, g):
    ms = jnp.mean(x * x, axis=-1, keepdims=True)
    return x * lax.rsqrt(ms + RMS_EPS) * g


def _params(sem):
    return pltpu.CompilerParams(dimension_semantics=sem, vmem_limit_bytes=VMEM_LIMIT)


class _LayerWeights:
    def __init__(self, stack, layer):
        self.stack, self.layer = stack, layer

    def __getitem__(self, name):
        return self.stack[name]

    def spec(self, name):
        shape = self.stack[name].shape[1:]
        layer = self.layer
        return pl.BlockSpec((None,) + shape, lambda *_: (layer,) + (0,) * len(shape))


def _rope_kernel(pos_ref, invf_ref, cm_ref, sm_ref, cn_ref, sn_ref, ctm_ref, stm_ref, ctn_ref, stn_ref):
    ang = invf_ref[...] * pos_ref[...]
    c, s = jnp.cos(ang), jnp.sin(ang)
    hm = MLA_ROPE // 2
    cm, sm, cn, sn = c[:hm], s[:hm], c[hm:hm + 16], s[hm:hm + 16]
    cm_ref[...], sm_ref[...], cn_ref[...], sn_ref[...] = cm, sm, cn, sn
    tn = ang.shape[1]
    one = lambda r: jnp.ones((r, tn), F32)
    zero = lambda r: jnp.zeros((r, tn), F32)
    ctm_ref[...] = jnp.concatenate([cm, one(32), cm, one(32)], axis=0).T
    stm_ref[...] = jnp.concatenate([-sm, zero(32), sm, zero(32)], axis=0).T
    ctn_ref[...] = jnp.concatenate([cn, one(48), cn, one(48)], axis=0).T
    stn_ref[...] = jnp.concatenate([-sn, zero(48), sn, zero(48)], axis=0).T


def _rope_tables(positions):
    B, S = positions.shape
    n = positions.size
    tn = 512
    pos = positions.reshape(1, n).astype(F32)
    half_m = MLA_ROPE // 2
    inv_m = ROPE_THETA ** (-jnp.arange(half_m, dtype=F32) / half_m)
    inv_n = ROPE_THETA ** (-jnp.arange(NSA_HALF, dtype=F32) / NSA_HALF)
    invf = jnp.concatenate([inv_m, inv_n, jnp.zeros((16 - NSA_HALF,), F32)])
    rows = invf.shape[0]
    invf = jnp.broadcast_to(invf[:, None], (rows, tn))
    feat = lambda r: pl.BlockSpec((r, tn), lambda i: (0, i))
    tok = pl.BlockSpec((tn, LANE), lambda i: (i, 0))
    cm, sm, cn, sn, ctm, stm, ctn, stn = pl.pallas_call(
        _rope_kernel,
        grid=(n // tn,),
        in_specs=[pl.BlockSpec((1, tn), lambda i: (0, i)),
                  pl.BlockSpec((rows, tn), lambda i: (0, 0))],
        out_specs=[feat(half_m), feat(half_m), feat(16), feat(16), tok, tok, tok, tok],
        out_shape=[jax.ShapeDtypeStruct((half_m, n), F32)] * 2
        + [jax.ShapeDtypeStruct((16, n), F32)] * 2 + [jax.ShapeDtypeStruct((n, LANE), F32)] * 4,
        compiler_params=_params(("parallel",)),
        name="rope_tables",
    )(pos, invf)
    ncmp = S // CMP_STRIDE
    last = CMP_LEN - 1
    pick = lambda t: jnp.pad(t.reshape(B, S, LANE)[:, last::CMP_STRIDE],
                             ((0, 0), (0, ncmp - (S - last + CMP_STRIDE - 1) // CMP_STRIDE), (0, 0)))
    return {"cosT_mla": cm, "sinT_mla": sm, "cosT_nsa": cn, "sinT_nsa": sn,
            "ct_mla": ctm, "st_mla": stm, "ct_nsa": ctn, "st_nsa": stn,
            "ct_cmp": pick(ctn), "st_cmp": pick(stn)}


def _mla_in_kernel(x_ref, ng_ref, wa_ref, wzT_ref, gcq_ref, gckv_ref, wuqT_ref, gq_ref,
                   wuk_ref, wuvT_ref, gkn_ref, gkp_ref, ct_ref, st_ref, cosT_ref, sinT_ref,
                   qT_ref, k_ref, vT_ref, sz_ref):
    x = x_ref[0]
    n = x.shape[0]
    h = _rms_rows(x, ng_ref[...]).astype(BF16)
    pa = _dot(h, wa_ref[...])
    z = _dot_nt(wzT_ref[...], h)
    sz_ref[0, 0] = z * _sigmoid(z)

    cqn = _rms_rows(pa[:, :MLA_Q_LORA], gcq_ref[...]).astype(BF16)
    ckvn = _rms_rows(pa[:, MLA_Q_LORA:MLA_Q_LORA + MLA_KV_LORA], gckv_ref[...]).astype(BF16)
    kpe = pa[:, MLA_Q_LORA + MLA_KV_LORA:]

    qa = _dot_nt(wuqT_ref[...], cqn)
    cos = cosT_ref[...]
    sin = sinT_ref[...]
    gq = gq_ref[...]
    zeros32 = jnp.zeros((32, n), BF16)
    for hd in range(MLA_HEADS):
        blk = qa[hd * MLA_HD:(hd + 1) * MLA_HD]
        ss = jnp.sum(blk * blk, axis=0, keepdims=True)
        qn = blk * lax.rsqrt(ss * (1.0 / MLA_QK) + RMS_EPS) * gq
        x1 = qn[128:160]
        x2 = qn[192:224]
        qT_ref[0, hd, 0, 0:128, :] = qn[0:128].astype(BF16)
        qT_ref[0, hd, 0, 128:160, :] = (x1 * cos - x2 * sin).astype(BF16)
        qT_ref[0, hd, 0, 160:192, :] = zeros32
        qT_ref[0, hd, 0, 192:224, :] = (x2 * cos + x1 * sin).astype(BF16)
        qT_ref[0, hd, 0, 224:256, :] = zeros32

    kn = _dot(ckvn, wuk_ref[...])
    ss_pe = jnp.sum(kpe * kpe, axis=-1, keepdims=True)
    kpg = kpe * gkp_ref[...]
    prot = kpg * ct_ref[...] + pltpu.roll(kpg, 64, axis=1) * st_ref[...]
    gkn = gkn_ref[...]
    for hd in range(MLA_HEADS):
        kb = kn[:, hd * MLA_NOPE:(hd + 1) * MLA_NOPE]
        ss = jnp.sum(kb * kb, axis=-1, keepdims=True) + ss_pe
        r = lax.rsqrt(ss * (1.0 / MLA_QK) + RMS_EPS)
        k_ref[0, hd, :, 0:128] = (kb * r * gkn).astype(BF16)
        k_ref[0, hd, :, 128:256] = (prot * r).astype(BF16)

    va = _dot_nt(wuvT_ref[...], ckvn)
    ones = jnp.ones((ONES, n), BF16)
    for hd in range(MLA_HEADS):
        vT_ref[0, hd, 0, 0:MLA_V, :] = va[hd * MLA_V:(hd + 1) * MLA_V].astype(BF16)
        vT_ref[0, hd, 0, MLA_V:MLA_V + ONES, :] = ones


def _mla_in(x, ng, w, tabs):
    B, S, D = x.shape
    nt = S // TT
    r = TQ_MLA // TT
    assert TT == TKV and TQ_MLA % TT == 0
    full = lambda shape: pl.BlockSpec(shape, lambda b, t: (0,) * len(shape))
    tok = lambda b, t: (b * nt + t, 0)
    feat = lambda b, t: (0, b * nt + t)
    return pl.pallas_call(
        _mla_in_kernel,
        grid=(B, nt),
        in_specs=[
            pl.BlockSpec((1, TT, D), lambda b, t: (b, t, 0)),
            full((1, D)),
            w.spec("wa"), w.spec("wzT"), w.spec("gcq"), w.spec("gckv"),
            w.spec("wuqT"), w.spec("gq"), w.spec("wuk"), w.spec("wuvT"),
            w.spec("gkn"), w.spec("gkp"),
            pl.BlockSpec((TT, LANE), tok), pl.BlockSpec((TT, LANE), tok),
            pl.BlockSpec((32, TT), feat), pl.BlockSpec((32, TT), feat),
        ],
        out_specs=[
            pl.BlockSpec((1, MLA_HEADS, 1, MLA_HD, TT), lambda b, t: (b, 0, t // r, 0, t % r)),
            pl.BlockSpec((1, MLA_HEADS, TT, MLA_HD), lambda b, t: (b, 0, t, 0)),
            pl.BlockSpec((1, MLA_HEADS, 1, MLA_V + ONES, TT), lambda b, t: (b, 0, t, 0, 0)),
            pl.BlockSpec((1, 1, MLA_HEADS * MLA_V, TT), lambda b, t: (b, t // r, 0, t % r)),
        ],
        out_shape=[
            jax.ShapeDtypeStruct((B, MLA_HEADS, S // TQ_MLA, MLA_HD, TQ_MLA), BF16),
            jax.ShapeDtypeStruct((B, MLA_HEADS, S, MLA_HD), BF16),
            jax.ShapeDtypeStruct((B, MLA_HEADS, nt, MLA_V + ONES, TT), BF16),
            jax.ShapeDtypeStruct((B, S // TQ_MLA, MLA_HEADS * MLA_V, TQ_MLA), F32),
        ],
        compiler_params=_params(("parallel", "parallel")),
        name="mla_in_proj",
    )(x, ng, w["wa"], w["wzT"], w["gcq"], w["gckv"], w["wuqT"], w["gq"], w["wuk"],
      w["wuvT"], w["gkn"], w["gkp"], tabs["ct_mla"], tabs["st_mla"], tabs["cosT_mla"],
      tabs["sinT_mla"])


def _softmax_step_ref(s_ref, carry, vtile=None, segs=None, bias=None):
    m, acc = carry
    n, w = s_ref.shape
    segs = segs or [(t * TKV, (t + 1) * TKV, functools.partial(vtile, t)) for t in range(n // TKV)]
    bias = bias or {}
    fold = lambda a, op: op(a.reshape(a.shape[0] // SUB, SUB, w), axis=0)
    mx = None
    for lo in range(0, n, CHUNK):
        blk = s_ref[lo:lo + CHUNK, :]
        if lo in bias:
            blk = bias[lo](blk) if callable(bias[lo]) else blk + bias[lo]
            s_ref[lo:lo + CHUNK, :] = blk
        mx = fold(blk, jnp.max) if mx is None else jnp.maximum(mx, fold(blk, jnp.max))
    m_new = jnp.maximum(m, jnp.max(mx, axis=0, keepdims=True))
    alpha = jnp.exp2(m - m_new)
    pv = None
    for seg_lo, seg_hi, value_tile in segs:
        parts = [jnp.exp2(s_ref[lo:lo + CHUNK, :] - m_new).astype(BF16)
                 for lo in range(seg_lo, seg_hi, CHUNK)]
        d = _dot(value_tile(), jnp.concatenate(parts, axis=0))
        pv = d if pv is None else pv + d
    return m_new, alpha * acc + pv


def _sweep(scores, vtile, last_bias, last_rows, n_full, carry, s_a, s_b):
    def step(src, dst, j, carry):
        dst[...] = scores(j + 1)
        return _softmax_step_ref(src, carry, vtile(j))

    def last(src, j, carry):
        return _softmax_step_ref(src.at[0:last_rows], carry, vtile(j), bias=last_bias)

    def pair(t, carry):
        carry = step(s_a, s_b, 2 * t, carry)
        return step(s_b, s_a, 2 * t + 1, carry)

    s_a[...] = scores(0)
    if isinstance(n_full, int):
        bufs = (s_a, s_b)
        for j in range(n_full):
            carry = step(bufs[j % 2], bufs[(j + 1) % 2], j, carry)
        return last(bufs[n_full % 2], n_full, carry)
    carry = lax.fori_loop(0, n_full // 2, pair, carry)
    odd = lambda carry: last(s_b, n_full, step(s_a, s_b, n_full - 1, carry))
    even = lambda carry: last(s_a, n_full, carry)
    return lax.cond(n_full % 2 == 1, odd, even, carry)


def _mla_attn_kernel(qT_ref, k_ref, vT_ref, sz_ref, o_ref, s_a, s_b):
    nq, _, tq = qT_ref.shape[2:]
    nsub = TSTEP // TKV

    row = lax.broadcasted_iota(jnp.int32, (CHUNK, tq), 0)
    col = lax.broadcasted_iota(jnp.int32, (CHUNK, tq), 1)
    causal = {lo: functools.partial(lambda lo, blk: jnp.where(lo + row <= col, blk, NEG), lo)
              for lo in range(0, TSTEP, CHUNK)}

    def qtile(i, _):
        q = qT_ref[0, 0, i]

        def scores(j):
            kt = k_ref[0, 0, j * TSTEP:(j + 1) * TSTEP, :]
            return _dot(kt, q)

        def vtile(j):
            return lambda r: vT_ref[0, 0, j * nsub + r]

        init = (jnp.full((1, tq), M_INIT, F32), jnp.zeros((MLA_V + ONES, tq), F32))
        _, acc = _sweep(scores, vtile, causal, TSTEP, i, init, s_a, s_b)
        o = acc[0:MLA_V] / acc[MLA_V:MLA_V + 1]
        o_ref[0, i] = (o * sz_ref[0, i]).astype(BF16)
        return 0

    for i in range(nq):
        qtile(i, 0)


def _mla_attn(qT, k, vT, sz):
    B, H, nq, _, tq = qT.shape
    S = k.shape[2]
    assert tq == TSTEP and TSTEP % TKV == 0
    per_bh = lambda shape: pl.BlockSpec((1, 1) + shape, lambda b, h: (b, h) + (0,) * len(shape))
    gate = pl.BlockSpec((1, nq, MLA_V, tq), lambda b, h: (b, 0, h, 0))
    return pl.pallas_call(
        _mla_attn_kernel,
        grid=(B, H),
        in_specs=[per_bh((nq, MLA_HD, tq)), per_bh((S, MLA_HD)),
                  per_bh((S // TKV, MLA_V + ONES, TKV)), gate],
        out_specs=gate,
        out_shape=jax.ShapeDtypeStruct((B, nq, H * MLA_V, tq), BF16),
        scratch_shapes=[pltpu.VMEM((TSTEP, tq), F32)] * 2,
        compiler_params=_params(("parallel", "parallel")),
        name="mla_attention",
    )(qT, k, vT, sz)


def _out_proj_kernel(og_ref, wT_ref, x_ref, o_ref):
    ntile = og_ref.shape[1]
    og = og_ref[0, 0] if ntile == 1 else jnp.concatenate([og_ref[0, r] for r in range(ntile)], axis=1)
    yT = _dot(wT_ref[...], og)
    o_ref[0] = x_ref[0] + yT.T


def _out_proj(og, w, x):
    B, S, D = x.shape
    _, _, K, W = og.shape
    r = TO // W
    wT = w["woT"]
    return pl.pallas_call(
        _out_proj_kernel,
        grid=(B, S // TO),
        in_specs=[
            pl.BlockSpec((1, r, K, W), lambda b, t: (b, t, 0, 0)),
            w.spec("woT"),
            pl.BlockSpec((1, TO, D), lambda b, t: (b, t, 0)),
        ],
        out_specs=pl.BlockSpec((1, TO, D), lambda b, t: (b, t, 0)),
        out_shape=jax.ShapeDtypeStruct((B, S, D), F32),
        compiler_params=_params(("parallel", "parallel")),
        name="out_proj",
    )(og, wT, x)


def _nsa_in_kernel(x_ref, ng_ref, wtok_ref, wfT_ref, gq_ref, gk_ref, ct_ref, st_ref,
                   cosT_ref, sinT_ref,
                   qT_ref, ks_ref, kw_ref, kc_ref, vc_ref, vsT_ref, vwT_ref, g_ref, sz_ref, cv_scr):
    t = pl.program_id(1)
    x = x_ref[0]
    n = x.shape[0]
    tq = qT_ref.shape[-1]
    h = _rms_rows(x, ng_ref[...]).astype(BF16)
    G = NSA_GROUPS
    lanes = [slice(u * tq, (u + 1) * tq) for u in range(n // tq)]

    fa = _dot_nt(wfT_ref[...], h)
    cos = cosT_ref[...]
    sin = sinT_ref[...]
    gq = gq_ref[...]
    for hd in range(NSA_HEADS):
        blk = fa[hd * NSA_HD:(hd + 1) * NSA_HD]
        ss = jnp.sum(blk * blk, axis=0, keepdims=True)
        qn = blk * lax.rsqrt(ss * (1.0 / NSA_DK) + RMS_EPS) * gq
        x1 = qn[0:16]
        x2 = qn[64:80]
        r1 = (x1 * cos - x2 * sin).astype(BF16)
        r2 = (x2 * cos + x1 * sin).astype(BF16)
        qb = qn.astype(BF16)
        for u, sl in enumerate(lanes):
            qT_ref[0, hd, u, 0:16, :] = r1[:, sl]
            qT_ref[0, hd, u, 16:64, :] = qb[16:64, sl]
            qT_ref[0, hd, u, 64:80, :] = r2[:, sl]
            qT_ref[0, hd, u, 80:128, :] = qb[80:128, sl]
    off = NSA_HEADS * NSA_HD
    ones = jnp.ones((ONES, n), BF16)
    for ref in (vsT_ref, vwT_ref):
        for g in range(G):
            ref[0, g, 0, 0:NSA_DV, :] = fa[off + g * NSA_DV:off + (g + 1) * NSA_DV].astype(BF16)
            ref[0, g, 0, NSA_DV:NSA_DV + ONES, :] = ones
        off += G * NSA_DV
    sg = _sigmoid(fa[off:off + 16 * G])
    for g in range(G):
        for u, sl in enumerate(lanes):
            g_ref[0, g, u] = sg[g * 16:(g + 1) * 16, sl]
    off += 16 * G
    z = fa[off:off + NSA_HEADS * NSA_DV]
    sz = z * _sigmoid(z)
    for u, sl in enumerate(lanes):
        sz_ref[0, u] = sz[:, sl]

    pt = _dot(h, wtok_ref[...])
    ct = ct_ref[...]
    st = st_ref[...]
    nblk = ks_ref.shape[-1] - NSA_HD
    row = lax.broadcasted_iota(jnp.int32, (n, nblk), 0) + t * n
    col = lax.broadcasted_iota(jnp.int32, (n, nblk), 1)
    onehot = jnp.where((row // SEL_LEN) == col, 1.0, 0.0).astype(BF16)
    for br, ref in ((0, ks_ref), (1, kw_ref)):
        gk = gk_ref[br + 1:br + 2, :]
        for g in range(G):
            kb = pt[:, (br * G + g) * NSA_HD:(br * G + g + 1) * NSA_HD]
            ss = jnp.sum(kb * kb, axis=-1, keepdims=True)
            kn = kb * lax.rsqrt(ss * (1.0 / NSA_DK) + RMS_EPS) * gk
            kr = kn * ct + pltpu.roll(kn, 64, axis=1) * st
            ref[0, g, :, 0:NSA_HD] = kr.astype(BF16)
            if br == 0:
                ref[0, g, :, NSA_HD:NSA_HD + nblk] = onehot
    for br, ref in ((2, kc_ref), (3, vc_ref)):
        for g in range(G):
            stage = cv_scr.at[(br - 2) * G + g]
            stage[...] = pt[:, (br * G + g) * LANE:(br * G + g + 1) * LANE]
            for l in range(CMP_STRIDE):
                ref[0, g, :, l * LANE:(l + 1) * LANE] = stage[pl.ds(l, n // CMP_STRIDE, stride=CMP_STRIDE), :]


def _nsa_in(x, ng, w, tabs):
    B, S, D = x.shape
    nt = S // TT
    G = NSA_GROUPS
    nblk = S // SEL_LEN
    r = TT // TQ_NSA
    nq = S // TQ_NSA
    assert TT == TKV and TT % TQ_NSA == 0
    full = lambda shape: pl.BlockSpec(shape, lambda b, t: (0,) * len(shape))
    tok = lambda b, t: (b * nt + t, 0)
    feat = lambda b, t: (0, b * nt + t)
    tokmaj = lambda width: pl.BlockSpec((1, G, TT, width), lambda b, t: (b, 0, t, 0))
    vtile = pl.BlockSpec((1, G, 1, NSA_DV + ONES, TT), lambda b, t: (b, 0, t, 0, 0))
    cmp_in = pl.BlockSpec((1, G, TT // CMP_STRIDE, CMP_STRIDE * LANE), lambda b, t: (b, 0, t, 0))
    vshape = jax.ShapeDtypeStruct((B, G, nt, NSA_DV + ONES, TT), BF16)
    return pl.pallas_call(
        _nsa_in_kernel,
        grid=(B, nt),
        in_specs=[
            pl.BlockSpec((1, TT, D), lambda b, t: (b, t, 0)),
            full((1, D)),
            w.spec("wtok"), w.spec("wfT"), w.spec("gq"), w.spec("gk"),
            pl.BlockSpec((TT, LANE), tok), pl.BlockSpec((TT, LANE), tok),
            pl.BlockSpec((16, TT), feat), pl.BlockSpec((16, TT), feat),
        ],
        out_specs=[
            pl.BlockSpec((1, NSA_HEADS, r, NSA_HD, TQ_NSA), lambda b, t: (b, 0, t, 0, 0)),
            tokmaj(NSA_HD + nblk), tokmaj(NSA_HD), cmp_in, cmp_in,
            vtile, vtile,
            pl.BlockSpec((1, G, r, 16, TQ_NSA), lambda b, t: (b, 0, t, 0, 0)),
            pl.BlockSpec((1, r, NSA_HEADS * NSA_DV, TQ_NSA), lambda b, t: (b, t, 0, 0)),
        ],
        out_shape=[
            jax.ShapeDtypeStruct((B, NSA_HEADS, nq, NSA_HD, TQ_NSA), BF16),
            jax.ShapeDtypeStruct((B, G, S, NSA_HD + nblk), BF16),
            jax.ShapeDtypeStruct((B, G, S, NSA_HD), BF16),
            jax.ShapeDtypeStruct((B, G, S // CMP_STRIDE, CMP_STRIDE * LANE), F32),
            jax.ShapeDtypeStruct((B, G, S // CMP_STRIDE, CMP_STRIDE * LANE), F32),
            vshape, vshape,
            jax.ShapeDtypeStruct((B, G, nq, 16, TQ_NSA), F32),
            jax.ShapeDtypeStruct((B, nq, NSA_HEADS * NSA_DV, TQ_NSA), F32),
        ],
        scratch_shapes=[pltpu.VMEM((2 * G, TT, LANE), F32)],
        compiler_params=_params(("parallel", "parallel")),
        name="nsa_in_proj",
    )(x, ng, w["wtok"], w["wfT"], w["gq"], w["gk"], tabs["ct_nsa"], tabs["st_nsa"],
      tabs["cosT_nsa"], tabs["sinT_nsa"])


def _nsa_cmp_kernel(kc_ref, vc_ref, pek_ref, pev_ref, w1k_ref, w2k_ref, w1v_ref, w2vT_ref,
                    gk_ref, ct_ref, st_ref, kcmp_ref, vcmpT_ref):
    nrow = kc_ref.shape[2]

    def pre(x, pe_ref, w1_ref):
        lo = _dot((x + pe_ref[0:1, :]).astype(BF16), w1_ref[0])
        hi = _dot((x + pe_ref[1:2, :]).astype(BF16), w1_ref[1])
        return lo + pltpu.roll(hi, nrow - 1, axis=0)

    a = pre(kc_ref[0, 0], pek_ref, w1k_ref)
    kc = _dot((a * _sigmoid(a)).astype(BF16), w2k_ref[...])
    ss = jnp.sum(kc * kc, axis=-1, keepdims=True)
    kn = kc * lax.rsqrt(ss * (1.0 / NSA_DK) + RMS_EPS) * gk_ref[0:1, :]
    kr = kn * ct_ref[0] + pltpu.roll(kn, 64, axis=1) * st_ref[0]
    kcmp_ref[0, 0] = kr.astype(BF16)

    a = pre(vc_ref[0, 0], pev_ref, w1v_ref)
    sv = (a * _sigmoid(a)).astype(BF16)
    vcmpT_ref[0, 0, 0:NSA_DV, :] = _dot_nt(w2vT_ref[...], sv).astype(BF16)
    vcmpT_ref[0, 0, NSA_DV:NSA_DV + ONES, :] = jnp.ones((ONES, nrow), BF16)


def _nsa_cmp(kc16, vc16, w, tabs):
    B, G, nrow, width = kc16.shape
    blk = pl.BlockSpec((1, 1, nrow, width), lambda b, g: (b, g, 0, 0))
    tab = pl.BlockSpec((1, nrow, LANE), lambda b, g: (b, 0, 0))
    return pl.pallas_call(
        _nsa_cmp_kernel,
        grid=(B, G),
        in_specs=[blk, blk, w.spec("pek"), w.spec("pev"), w.spec("w1k"), w.spec("w2k"),
                  w.spec("w1v"), w.spec("w2vT"), w.spec("gk"), tab, tab],
        out_specs=[pl.BlockSpec((1, 1, nrow, NSA_HD), lambda b, g: (b, g, 0, 0)),
                   pl.BlockSpec((1, 1, NSA_DV + ONES, nrow), lambda b, g: (b, g, 0, 0))],
        out_shape=[jax.ShapeDtypeStruct((B, G, nrow, NSA_HD), BF16),
                   jax.ShapeDtypeStruct((B, G, NSA_DV + ONES, nrow), BF16)],
        compiler_params=_params(("parallel", "parallel")),
        name="nsa_compress",
    )(kc16, vc16, w["pek"], w["pev"], w["w1k"], w["w2k"], w["w1v"], w["w2vT"], w["gk"],
      tabs["ct_cmp"], tabs["st_cmp"])


def _tile4(a):
    return jnp.concatenate([a] * NSA_HPG, axis=1)


def _cmp_exp(s_ref, bias):
    n, w = s_ref.shape
    fold = lambda a, op: op(a.reshape(a.shape[0] // SUB, SUB, w), axis=0)
    rows = [slice(lo, lo + CHUNK) for lo in range(0, n, CHUNK)]
    mx = None
    for sl in rows:
        blk = bias[sl.start](s_ref[sl, :])
        s_ref[sl, :] = blk
        mx = fold(blk, jnp.max) if mx is None else jnp.maximum(mx, fold(blk, jnp.max))
    mc = jnp.maximum(jnp.max(mx, axis=0, keepdims=True), 0.5 * NEG)
    return jnp.concatenate([jnp.exp2(s_ref[sl, :] - mc).astype(BF16) for sl in rows], axis=0)


def _nsa_pre_kernel(qT_ref, kw_ref, vwT_ref, kc_ref, vcT_ref, ovT_ref, g_ref,
                    csel_ref, ocw_ref, *scratch, k_sel):
    nq, _, tq = qT_ref.shape[2:]
    ncmp = kc_ref.shape[2]
    nblk = ovT_ref.shape[0]

    wk = WINDOW + tq
    rloc = lax.broadcasted_iota(jnp.int32, (CHUNK, tq), 0)
    cloc = lax.broadcasted_iota(jnp.int32, (CHUNK, tq), 1)
    mask = lambda ok, lo: (lambda blk: blk + _tile4(jnp.where(ok(lo), 0.0, NEG)))

    def one_tile(t, u, clamped, jlim):
        q0 = t * tq
        qT = jnp.concatenate([qT_ref[0, hh, t] for hh in range(NSA_HPG)], axis=1)
        tok = q0 + lax.broadcasted_iota(jnp.int32, (1, tq), 1)

        sc_scr, sw_scr = scratch[2 * u], scratch[2 * u + 1]
        sc_scr[...] = _dot(kc_ref[0, 0], qT)
        yield
        cvalid = lambda lo: ((lo + rloc) * CMP_STRIDE + (CMP_LEN - 1)) <= tok
        cbias = {lo: mask(cvalid, lo) for lo in range(0, ncmp, CHUNK)}
        ec = _cmp_exp(sc_scr, cbias)
        ac = _dot(vcT_ref[0, 0], ec)
        dc = ac[NSA_DV:NSA_DV + 1]
        inv = 1.0 / jnp.where(dc > 0, dc, 1.0)
        o_c = ac[0:NSA_DV] * inv

        if clamped:
            start, off = 0, 0
            wbias = {lo: mask(lambda lo: lo + rloc <= tok, lo) for lo in range(0, wk, CHUNK)}
        else:
            start, off = q0 - WINDOW, (u % (TKV // tq)) * tq
            wbias = {lo: mask(lambda lo: lo + rloc > cloc, lo) for lo in range(0, tq, CHUNK)}
            wbias.update({lo: mask(lambda lo: lo - WINDOW + rloc <= cloc, lo)
                          for lo in range(WINDOW, wk, CHUNK)})
        sw_scr[...] = _dot(kw_ref[0, 0, pl.ds(pl.multiple_of(start, tq), wk), :], qT)
        yield
        w0 = start // TKV
        segs, pos, tile = [], 0, 0
        while pos < wk:
            width = min(TKV - off, wk - pos)
            segs.append((pos, pos + width, functools.partial(
                lambda tile, off, width: vwT_ref[0, 0, w0 + tile, :, off:off + width], tile, off, width)))
            pos, tile, off = pos + width, tile + 1, 0
        init = (jnp.full((1, NSA_HPG * tq), M_INIT, F32),
                jnp.zeros((NSA_DV + ONES, NSA_HPG * tq), F32))
        _, aw = _softmax_step_ref(sw_scr, init, segs=segs, bias=wbias)
        o_w = aw[0:NSA_DV] * (1.0 / aw[NSA_DV:NSA_DV + 1])
        yield

        if jlim:
            iw = _dot(ovT_ref[0:jlim, :], ec) * inv
            imp = iw[:, 0:tq]
            for hh in range(1, NSA_HPG):
                imp = imp + iw[:, hh * tq:(hh + 1) * tq]
            bidx = lax.broadcasted_iota(jnp.int32, (jlim, tq), 0)
            cur = tok // SEL_LEN
            forced = (bidx == 0) | (bidx == cur) | (bidx == cur - 1)
            valid = (bidx * SEL_LEN) <= tok
            score = jnp.where(forced, SEL_FORCE, jnp.where(valid, imp, -1.0))
            grp = [score[r * SUB:(r + 1) * SUB] for r in range(jlim // SUB)]
            bsub = lax.broadcasted_iota(jnp.int32, (SUB, tq), 0)
            cnt = [jnp.zeros((SUB, tq), F32) for _ in grp]
            for j in range(jlim):
                rowj = score[j:j + 1, :]
                for r in range(jlim // SUB):
                    if r * SUB > j:
                        cnt[r] = cnt[r] + jnp.where(rowj >= grp[r], 1.0, 0.0)
                    elif r * SUB + SUB - 1 < j:
                        cnt[r] = cnt[r] + jnp.where(rowj > grp[r], 1.0, 0.0)
                    else:
                        tie = jnp.where(bsub + r * SUB > j, 1.0, 0.0)
                        cnt[r] = cnt[r] + jnp.where(rowj > grp[r], 1.0, 0.0)
                        cnt[r] = cnt[r] + jnp.where(rowj == grp[r], tie, 0.0)
            cnt = jnp.concatenate(cnt, axis=0)
            csel_ref[0, 0, t, 0:jlim, :] = jnp.where(cnt < k_sel, 0.0, NEG).astype(BF16)
        if jlim < nblk:
            csel_ref[0, 0, t, jlim:nblk, :] = jnp.zeros((nblk - jlim, tq), BF16)

        for hh in range(NSA_HPG):
            sl = slice(hh * tq, (hh + 1) * tq)
            g0 = g_ref[0, 0, t, hh:hh + 1, :]
            g2 = g_ref[0, 0, t, 2 * NSA_HPG + hh:2 * NSA_HPG + hh + 1, :]
            ocw_ref[0, 0, t, hh * NSA_DV:(hh + 1) * NSA_DV, :] = g0 * o_c[:, sl] + g2 * o_w[:, sl]

    def make_trip(clamped, jlim):
        def trip(i, _):
            live = [one_tile(i * PRE_TILES + u, u, clamped, jlim) for u in range(PRE_TILES)]
            while live:
                live = [g for g in live if next(g, "done") != "done"]
            return 0
        return trip

    step = tq * PRE_TILES
    n_trips = nq // PRE_TILES
    n_clamped = min(WINDOW // step, n_trips)
    n_norank = min(max((k_sel * SEL_LEN) // step, n_clamped), n_trips)
    cuts = sorted({0, n_clamped, n_norank, n_trips}
                  | {n_norank + (k * (n_trips - n_norank)) // RANK_CLASSES for k in range(RANK_CLASSES)})
    for lo, hi in zip(cuts[:-1], cuts[1:]):
        causal_blocks = -(-(hi * step) // SEL_LEN)
        jlim = 0 if hi <= n_norank else min(nblk, -(-causal_blocks // SUB) * SUB)
        lax.fori_loop(lo, hi, make_trip(lo < n_clamped, jlim), 0)


def _nsa_pre(qT, kw, vwT, kcmp, vcmpT, ovT, gates):
    B, H, nq, _, tq = qT.shape
    G = NSA_GROUPS
    S = kw.shape[2]
    nblk = S // SEL_LEN
    ncmp = kcmp.shape[2]
    assert TKV % tq == 0 and nq % PRE_TILES == 0 and PRE_TILES % (TKV // tq) == 0
    assert WINDOW % (tq * PRE_TILES) == 0 and WINDOW % TKV == 0 and S >= WINDOW + tq
    kern = functools.partial(_nsa_pre_kernel, k_sel=min(N_SELECT, nblk))
    per_bg = lambda shape: pl.BlockSpec((1, 1) + shape, lambda b, g: (b, g) + (0,) * len(shape))
    wide = NSA_HPG * tq
    return pl.pallas_call(
        kern,
        grid=(B, G),
        in_specs=[
            pl.BlockSpec((1, NSA_HPG, nq, NSA_HD, tq), lambda b, g: (b, g, 0, 0, 0)),
            per_bg((S, NSA_HD)),
            per_bg((S // TKV, NSA_DV + ONES, TKV)),
            per_bg((ncmp, NSA_HD)),
            per_bg((NSA_DV + ONES, ncmp)),
            pl.BlockSpec((nblk, ncmp), lambda b, g: (0, 0)),
            per_bg((nq, 16, tq)),
        ],
        out_specs=[per_bg((nq, nblk, tq)), per_bg((nq, NSA_HPG * NSA_DV, tq))],
        out_shape=[jax.ShapeDtypeStruct((B, G, nq, nblk, tq), BF16),
                   jax.ShapeDtypeStruct((B, G, nq, NSA_HPG * NSA_DV, tq), F32)],
        scratch_shapes=[pltpu.VMEM((ncmp, wide), F32), pltpu.VMEM((WINDOW + tq, wide), F32)] * PRE_TILES,
        compiler_params=_params(("parallel", "parallel")),
        name="nsa_branches",
    )(qT, kw, vwT, kcmp, vcmpT, ovT, gates)


def _nsa_sel_kernel(qT_ref, csel_ref, ks_ref, vsT_ref, ocw_ref, g_ref, sz_ref, o_ref,
                    qaug_ref, s_a, s_b):
    nq, _, tq = qT_ref.shape[2:]
    nblk = csel_ref.shape[3]
    nsub = TSTEP // TKV
    per = TQ_SEL // tq
    wide = NSA_HPG * TQ_SEL

    row = lax.broadcasted_iota(jnp.int32, (CHUNK, TQ_SEL), 0)
    col = lax.broadcasted_iota(jnp.int32, (CHUNK, TQ_SEL), 1)

    def qtile(sup, p):
        i = sup * (TSTEP // TQ_SEL) + p
        q0 = i * TQ_SEL
        for hh in range(NSA_HPG):
            for u in range(per):
                lanes = slice(hh * TQ_SEL + u * tq, hh * TQ_SEL + (u + 1) * tq)
                qaug_ref[0:NSA_HD, lanes] = qT_ref[0, hh, i * per + u]
                qaug_ref[NSA_HD:NSA_HD + nblk, lanes] = csel_ref[0, 0, i * per + u]
        qaug = qaug_ref[...]

        def scores(j):
            kt = ks_ref[0, 0, j * TSTEP:(j + 1) * TSTEP, :]
            return _dot(kt, qaug)

        def vtile(j):
            return lambda r: vsT_ref[0, 0, j * nsub + r]

        own = p * TQ_SEL
        causal = {own + lo: functools.partial(
            lambda lo, blk: blk + _tile4(jnp.where(lo + row <= col, 0.0, NEG)), lo)
            for lo in range(0, TQ_SEL, CHUNK)}
        init = (jnp.full((1, wide), M_INIT, F32), jnp.zeros((NSA_DV + ONES, wide), F32))
        _, acc = _sweep(scores, vtile, causal, own + TQ_SEL, sup, init, s_a, s_b)
        o_s = acc[0:NSA_DV] * (1.0 / acc[NSA_DV:NSA_DV + 1])

        for hh in range(NSA_HPG):
            rows = slice(hh * NSA_DV, (hh + 1) * NSA_DV)
            for u in range(per):
                t = i * per + u
                lanes = slice(hh * TQ_SEL + u * tq, hh * TQ_SEL + (u + 1) * tq)
                g1 = g_ref[0, 0, t, NSA_HPG + hh:NSA_HPG + hh + 1, :]
                o = ocw_ref[0, 0, t, rows, :] + g1 * o_s[:, lanes]
                o_ref[0, t, rows, :] = (o * sz_ref[0, t, rows, :]).astype(BF16)

    once = jnp.minimum(pl.program_id(0), 0) + 1
    for sup in range((nq * tq) // TSTEP):
        for p in range(TSTEP // TQ_SEL):
            lax.fori_loop(0, once, lambda _, c, sup=sup, p=p: (qtile(sup, p), c)[1], 0)


def _nsa_sel(qT, csel, ks, vsT, ocw, gates, sz):
    B, H, nq, _, tq = qT.shape
    G = NSA_GROUPS
    S = ks.shape[2]
    nblk = S // SEL_LEN
    assert TQ_SEL % tq == 0 and TSTEP % TQ_SEL == 0 and TSTEP % TKV == 0 and S % TSTEP == 0
    per_bg = lambda shape: pl.BlockSpec((1, 1) + shape, lambda b, g: (b, g) + (0,) * len(shape))
    gate = pl.BlockSpec((1, nq, NSA_HPG * NSA_DV, tq), lambda b, g: (b, 0, g, 0))
    return pl.pallas_call(
        _nsa_sel_kernel,
        grid=(B, G),
        in_specs=[
            pl.BlockSpec((1, NSA_HPG, nq, NSA_HD, tq), lambda b, g: (b, g, 0, 0, 0)),
            per_bg((nq, nblk, tq)),
            per_bg((S, NSA_HD + nblk)),
            per_bg((S // TKV, NSA_DV + ONES, TKV)),
            per_bg((nq, NSA_HPG * NSA_DV, tq)),
            per_bg((nq, 16, tq)),
            gate,
        ],
        out_specs=gate,
        out_shape=jax.ShapeDtypeStruct((B, nq, H * NSA_DV, tq), BF16),
        scratch_shapes=[pltpu.VMEM((NSA_HD + nblk, NSA_HPG * TQ_SEL), BF16)]
        + [pltpu.VMEM((TSTEP, NSA_HPG * TQ_SEL), F32)] * 2,
        compiler_params=_params(("parallel", "parallel")),
        name="nsa_selected",
    )(qT, csel, ks, vsT, ocw, gates, sz)


def _nsa_perm():
    src = np.full((NSA_HD,), -1, np.int64)
    src[0:12] = np.arange(0, 12)
    src[12:16] = np.arange(24, 28)
    src[16:64] = np.arange(28, 76)
    src[64:76] = np.arange(12, 24)
    src[76:80] = np.arange(76, 80)
    src[80:96] = np.arange(80, 96)
    return src


def _take_cols(w, src):
    idx = np.where(src >= 0, src, 0)
    out = jnp.take(w, jnp.asarray(idx), axis=-1)
    return jnp.where(jnp.asarray(src >= 0), out, 0.0)


def _pad_last(w, width):
    return jnp.pad(w, [(0, 0)] * (w.ndim - 1) + [(0, width - w.shape[-1])])


def _prep_mla(w_in, g_cq, w_uq, g_ckv, w_ukv, g_q, g_k, w_out):
    D = w_in.shape[0]
    o1 = MLA_Q_LORA + MLA_KV_LORA
    z32 = jnp.zeros((D, 32), F32)
    wa = jnp.concatenate([w_in[:, :o1], w_in[:, o1:o1 + 32], z32, w_in[:, o1 + 32:o1 + 64], z32], axis=1)
    wq = w_uq.reshape(MLA_Q_LORA, MLA_HEADS, MLA_QK)
    zq = jnp.zeros((MLA_Q_LORA, MLA_HEADS, 32), F32)
    wq = jnp.concatenate([wq[:, :, 64:], wq[:, :, :32], zq, wq[:, :, 32:64], zq], axis=2)
    z1 = jnp.zeros((32,), F32)
    gq = jnp.concatenate([g_q[64:], g_q[:32], z1, g_q[32:64], z1]) * C_MLA
    wkv = w_ukv.reshape(MLA_KV_LORA, MLA_HEADS, MLA_NOPE + MLA_V)
    return {
        "wa": wa.astype(BF16),
        "wzT": w_in[:, o1 + MLA_ROPE:].T.astype(BF16),
        "gcq": g_cq.reshape(1, -1), "gckv": g_ckv.reshape(1, -1),
        "wuqT": wq.reshape(MLA_Q_LORA, MLA_HEADS * MLA_HD).T.astype(BF16),
        "gq": jnp.broadcast_to(gq[:, None], (MLA_HD, TT)),
        "wuk": wkv[:, :, :MLA_NOPE].reshape(MLA_KV_LORA, -1).astype(BF16),
        "wuvT": wkv[:, :, MLA_NOPE:].reshape(MLA_KV_LORA, -1).T.astype(BF16),
        "gkn": g_k[64:].reshape(1, -1),
        "gkp": jnp.concatenate([g_k[:32], z1, g_k[32:64], z1]).reshape(1, -1),
        "woT": w_out.T.astype(BF16),
    }


def _prep_nsa(w_in, g_q, g_k, pe_k, w1_k, w2_k, pe_v, w1_v, w2_v, w_out):
    D = w_in.shape[0]
    G = NSA_GROUPS
    src = _nsa_perm()
    offs = np.concatenate([[0], np.cumsum(NSA_SIZES)])
    part = lambda i: w_in[:, offs[i]:offs[i + 1]]
    q, kc, vc, ks, vs, kw, vw, gl, z = [part(i) for i in range(9)]
    perm_heads = lambda w, nh: _take_cols(w.reshape(D, nh, NSA_DK), src).reshape(D, nh * NSA_HD)
    pad_groups = lambda w, d: _pad_last(w.reshape(D, G, d), LANE).reshape(D, G * LANE)
    wtok = jnp.concatenate([perm_heads(ks, G), perm_heads(kw, G), pad_groups(kc, NSA_DK),
                            pad_groups(vc, NSA_DV)], axis=1)
    glr = gl.reshape(D, G, NSA_HPG, 3).transpose(0, 1, 3, 2).reshape(D, G, 3 * NSA_HPG)
    glr = _pad_last(glr, 16).reshape(D, G * 16)
    wf = jnp.concatenate([perm_heads(q, NSA_HEADS), vs, vw, glr, z], axis=1)
    halves = lambda w, d: _pad_last(w.reshape(2, CMP_LEN // 2, d, -1).transpose(0, 1, 3, 2), LANE) \
        .transpose(0, 1, 3, 2).reshape(2, (CMP_LEN // 2) * LANE, -1)
    pe_flat = lambda pe: _pad_last(pe, LANE).reshape(2, (CMP_LEN // 2) * LANE)
    return {
        "wtok": wtok.astype(BF16),
        "wfT": wf.T.astype(BF16),
        "gq": jnp.broadcast_to((_take_cols(g_q, src) * C_NSA)[:, None], (NSA_HD, TT)),
        "gk": _take_cols(g_k, src),
        "pek": pe_flat(pe_k), "pev": pe_flat(pe_v),
        "w1k": _pad_last(halves(w1_k, NSA_DK), LANE).astype(BF16),
        "w2k": _pad_last(_take_cols(w2_k, src).T, LANE).T.astype(BF16),
        "w1v": _pad_last(halves(w1_v, NSA_DV), LANE).astype(BF16),
        "w2vT": _pad_last(w2_v.T, LANE).astype(BF16),
        "woT": w_out.T.astype(BF16),
    }


def _overlap_T(S):
    ncmp = S // CMP_STRIDE
    nblk = S // SEL_LEN
    n_cmp = (S - CMP_LEN) // CMP_STRIDE + 1
    cs = np.arange(ncmp) * CMP_STRIDE
    ss = np.arange(nblk) * SEL_LEN
    ov = (cs[None, :] < ss[:, None] + SEL_LEN) & (cs[None, :] + CMP_LEN > ss[:, None])
    ov = ov & (np.arange(ncmp)[None, :] < n_cmp)
    return jnp.asarray(ov.astype(np.float32)).astype(BF16)


def _mla_layer(x, ng, w, tabs):
    qT, k, vT, sz = _mla_in(x, ng.reshape(1, -1), w, tabs)
    og = _mla_attn(qT, k, vT, sz)
    return _out_proj(og, w, x)


def _nsa_layer(x, ng, w, tabs, ovT):
    B, S, _ = x.shape
    qT, ks, kw, kc16, vc16, vsT, vwT, gates, sz = _nsa_in(x, ng.reshape(1, -1), w, tabs)
    kcmp, vcmpT = _nsa_cmp(kc16, vc16, w, tabs)
    csel, ocw = _nsa_pre(qT, kw, vwT, kcmp, vcmpT, ovT, gates)
    og = _nsa_sel(qT, csel, ks, vsT, ocw, gates, sz)
    return _out_proj(og, w, x)


def kernel(x, positions, norm_g, mla_w_in, mla_g_cq, mla_w_uq, mla_g_ckv, mla_w_ukv, mla_g_q, mla_g_k, mla_w_out, nsa_w_in, nsa_g_q, nsa_g_k, nsa_pe_k, nsa_w1_k, nsa_w2_k, nsa_pe_v, nsa_w1_v, nsa_w2_v, nsa_w_out):
    B, S, _ = x.shape
    tabs = _rope_tables(positions)
    ovT = _overlap_T(S)
    w_mla = jax.vmap(_prep_mla)(mla_w_in, mla_g_cq, mla_w_uq, mla_g_ckv, mla_w_ukv, mla_g_q,
                                mla_g_k, mla_w_out)
    w_nsa = jax.vmap(_prep_nsa)(nsa_w_in, nsa_g_q, nsa_g_k, nsa_pe_k, nsa_w1_k, nsa_w2_k,
                                nsa_pe_v, nsa_w1_v, nsa_w2_v, nsa_w_out)
    for i in range(DEPTH):
        j = i // N_MIXERS
        if i % N_MIXERS == 0:
            x = _mla_layer(x, norm_g[i], _LayerWeights(w_mla, j), tabs)
        else:
            x = _nsa_layer(x, norm_g[i], _LayerWeights(w_nsa, j), tabs, ovT)
    return x
```

```python
import functools

import numpy as np
import jax
import jax.numpy as jnp
from jax import lax
from jax.experimental import pallas as pl
from jax.experimental.pallas import tpu as pltpu

F32 = jnp.float32
BF16 = jnp.bfloat16

D_MODEL = 1024
DEPTH = 4
N_MIXERS = 2
ROPE_THETA = 500000.0
RMS_EPS = 1e-6
MLA_HEADS = 16
MLA_NOPE = 128
MLA_ROPE = 64
MLA_V = 128
MLA_QK = MLA_NOPE + MLA_ROPE
MLA_Q_LORA = 384
MLA_KV_LORA = 256
MLA_HD = 256
NSA_HEADS = 16
NSA_GROUPS = 4
NSA_HPG = NSA_HEADS // NSA_GROUPS
NSA_DK = 96
NSA_DV = 64
NSA_ROT = NSA_DK // 4
NSA_HALF = NSA_ROT // 2
NSA_HD = 128
CMP_LEN = 32
CMP_STRIDE = 16
SEL_LEN = 64
N_SELECT = 16
WINDOW = 512
SEL_FORCE = 1e4
NSA_SIZES = (NSA_HEADS * NSA_DK,
             NSA_GROUPS * NSA_DK, NSA_GROUPS * NSA_DV,
             NSA_GROUPS * NSA_DK, NSA_GROUPS * NSA_DV,
             NSA_GROUPS * NSA_DK, NSA_GROUPS * NSA_DV,
             3 * NSA_HEADS, NSA_HEADS * NSA_DV)

LANE = 128
SUB = 8
TT = 256
TO = 512
TKV = 256
TSTEP = 512
TQ_MLA = 512
MLA_HEADS_PER_STEP = 2
TQ_NSA = 128
TQ_SEL = 256
PRE_TILES = 4
RANK_CLASSES = 3
CHUNK = 64
ONES = 16
NEG = -30000.0
M_INIT = -1e30
LOG2E = 1.4426950408889634
C_MLA = (MLA_QK ** -0.5) * LOG2E
C_NSA = (NSA_DK ** -0.5) * LOG2E
VMEM_LIMIT = 56 * 1024 * 1024


def _dot(a, b):
    return jnp.dot(a, b, preferred_element_type=F32)


def _dot_nt(a, b):
    return lax.dot_general(a, b, (((1,), (1,)), ((), ())), preferred_element_type=F32)


def _sigmoid(x):
    return 1.0 / (1.0 + jnp.exp(-x))


def _rms_rows(x, g):
    ms = jnp.mean(x * x, axis=-1, keepdims=True)
    return x * lax.rsqrt(ms + RMS_EPS) * g


def _params(sem):
    return pltpu.CompilerParams(dimension_semantics=sem, vmem_limit_bytes=VMEM_LIMIT)


class _LayerWeights:
    def __init__(self, stack, layer):
        self.stack, self.layer = stack, layer

    def __getitem__(self, name):
        return self.stack[name]

    def spec(self, name):
        shape = self.stack[name].shape[1:]
        layer = self.layer
        return pl.BlockSpec((None,) + shape, lambda *_: (layer,) + (0,) * len(shape))


def _rope_kernel(pos_ref, invf_ref, cm_ref, sm_ref, cn_ref, sn_ref, ctm_ref, stm_ref, ctn_ref, stn_ref):
    ang = invf_ref[...] * pos_ref[...]
    c, s = jnp.cos(ang), jnp.sin(ang)
    hm = MLA_ROPE // 2
    cm, sm, cn, sn = c[:hm], s[:hm], c[hm:hm + 16], s[hm:hm + 16]
    cm_ref[...], sm_ref[...], cn_ref[...], sn_ref[...] = cm, sm, cn, sn
    tn = ang.shape[1]
    one = lambda r: jnp.ones((r, tn), F32)
    zero = lambda r: jnp.zeros((r, tn), F32)
    ctm_ref[...] = jnp.concatenate([cm, one(32), cm, one(32)], axis=0).T
    stm_ref[...] = jnp.concatenate([-sm, zero(32), sm, zero(32)], axis=0).T
    ctn_ref[...] = jnp.concatenate([cn, one(48), cn, one(48)], axis=0).T
    stn_ref[...] = jnp.concatenate([-sn, zero(48), sn, zero(48)], axis=0).T


def _rope_tables(positions):
    B, S = positions.shape
    n = positions.size
    tn = 512
    pos = positions.reshape(1, n).astype(F32)
    half_m = MLA_ROPE // 2
    inv_m = ROPE_THETA ** (-jnp.arange(half_m, dtype=F32) / half_m)
    inv_n = ROPE_THETA ** (-jnp.arange(NSA_HALF, dtype=F32) / NSA_HALF)
    invf = jnp.concatenate([inv_m, inv_n, jnp.zeros((16 - NSA_HALF,), F32)])
    rows = invf.shape[0]
    invf = jnp.broadcast_to(invf[:, None], (rows, tn))
    feat = lambda r: pl.BlockSpec((r, tn), lambda i: (0, i))
    tok = pl.BlockSpec((tn, LANE), lambda i: (i, 0))
    cm, sm, cn, sn, ctm, stm, ctn, stn = pl.pallas_call(
        _rope_kernel,
        grid=(n // tn,),
        in_specs=[pl.BlockSpec((1, tn), lambda i: (0, i)),
                  pl.BlockSpec((rows, tn), lambda i: (0, 0))],
        out_specs=[feat(half_m), feat(half_m), feat(16), feat(16), tok, tok, tok, tok],
        out_shape=[jax.ShapeDtypeStruct((half_m, n), F32)] * 2
        + [jax.ShapeDtypeStruct((16, n), F32)] * 2 + [jax.ShapeDtypeStruct((n, LANE), F32)] * 4,
        compiler_params=_params(("parallel",)),
        name="rope_tables",
    )(pos, invf)
    ncmp = S // CMP_STRIDE
    last = CMP_LEN - 1
    pick = lambda t: jnp.pad(t.reshape(B, S, LANE)[:, last::CMP_STRIDE],
                             ((0, 0), (0, ncmp - (S - last + CMP_STRIDE - 1) // CMP_STRIDE), (0, 0)))
    return {"cosT_mla": cm, "sinT_mla": sm, "cosT_nsa": cn, "sinT_nsa": sn,
            "ct_mla": ctm, "st_mla": stm, "ct_nsa": ctn, "st_nsa": stn,
            "ct_cmp": pick(ctn), "st_cmp": pick(stn)}


def _mla_in_kernel(x_ref, ng_ref, wa_ref, wzT_ref, gcq_ref, gckv_ref, wuqT_ref, gq_ref,
                   wuk_ref, wuvT_ref, gkn_ref, gkp_ref, ct_ref, st_ref, cosT_ref, sinT_ref,
                   qT_ref, k_ref, vT_ref, sz_ref):
    x = x_ref[0]
    n = x.shape[0]
    h = _rms_rows(x, ng_ref[...]).astype(BF16)
    pa = _dot(h, wa_ref[...])
    z = _dot_nt(wzT_ref[...], h)
    sz_ref[0, 0] = z * _sigmoid(z)

    cqn = _rms_rows(pa[:, :MLA_Q_LORA], gcq_ref[...]).astype(BF16)
    ckvn = _rms_rows(pa[:, MLA_Q_LORA:MLA_Q_LORA + MLA_KV_LORA], gckv_ref[...]).astype(BF16)
    kpe = pa[:, MLA_Q_LORA + MLA_KV_LORA:]

    qa = _dot_nt(wuqT_ref[...], cqn)
    cos = cosT_ref[...]
    sin = sinT_ref[...]
    gq = gq_ref[...]
    zeros32 = jnp.zeros((32, n), BF16)
    for hd in range(MLA_HEADS):
        blk = qa[hd * MLA_HD:(hd + 1) * MLA_HD]
        ss = jnp.sum(blk * blk, axis=0, keepdims=True)
        qn = blk * lax.rsqrt(ss * (1.0 / MLA_QK) + RMS_EPS) * gq
        x1 = qn[128:160]
        x2 = qn[192:224]
        qT_ref[0, hd, 0, 0:128, :] = qn[0:128].astype(BF16)
        qT_ref[0, hd, 0, 128:160, :] = (x1 * cos - x2 * sin).astype(BF16)
        qT_ref[0, hd, 0, 160:192, :] = zeros32
        qT_ref[0, hd, 0, 192:224, :] = (x2 * cos + x1 * sin).astype(BF16)
        qT_ref[0, hd, 0, 224:256, :] = zeros32

    kn = _dot(ckvn, wuk_ref[...])
    ss_pe = jnp.sum(kpe * kpe, axis=-1, keepdims=True)
    kpg = kpe * gkp_ref[...]
    prot = kpg * ct_ref[...] + pltpu.roll(kpg, 64, axis=1) * st_ref[...]
    gkn = gkn_ref[...]
    for hd in range(MLA_HEADS):
        kb = kn[:, hd * MLA_NOPE:(hd + 1) * MLA_NOPE]
        ss = jnp.sum(kb * kb, axis=-1, keepdims=True) + ss_pe
        r = lax.rsqrt(ss * (1.0 / MLA_QK) + RMS_EPS)
        k_ref[0, hd, :, 0:128] = (kb * r * gkn).astype(BF16)
        k_ref[0, hd, :, 128:256] = (prot * r).astype(BF16)

    va = _dot_nt(wuvT_ref[...], ckvn)
    ones = jnp.ones((ONES, n), BF16)
    for hd in range(MLA_HEADS):
        vT_ref[0, hd, 0, 0:MLA_V, :] = va[hd * MLA_V:(hd + 1) * MLA_V].astype(BF16)
        vT_ref[0, hd, 0, MLA_V:MLA_V + ONES, :] = ones


def _mla_in(x, ng, w, tabs):
    B, S, D = x.shape
    nt = S // TT
    r = TQ_MLA // TT
    assert TT == TKV and TQ_MLA % TT == 0
    full = lambda shape: pl.BlockSpec(shape, lambda b, t: (0,) * len(shape))
    tok = lambda b, t: (b * nt + t, 0)
    feat = lambda b, t: (0, b * nt + t)
    return pl.pallas_call(
        _mla_in_kernel,
        grid=(B, nt),
        in_specs=[
            pl.BlockSpec((1, TT, D), lambda b, t: (b, t, 0)),
            full((1, D)),
            w.spec("wa"), w.spec("wzT"), w.spec("gcq"), w.spec("gckv"),
            w.spec("wuqT"), w.spec("gq"), w.spec("wuk"), w.spec("wuvT"),
            w.spec("gkn"), w.spec("gkp"),
            pl.BlockSpec((TT, LANE), tok), pl.BlockSpec((TT, LANE), tok),
            pl.BlockSpec((32, TT), feat), pl.BlockSpec((32, TT), feat),
        ],
        out_specs=[
            pl.BlockSpec((1, MLA_HEADS, 1, MLA_HD, TT), lambda b, t: (b, 0, t // r, 0, t % r)),
            pl.BlockSpec((1, MLA_HEADS, TT, MLA_HD), lambda b, t: (b, 0, t, 0)),
            pl.BlockSpec((1, MLA_HEADS, 1, MLA_V + ONES, TT), lambda b, t: (b, 0, t, 0, 0)),
            pl.BlockSpec((1, 1, MLA_HEADS * MLA_V, TT), lambda b, t: (b, t // r, 0, t % r)),
        ],
        out_shape=[
            jax.ShapeDtypeStruct((B, MLA_HEADS, S // TQ_MLA, MLA_HD, TQ_MLA), BF16),
            jax.ShapeDtypeStruct((B, MLA_HEADS, S, MLA_HD), BF16),
            jax.ShapeDtypeStruct((B, MLA_HEADS, nt, MLA_V + ONES, TT), BF16),
            jax.ShapeDtypeStruct((B, S // TQ_MLA, MLA_HEADS * MLA_V, TQ_MLA), F32),
        ],
        compiler_params=_params(("parallel", "parallel")),
        name="mla_in_proj",
    )(x, ng, w["wa"], w["wzT"], w["gcq"], w["gckv"], w["wuqT"], w["gq"], w["wuk"],
      w["wuvT"], w["gkn"], w["gkp"], tabs["ct_mla"], tabs["st_mla"], tabs["cosT_mla"],
      tabs["sinT_mla"])


def _softmax_step_ref(s_ref, carry, vtile=None, segs=None, bias=None):
    m, acc = carry
    n, w = s_ref.shape
    segs = segs or [(t * TKV, (t + 1) * TKV, functools.partial(vtile, t)) for t in range(n // TKV)]
    bias = bias or {}
    fold = lambda a, op: op(a.reshape(a.shape[0] // SUB, SUB, w), axis=0)
    mx = None
    for lo in range(0, n, CHUNK):
        blk = s_ref[lo:lo + CHUNK, :]
        if lo in bias:
            blk = bias[lo](blk) if callable(bias[lo]) else blk + bias[lo]
            s_ref[lo:lo + CHUNK, :] = blk
        mx = fold(blk, jnp.max) if mx is None else jnp.maximum(mx, fold(blk, jnp.max))
    m_new = jnp.maximum(m, jnp.max(mx, axis=0, keepdims=True))
    alpha = jnp.exp2(m - m_new)
    pv = None
    for seg_lo, seg_hi, value_tile in segs:
        parts = [jnp.exp2(s_ref[lo:lo + CHUNK, :] - m_new).astype(BF16)
                 for lo in range(seg_lo, seg_hi, CHUNK)]
        d = _dot(value_tile(), jnp.concatenate(parts, axis=0))
        pv = d if pv is None else pv + d
    return m_new, alpha * acc + pv


def _sweep_steps(scores, vtile, last_bias, last_rows, n_full, carry, s_a, s_b):
    bufs = (s_a, s_b)
    s_a[...] = scores(0)
    for j in range(n_full):
        bufs[(j + 1) % 2][...] = scores(j + 1)
        carry = _softmax_step_ref(bufs[j % 2], carry, vtile(j))
        yield
    src = bufs[n_full % 2]
    return _softmax_step_ref(src.at[0:last_rows], carry, vtile(n_full), bias=last_bias)


def _sweep(*args):
    steps = _sweep_steps(*args)
    while True:
        try:
            next(steps)
        except StopIteration as done:
            return done.value


def _round_robin(generators):
    live = list(generators)
    while live:
        live = [g for g in live if next(g, "done") != "done"]


def _mla_attn_kernel(qT_ref, k_ref, vT_ref, sz_ref, o_ref, *scratch):
    nh, nq, _, tq = qT_ref.shape[1:]
    nsub = TSTEP // TKV

    row = lax.broadcasted_iota(jnp.int32, (CHUNK, tq), 0)
    col = lax.broadcasted_iota(jnp.int32, (CHUNK, tq), 1)
    causal = {lo: functools.partial(lambda lo, blk: jnp.where(lo + row <= col, blk, NEG), lo)
              for lo in range(0, TSTEP, CHUNK)}

    def qtile(h, i):
        s_a, s_b = scratch[2 * h], scratch[2 * h + 1]
        rows = slice(h * MLA_V, (h + 1) * MLA_V)
        q = qT_ref[0, h, i]
        scores = lambda j: _dot(k_ref[0, h, j * TSTEP:(j + 1) * TSTEP, :], q)
        vtile = lambda j: (lambda r: vT_ref[0, h, j * nsub + r])
        init = (jnp.full((1, tq), M_INIT, F32), jnp.zeros((MLA_V + ONES, tq), F32))
        _, acc = yield from _sweep_steps(scores, vtile, causal, TSTEP, i, init, s_a, s_b)
        o = acc[0:MLA_V] / acc[MLA_V:MLA_V + 1]
        o_ref[0, i, rows, :] = (o * sz_ref[0, i, rows, :]).astype(BF16)

    once = jnp.minimum(pl.program_id(0), 0) + 1
    for i in range(nq):
        lax.fori_loop(0, once, lambda _, c, i=i: (_round_robin(qtile(h, i) for h in range(nh)), c)[1], 0)


def _mla_attn(qT, k, vT, sz):
    B, H, nq, _, tq = qT.shape
    S = k.shape[2]
    nh = MLA_HEADS_PER_STEP
    assert tq == TSTEP and TSTEP % TKV == 0 and H % nh == 0
    per_bh = lambda shape: pl.BlockSpec((1, nh) + shape, lambda b, h: (b, h) + (0,) * len(shape))
    gate = pl.BlockSpec((1, nq, nh * MLA_V, tq), lambda b, h: (b, 0, h, 0))
    return pl.pallas_call(
        _mla_attn_kernel,
        grid=(B, H // nh),
        in_specs=[per_bh((nq, MLA_HD, tq)), per_bh((S, MLA_HD)),
                  per_bh((S // TKV, MLA_V + ONES, TKV)), gate],
        out_specs=gate,
        out_shape=jax.ShapeDtypeStruct((B, nq, H * MLA_V, tq), BF16),
        scratch_shapes=[pltpu.VMEM((TSTEP, tq), F32)] * (2 * nh),
        compiler_params=_params(("parallel", "parallel")),
        name="mla_attention",
    )(qT, k, vT, sz)


def _out_proj_kernel(og_ref, wT_ref, x_ref, o_ref):
    ntile = og_ref.shape[1]
    og = og_ref[0, 0] if ntile == 1 else jnp.concatenate([og_ref[0, r] for r in range(ntile)], axis=1)
    yT = _dot(wT_ref[...], og)
    o_ref[0] = x_ref[0] + yT.T


def _out_proj(og, w, x):
    B, S, D = x.shape
    _, _, K, W = og.shape
    r = TO // W
    wT = w["woT"]
    return pl.pallas_call(
        _out_proj_kernel,
        grid=(B, S // TO),
        in_specs=[
            pl.BlockSpec((1, r, K, W), lambda b, t: (b, t, 0, 0)),
            w.spec("woT"),
            pl.BlockSpec((1, TO, D), lambda b, t: (b, t, 0)),
        ],
        out_specs=pl.BlockSpec((1, TO, D), lambda b, t: (b, t, 0)),
        out_shape=jax.ShapeDtypeStruct((B, S, D), F32),
        compiler_params=_params(("parallel", "parallel")),
        name="out_proj",
    )(og, wT, x)


def _nsa_in_kernel(x_ref, ng_ref, wtok_ref, wfT_ref, gq_ref, gk_ref, ct_ref, st_ref,
                   cosT_ref, sinT_ref,
                   qT_ref, ks_ref, kw_ref, kc_ref, vc_ref, vsT_ref, vwT_ref, g_ref, sz_ref, cv_scr):
    t = pl.program_id(1)
    x = x_ref[0]
    n = x.shape[0]
    tq = qT_ref.shape[-1]
    h = _rms_rows(x, ng_ref[...]).astype(BF16)
    G = NSA_GROUPS
    lanes = [slice(u * tq, (u + 1) * tq) for u in range(n // tq)]

    fa = _dot_nt(wfT_ref[...], h)
    cos = cosT_ref[...]
    sin = sinT_ref[...]
    gq = gq_ref[...]
    for hd in range(NSA_HEADS):
        blk = fa[hd * NSA_HD:(hd + 1) * NSA_HD]
        ss = jnp.sum(blk * blk, axis=0, keepdims=True)
        qn = blk * lax.rsqrt(ss * (1.0 / NSA_DK) + RMS_EPS) * gq
        x1 = qn[0:16]
        x2 = qn[64:80]
        r1 = (x1 * cos - x2 * sin).astype(BF16)
        r2 = (x2 * cos + x1 * sin).astype(BF16)
        qb = qn.astype(BF16)
        for u, sl in enumerate(lanes):
            qT_ref[0, hd, u, 0:16, :] = r1[:, sl]
            qT_ref[0, hd, u, 16:64, :] = qb[16:64, sl]
            qT_ref[0, hd, u, 64:80, :] = r2[:, sl]
            qT_ref[0, hd, u, 80:128, :] = qb[80:128, sl]
    off = NSA_HEADS * NSA_HD
    ones = jnp.ones((ONES, n), BF16)
    for ref in (vsT_ref, vwT_ref):
        for g in range(G):
            ref[0, g, 0, 0:NSA_DV, :] = fa[off + g * NSA_DV:off + (g + 1) * NSA_DV].astype(BF16)
            ref[0, g, 0, NSA_DV:NSA_DV + ONES, :] = ones
        off += G * NSA_DV
    sg = _sigmoid(fa[off:off + 16 * G])
    for g in range(G):
        for u, sl in enumerate(lanes):
            g_ref[0, g, u] = sg[g * 16:(g + 1) * 16, sl]
    off += 16 * G
    z = fa[off:off + NSA_HEADS * NSA_DV]
    sz = z * _sigmoid(z)
    for u, sl in enumerate(lanes):
        sz_ref[0, u] = sz[:, sl]

    pt = _dot(h, wtok_ref[...])
    ct = ct_ref[...]
    st = st_ref[...]
    nblk = ks_ref.shape[-1] - NSA_HD
    row = lax.broadcasted_iota(jnp.int32, (n, nblk), 0) + t * n
    col = lax.broadcasted_iota(jnp.int32, (n, nblk), 1)
    onehot = jnp.where((row // SEL_LEN) == col, 1.0, 0.0).astype(BF16)
    for br, ref in ((0, ks_ref), (1, kw_ref)):
        gk = gk_ref[br + 1:br + 2, :]
        for g in range(G):
            kb = pt[:, (br * G + g) * NSA_HD:(br * G + g + 1) * NSA_HD]
            ss = jnp.sum(kb * kb, axis=-1, keepdims=True)
            kn = kb * lax.rsqrt(ss * (1.0 / NSA_DK) + RMS_EPS) * gk
            kr = kn * ct + pltpu.roll(kn, 64, axis=1) * st
            ref[0, g, :, 0:NSA_HD] = kr.astype(BF16)
            if br == 0:
                ref[0, g, :, NSA_HD:NSA_HD + nblk] = onehot
    for br, ref in ((2, kc_ref), (3, vc_ref)):
        for g in range(G):
            stage = cv_scr.at[(br - 2) * G + g]
            stage[...] = pt[:, (br * G + g) * LANE:(br * G + g + 1) * LANE]
            for l in range(CMP_STRIDE):
                ref[0, g, :, l * LANE:(l + 1) * LANE] = stage[pl.ds(l, n // CMP_STRIDE, stride=CMP_STRIDE), :]


def _nsa_in(x, ng, w, tabs):
    B, S, D = x.shape
    nt = S // TT
    G = NSA_GROUPS
    nblk = S // SEL_LEN
    r = TT // TQ_NSA
    nq = S // TQ_NSA
    assert TT == TKV and TT % TQ_NSA == 0
    full = lambda shape: pl.BlockSpec(shape, lambda b, t: (0,) * len(shape))
    tok = lambda b, t: (b * nt + t, 0)
    feat = lambda b, t: (0, b * nt + t)
    tokmaj = lambda width: pl.BlockSpec((1, G, TT, width), lambda b, t: (b, 0, t, 0))
    vtile = pl.BlockSpec((1, G, 1, NSA_DV + ONES, TT), lambda b, t: (b, 0, t, 0, 0))
    cmp_in = pl.BlockSpec((1, G, TT // CMP_STRIDE, CMP_STRIDE * LANE), lambda b, t: (b, 0, t, 0))
    vshape = jax.ShapeDtypeStruct((B, G, nt, NSA_DV + ONES, TT), BF16)
    return pl.pallas_call(
        _nsa_in_kernel,
        grid=(B, nt),
        in_specs=[
            pl.BlockSpec((1, TT, D), lambda b, t: (b, t, 0)),
            full((1, D)),
            w.spec("wtok"), w.spec("wfT"), w.spec("gq"), w.spec("gk"),
            pl.BlockSpec((TT, LANE), tok), pl.BlockSpec((TT, LANE), tok),
            pl.BlockSpec((16, TT), feat), pl.BlockSpec((16, TT), feat),
        ],
        out_specs=[
            pl.BlockSpec((1, NSA_HEADS, r, NSA_HD, TQ_NSA), lambda b, t: (b, 0, t, 0, 0)),
            tokmaj(NSA_HD + nblk), tokmaj(NSA_HD), cmp_in, cmp_in,
            vtile, vtile,
            pl.BlockSpec((1, G, r, 16, TQ_NSA), lambda b, t: (b, 0, t, 0, 0)),
            pl.BlockSpec((1, r, NSA_HEADS * NSA_DV, TQ_NSA), lambda b, t: (b, t, 0, 0)),
        ],
        out_shape=[
            jax.ShapeDtypeStruct((B, NSA_HEADS, nq, NSA_HD, TQ_NSA), BF16),
            jax.ShapeDtypeStruct((B, G, S, NSA_HD + nblk), BF16),
            jax.ShapeDtypeStruct((B, G, S, NSA_HD), BF16),
            jax.ShapeDtypeStruct((B, G, S // CMP_STRIDE, CMP_STRIDE * LANE), F32),
            jax.ShapeDtypeStruct((B, G, S // CMP_STRIDE, CMP_STRIDE * LANE), F32),
            vshape, vshape,
            jax.ShapeDtypeStruct((B, G, nq, 16, TQ_NSA), F32),
            jax.ShapeDtypeStruct((B, nq, NSA_HEADS * NSA_DV, TQ_NSA), F32),
        ],
        scratch_shapes=[pltpu.VMEM((2 * G, TT, LANE), F32)],
        compiler_params=_params(("parallel", "parallel")),
        name="nsa_in_proj",
    )(x, ng, w["wtok"], w["wfT"], w["gq"], w["gk"], tabs["ct_nsa"], tabs["st_nsa"],
      tabs["cosT_nsa"], tabs["sinT_nsa"])


def _nsa_cmp_kernel(kc_ref, vc_ref, pek_ref, pev_ref, w1k_ref, w2k_ref, w1v_ref, w2vT_ref,
                    gk_ref, ct_ref, st_ref, kcmp_ref, vcmpT_ref):
    nrow = kc_ref.shape[2]

    def pre(x, pe_ref, w1_ref):
        lo = _dot((x + pe_ref[0:1, :]).astype(BF16), w1_ref[0])
        hi = _dot((x + pe_ref[1:2, :]).astype(BF16), w1_ref[1])
        return lo + pltpu.roll(hi, nrow - 1, axis=0)

    a = pre(kc_ref[0, 0], pek_ref, w1k_ref)
    kc = _dot((a * _sigmoid(a)).astype(BF16), w2k_ref[...])
    ss = jnp.sum(kc * kc, axis=-1, keepdims=True)
    kn = kc * lax.rsqrt(ss * (1.0 / NSA_DK) + RMS_EPS) * gk_ref[0:1, :]
    kr = kn * ct_ref[0] + pltpu.roll(kn, 64, axis=1) * st_ref[0]
    kcmp_ref[0, 0] = kr.astype(BF16)

    a = pre(vc_ref[0, 0], pev_ref, w1v_ref)
    sv = (a * _sigmoid(a)).astype(BF16)
    vcmpT_ref[0, 0, 0:NSA_DV, :] = _dot_nt(w2vT_ref[...], sv).astype(BF16)
    vcmpT_ref[0, 0, NSA_DV:NSA_DV + ONES, :] = jnp.ones((ONES, nrow), BF16)


def _nsa_cmp(kc16, vc16, w, tabs):
    B, G, nrow, width = kc16.shape
    blk = pl.BlockSpec((1, 1, nrow, width), lambda b, g: (b, g, 0, 0))
    tab = pl.BlockSpec((1, nrow, LANE), lambda b, g: (b, 0, 0))
    return pl.pallas_call(
        _nsa_cmp_kernel,
        grid=(B, G),
        in_specs=[blk, blk, w.spec("pek"), w.spec("pev"), w.spec("w1k"), w.spec("w2k"),
                  w.spec("w1v"), w.spec("w2vT"), w.spec("gk"), tab, tab],
        out_specs=[pl.BlockSpec((1, 1, nrow, NSA_HD), lambda b, g: (b, g, 0, 0)),
                   pl.BlockSpec((1, 1, NSA_DV + ONES, nrow), lambda b, g: (b, g, 0, 0))],
        out_shape=[jax.ShapeDtypeStruct((B, G, nrow, NSA_HD), BF16),
                   jax.ShapeDtypeStruct((B, G, NSA_DV + ONES, nrow), BF16)],
        compiler_params=_params(("parallel", "parallel")),
        name="nsa_compress",
    )(kc16, vc16, w["pek"], w["pev"], w["w1k"], w["w2k"], w["w1v"], w["w2vT"], w["gk"],
      tabs["ct_cmp"], tabs["st_cmp"])


def _tile4(a):
    return jnp.concatenate([a] * NSA_HPG, axis=1)


def _cmp_exp(s_ref, bias):
    n, w = s_ref.shape
    fold = lambda a, op: op(a.reshape(a.shape[0] // SUB, SUB, w), axis=0)
    rows = [slice(lo, lo + CHUNK) for lo in range(0, n, CHUNK)]
    mx = None
    for sl in rows:
        blk = bias[sl.start](s_ref[sl, :])
        s_ref[sl, :] = blk
        mx = fold(blk, jnp.max) if mx is None else jnp.maximum(mx, fold(blk, jnp.max))
    mc = jnp.maximum(jnp.max(mx, axis=0, keepdims=True), 0.5 * NEG)
    return jnp.concatenate([jnp.exp2(s_ref[sl, :] - mc).astype(BF16) for sl in rows], axis=0)


def _nsa_pre_kernel(qT_ref, kw_ref, vwT_ref, kc_ref, vcT_ref, ovT_ref, g_ref,
                    csel_ref, ocw_ref, *scratch, k_sel):
    nq, _, tq = qT_ref.shape[2:]
    ncmp = kc_ref.shape[2]
    nblk = ovT_ref.shape[0]

    wk = WINDOW + tq
    rloc = lax.broadcasted_iota(jnp.int32, (CHUNK, tq), 0)
    cloc = lax.broadcasted_iota(jnp.int32, (CHUNK, tq), 1)
    mask = lambda ok, lo: (lambda blk: blk + _tile4(jnp.where(ok(lo), 0.0, NEG)))

    def one_tile(t, u, clamped, jlim):
        q0 = t * tq
        qT = jnp.concatenate([qT_ref[0, hh, t] for hh in range(NSA_HPG)], axis=1)
        tok = q0 + lax.broadcasted_iota(jnp.int32, (1, tq), 1)

        sc_scr, sw_scr = scratch[2 * u], scratch[2 * u + 1]
        sc_scr[...] = _dot(kc_ref[0, 0], qT)
        yield
        cvalid = lambda lo: ((lo + rloc) * CMP_STRIDE + (CMP_LEN - 1)) <= tok
        cbias = {lo: mask(cvalid, lo) for lo in range(0, ncmp, CHUNK)}
        ec = _cmp_exp(sc_scr, cbias)
        ac = _dot(vcT_ref[0, 0], ec)
        dc = ac[NSA_DV:NSA_DV + 1]
        inv = 1.0 / jnp.where(dc > 0, dc, 1.0)
        o_c = ac[0:NSA_DV] * inv

        if clamped:
            start, off = 0, 0
            wbias = {lo: mask(lambda lo: lo + rloc <= tok, lo) for lo in range(0, wk, CHUNK)}
        else:
            start, off = q0 - WINDOW, (u % (TKV // tq)) * tq
            wbias = {lo: mask(lambda lo: lo + rloc > cloc, lo) for lo in range(0, tq, CHUNK)}
            wbias.update({lo: mask(lambda lo: lo - WINDOW + rloc <= cloc, lo)
                          for lo in range(WINDOW, wk, CHUNK)})
        sw_scr[...] = _dot(kw_ref[0, 0, pl.ds(pl.multiple_of(start, tq), wk), :], qT)
        yield
        w0 = start // TKV
        segs, pos, tile = [], 0, 0
        while pos < wk:
            width = min(TKV - off, wk - pos)
            segs.append((pos, pos + width, functools.partial(
                lambda tile, off, width: vwT_ref[0, 0, w0 + tile, :, off:off + width], tile, off, width)))
            pos, tile, off = pos + width, tile + 1, 0
        init = (jnp.full((1, NSA_HPG * tq), M_INIT, F32),
                jnp.zeros((NSA_DV + ONES, NSA_HPG * tq), F32))
        _, aw = _softmax_step_ref(sw_scr, init, segs=segs, bias=wbias)
        o_w = aw[0:NSA_DV] * (1.0 / aw[NSA_DV:NSA_DV + 1])
        yield

        if jlim:
            iw = _dot(ovT_ref[0:jlim, :], ec) * inv
            imp = iw[:, 0:tq]
            for hh in range(1, NSA_HPG):
                imp = imp + iw[:, hh * tq:(hh + 1) * tq]
            bidx = lax.broadcasted_iota(jnp.int32, (jlim, tq), 0)
            cur = tok // SEL_LEN
            forced = (bidx == 0) | (bidx == cur) | (bidx == cur - 1)
            valid = (bidx * SEL_LEN) <= tok
            score = jnp.where(forced, SEL_FORCE, jnp.where(valid, imp, -1.0))
            grp = [score[r * SUB:(r + 1) * SUB] for r in range(jlim // SUB)]
            bsub = lax.broadcasted_iota(jnp.int32, (SUB, tq), 0)
            cnt = [jnp.zeros((SUB, tq), F32) for _ in grp]
            for j in range(jlim):
                rowj = score[j:j + 1, :]
                for r in range(jlim // SUB):
                    if r * SUB > j:
                        cnt[r] = cnt[r] + jnp.where(rowj >= grp[r], 1.0, 0.0)
                    elif r * SUB + SUB - 1 < j:
                        cnt[r] = cnt[r] + jnp.where(rowj > grp[r], 1.0, 0.0)
                    else:
                        tie = jnp.where(bsub + r * SUB > j, 1.0, 0.0)
                        cnt[r] = cnt[r] + jnp.where(rowj > grp[r], 1.0, 0.0)
                        cnt[r] = cnt[r] + jnp.where(rowj == grp[r], tie, 0.0)
            cnt = jnp.concatenate(cnt, axis=0)
            csel_ref[0, 0, t, 0:jlim, :] = jnp.where(cnt < k_sel, 0.0, NEG).astype(BF16)
        if jlim < nblk:
            csel_ref[0, 0, t, jlim:nblk, :] = jnp.zeros((nblk - jlim, tq), BF16)

        for hh in range(NSA_HPG):
            sl = slice(hh * tq, (hh + 1) * tq)
            g0 = g_ref[0, 0, t, hh:hh + 1, :]
            g2 = g_ref[0, 0, t, 2 * NSA_HPG + hh:2 * NSA_HPG + hh + 1, :]
            ocw_ref[0, 0, t, hh * NSA_DV:(hh + 1) * NSA_DV, :] = g0 * o_c[:, sl] + g2 * o_w[:, sl]

    def make_trip(clamped, jlim):
        def trip(i, _):
            live = [one_tile(i * PRE_TILES + u, u, clamped, jlim) for u in range(PRE_TILES)]
            while live:
                live = [g for g in live if next(g, "done") != "done"]
            return 0
        return trip

    step = tq * PRE_TILES
    n_trips = nq // PRE_TILES
    n_clamped = min(WINDOW // step, n_trips)
    n_norank = min(max((k_sel * SEL_LEN) // step, n_clamped), n_trips)
    cuts = sorted({0, n_clamped, n_norank, n_trips}
                  | {n_norank + (k * (n_trips - n_norank)) // RANK_CLASSES for k in range(RANK_CLASSES)})
    for lo, hi in zip(cuts[:-1], cuts[1:]):
        causal_blocks = -(-(hi * step) // SEL_LEN)
        jlim = 0 if hi <= n_norank else min(nblk, -(-causal_blocks // SUB) * SUB)
        lax.fori_loop(lo, hi, make_trip(lo < n_clamped, jlim), 0)


def _nsa_pre(qT, kw, vwT, kcmp, vcmpT, ovT, gates):
    B, H, nq, _, tq = qT.shape
    G = NSA_GROUPS
    S = kw.shape[2]
    nblk = S // SEL_LEN
    ncmp = kcmp.shape[2]
    assert TKV % tq == 0 and nq % PRE_TILES == 0 and PRE_TILES % (TKV // tq) == 0
    assert WINDOW % (tq * PRE_TILES) == 0 and WINDOW % TKV == 0 and S >= WINDOW + tq
    kern = functools.partial(_nsa_pre_kernel, k_sel=min(N_SELECT, nblk))
    per_bg = lambda shape: pl.BlockSpec((1, 1) + shape, lambda b, g: (b, g) + (0,) * len(shape))
    wide = NSA_HPG * tq
    return pl.pallas_call(
        kern,
        grid=(B, G),
        in_specs=[
            pl.BlockSpec((1, NSA_HPG, nq, NSA_HD, tq), lambda b, g: (b, g, 0, 0, 0)),
            per_bg((S, NSA_HD)),
            per_bg((S // TKV, NSA_DV + ONES, TKV)),
            per_bg((ncmp, NSA_HD)),
            per_bg((NSA_DV + ONES, ncmp)),
            pl.BlockSpec((nblk, ncmp), lambda b, g: (0, 0)),
            per_bg((nq, 16, tq)),
        ],
        out_specs=[per_bg((nq, nblk, tq)), per_bg((nq, NSA_HPG * NSA_DV, tq))],
        out_shape=[jax.ShapeDtypeStruct((B, G, nq, nblk, tq), BF16),
                   jax.ShapeDtypeStruct((B, G, nq, NSA_HPG * NSA_DV, tq), F32)],
        scratch_shapes=[pltpu.VMEM((ncmp, wide), F32), pltpu.VMEM((WINDOW + tq, wide), F32)] * PRE_TILES,
        compiler_params=_params(("parallel", "parallel")),
        name="nsa_branches",
    )(qT, kw, vwT, kcmp, vcmpT, ovT, gates)


def _nsa_sel_kernel(qT_ref, csel_ref, ks_ref, vsT_ref, ocw_ref, g_ref, sz_ref, o_ref,
                    *scratch):
    nq, _, tq = qT_ref.shape[2:]
    nblk = csel_ref.shape[3]
    nsub = TSTEP // TKV
    per = TQ_SEL // tq
    wide = NSA_HPG * TQ_SEL

    row = lax.broadcasted_iota(jnp.int32, (CHUNK, TQ_SEL), 0)
    col = lax.broadcasted_iota(jnp.int32, (CHUNK, TQ_SEL), 1)

    def qtile(sup, p):
        qaug_ref, s_a, s_b = scratch[3 * p:3 * p + 3]
        i = sup * (TSTEP // TQ_SEL) + p
        for hh in range(NSA_HPG):
            for u in range(per):
                lanes = slice(hh * TQ_SEL + u * tq, hh * TQ_SEL + (u + 1) * tq)
                qaug_ref[0:NSA_HD, lanes] = qT_ref[0, hh, i * per + u]
                qaug_ref[NSA_HD:NSA_HD + nblk, lanes] = csel_ref[0, 0, i * per + u]
        qaug = qaug_ref[...]

        def scores(j):
            kt = ks_ref[0, 0, j * TSTEP:(j + 1) * TSTEP, :]
            return _dot(kt, qaug)

        def vtile(j):
            return lambda r: vsT_ref[0, 0, j * nsub + r]

        own = p * TQ_SEL
        causal = {own + lo: functools.partial(
            lambda lo, blk: blk + _tile4(jnp.where(lo + row <= col, 0.0, NEG)), lo)
            for lo in range(0, TQ_SEL, CHUNK)}
        init = (jnp.full((1, wide), M_INIT, F32), jnp.zeros((NSA_DV + ONES, wide), F32))
        _, acc = yield from _sweep_steps(scores, vtile, causal, own + TQ_SEL, sup, init, s_a, s_b)
        o_s = acc[0:NSA_DV] * (1.0 / acc[NSA_DV:NSA_DV + 1])

        for hh in range(NSA_HPG):
            rows = slice(hh * NSA_DV, (hh + 1) * NSA_DV)
            for u in range(per):
                t = i * per + u
                lanes = slice(hh * TQ_SEL + u * tq, hh * TQ_SEL + (u + 1) * tq)
                g1 = g_ref[0, 0, t, NSA_HPG + hh:NSA_HPG + hh + 1, :]
                o = ocw_ref[0, 0, t, rows, :] + g1 * o_s[:, lanes]
                o_ref[0, t, rows, :] = (o * sz_ref[0, t, rows, :]).astype(BF16)

    once = jnp.minimum(pl.program_id(0), 0) + 1
    for sup in range((nq * tq) // TSTEP):
        lax.fori_loop(0, once, lambda _, c, sup=sup: (
            _round_robin(qtile(sup, p) for p in range(TSTEP // TQ_SEL)), c)[1], 0)


def _nsa_sel(qT, csel, ks, vsT, ocw, gates, sz):
    B, H, nq, _, tq = qT.shape
    G = NSA_GROUPS
    S = ks.shape[2]
    nblk = S // SEL_LEN
    assert TQ_SEL % tq == 0 and TSTEP % TQ_SEL == 0 and TSTEP % TKV == 0 and S % TSTEP == 0
    per_bg = lambda shape: pl.BlockSpec((1, 1) + shape, lambda b, g: (b, g) + (0,) * len(shape))
    gate = pl.BlockSpec((1, nq, NSA_HPG * NSA_DV, tq), lambda b, g: (b, 0, g, 0))
    return pl.pallas_call(
        _nsa_sel_kernel,
        grid=(B, G),
        in_specs=[
            pl.BlockSpec((1, NSA_HPG, nq, NSA_HD, tq), lambda b, g: (b, g, 0, 0, 0)),
            per_bg((nq, nblk, tq)),
            per_bg((S, NSA_HD + nblk)),
            per_bg((S // TKV, NSA_DV + ONES, TKV)),
            per_bg((nq, NSA_HPG * NSA_DV, tq)),
            per_bg((nq, 16, tq)),
            gate,
        ],
        out_specs=gate,
        out_shape=jax.ShapeDtypeStruct((B, nq, H * NSA_DV, tq), BF16),
        scratch_shapes=([pltpu.VMEM((NSA_HD + nblk, NSA_HPG * TQ_SEL), BF16)]
                        + [pltpu.VMEM((TSTEP, NSA_HPG * TQ_SEL), F32)] * 2) * (TSTEP // TQ_SEL),
        compiler_params=_params(("parallel", "parallel")),
        name="nsa_selected",
    )(qT, csel, ks, vsT, ocw, gates, sz)


def _nsa_perm():
    src = np.full((NSA_HD,), -1, np.int64)
    src[0:12] = np.arange(0, 12)
    src[12:16] = np.arange(24, 28)
    src[16:64] = np.arange(28, 76)
    src[64:76] = np.arange(12, 24)
    src[76:80] = np.arange(76, 80)
    src[80:96] = np.arange(80, 96)
    return src


def _take_cols(w, src):
    idx = np.where(src >= 0, src, 0)
    out = jnp.take(w, jnp.asarray(idx), axis=-1)
    return jnp.where(jnp.asarray(src >= 0), out, 0.0)


def _pad_last(w, width):
    return jnp.pad(w, [(0, 0)] * (w.ndim - 1) + [(0, width - w.shape[-1])])


def _prep_mla(w_in, g_cq, w_uq, g_ckv, w_ukv, g_q, g_k, w_out):
    D = w_in.shape[0]
    o1 = MLA_Q_LORA + MLA_KV_LORA
    z32 = jnp.zeros((D, 32), F32)
    wa = jnp.concatenate([w_in[:, :o1], w_in[:, o1:o1 + 32], z32, w_in[:, o1 + 32:o1 + 64], z32], axis=1)
    wq = w_uq.reshape(MLA_Q_LORA, MLA_HEADS, MLA_QK)
    zq = jnp.zeros((MLA_Q_LORA, MLA_HEADS, 32), F32)
    wq = jnp.concatenate([wq[:, :, 64:], wq[:, :, :32], zq, wq[:, :, 32:64], zq], axis=2)
    z1 = jnp.zeros((32,), F32)
    gq = jnp.concatenate([g_q[64:], g_q[:32], z1, g_q[32:64], z1]) * C_MLA
    wkv = w_ukv.reshape(MLA_KV_LORA, MLA_HEADS, MLA_NOPE + MLA_V)
    return {
        "wa": wa.astype(BF16),
        "wzT": w_in[:, o1 + MLA_ROPE:].T.astype(BF16),
        "gcq": g_cq.reshape(1, -1), "gckv": g_ckv.reshape(1, -1),
        "wuqT": wq.reshape(MLA_Q_LORA, MLA_HEADS * MLA_HD).T.astype(BF16),
        "gq": jnp.broadcast_to(gq[:, None], (MLA_HD, TT)),
        "wuk": wkv[:, :, :MLA_NOPE].reshape(MLA_KV_LORA, -1).astype(BF16),
        "wuvT": wkv[:, :, MLA_NOPE:].reshape(MLA_KV_LORA, -1).T.astype(BF16),
        "gkn": g_k[64:].reshape(1, -1),
        "gkp": jnp.concatenate([g_k[:32], z1, g_k[32:64], z1]).reshape(1, -1),
        "woT": w_out.T.astype(BF16),
    }


def _prep_nsa(w_in, g_q, g_k, pe_k, w1_k, w2_k, pe_v, w1_v, w2_v, w_out):
    D = w_in.shape[0]
    G = NSA_GROUPS
    src = _nsa_perm()
    offs = np.concatenate([[0], np.cumsum(NSA_SIZES)])
    part = lambda i: w_in[:, offs[i]:offs[i + 1]]
    q, kc, vc, ks, vs, kw, vw, gl, z = [part(i) for i in range(9)]
    perm_heads = lambda w, nh: _take_cols(w.reshape(D, nh, NSA_DK), src).reshape(D, nh * NSA_HD)
    pad_groups = lambda w, d: _pad_last(w.reshape(D, G, d), LANE).reshape(D, G * LANE)
    wtok = jnp.concatenate([perm_heads(ks, G), perm_heads(kw, G), pad_groups(kc, NSA_DK),
                            pad_groups(vc, NSA_DV)], axis=1)
    glr = gl.reshape(D, G, NSA_HPG, 3).transpose(0, 1, 3, 2).reshape(D, G, 3 * NSA_HPG)
    glr = _pad_last(glr, 16).reshape(D, G * 16)
    wf = jnp.concatenate([perm_heads(q, NSA_HEADS), vs, vw, glr, z], axis=1)
    halves = lambda w, d: _pad_last(w.reshape(2, CMP_LEN // 2, d, -1).transpose(0, 1, 3, 2), LANE) \
        .transpose(0, 1, 3, 2).reshape(2, (CMP_LEN // 2) * LANE, -1)
    pe_flat = lambda pe: _pad_last(pe, LANE).reshape(2, (CMP_LEN // 2) * LANE)
    return {
        "wtok": wtok.astype(BF16),
        "wfT": wf.T.astype(BF16),
        "gq": jnp.broadcast_to((_take_cols(g_q, src) * C_NSA)[:, None], (NSA_HD, TT)),
        "gk": _take_cols(g_k, src),
        "pek": pe_flat(pe_k), "pev": pe_flat(pe_v),
        "w1k": _pad_last(halves(w1_k, NSA_DK), LANE).astype(BF16),
        "w2k": _pad_last(_take_cols(w2_k, src).T, LANE).T.astype(BF16),
        "w1v": _pad_last(halves(w1_v, NSA_DV), LANE).astype(BF16),
        "w2vT": _pad_last(w2_v.T, LANE).astype(BF16),
        "woT": w_out.T.astype(BF16),
    }


def _overlap_T(S):
    ncmp = S // CMP_STRIDE
    nblk = S // SEL_LEN
    n_cmp = (S - CMP_LEN) // CMP_STRIDE + 1
    cs = np.arange(ncmp) * CMP_STRIDE
    ss = np.arange(nblk) * SEL_LEN
    ov = (cs[None, :] < ss[:, None] + SEL_LEN) & (cs[None, :] + CMP_LEN > ss[:, None])
    ov = ov & (np.arange(ncmp)[None, :] < n_cmp)
    return jnp.asarray(ov.astype(np.float32)).astype(BF16)


def _mla_layer(x, ng, w, tabs):
    qT, k, vT, sz = _mla_in(x, ng.reshape(1, -1), w, tabs)
    og = _mla_attn(qT, k, vT, sz)
    return _out_proj(og, w, x)


def _nsa_layer(x, ng, w, tabs, ovT):
    B, S, _ = x.shape
    qT, ks, kw, kc16, vc16, vsT, vwT, gates, sz = _nsa_in(x, ng.reshape(1, -1), w, tabs)
    kcmp, vcmpT = _nsa_cmp(kc16, vc16, w, tabs)
    csel, ocw = _nsa_pre(qT, kw, vwT, kcmp, vcmpT, ovT, gates)
    og = _nsa_sel(qT, csel, ks, vsT, ocw, gates, sz)
    return _out_proj(og, w, x)


def kernel(x, positions, norm_g, mla_w_in, mla_g_cq, mla_w_uq, mla_g_ckv, mla_w_ukv, mla_g_q, mla_g_k, mla_w_out, nsa_w_in, nsa_g_q, nsa_g_k, nsa_pe_k, nsa_w1_k, nsa_w2_k, nsa_pe_v, nsa_w1_v, nsa_w2_v, nsa_w_out):
    B, S, _ = x.shape
    tabs = _rope_tables(positions)
    ovT = _overlap_T(S)
    w_mla = jax.vmap(_prep_mla)(mla_w_in, mla_g_cq, mla_w_uq, mla_g_ckv, mla_w_ukv, mla_g_q,
                                mla_g_k, mla_w_out)
    w_nsa = jax.vmap(_prep_nsa)(nsa_w_in, nsa_g_q, nsa_g_k, nsa_pe_k, nsa_w1_k, nsa_w2_k,
                                nsa_pe_v, nsa_w1_v, nsa_w2_v, nsa_w_out)
    for i in range(DEPTH):
        j = i // N_MIXERS
        if i % N_MIXERS == 0:
            x = _mla_layer(x, norm_g[i], _LayerWeights(w_mla, j), tabs)
        else:
            x = _nsa_layer(x, norm_g[i], _LayerWeights(w_nsa, j), tabs, ovT)
    return x
```

```python
import functools

import numpy as np
import jax
import jax.numpy as jnp
from jax import lax
from jax.experimental import pallas as pl
from jax.experimental.pallas import tpu as pltpu

F32 = jnp.float32
BF16 = jnp.bfloat16

D_MODEL = 1024
DEPTH = 4
N_MIXERS = 2
ROPE_THETA = 500000.0
RMS_EPS = 1e-6
MLA_HEADS = 16
MLA_NOPE = 128
MLA_ROPE = 64
MLA_V = 128
MLA_QK = MLA_NOPE + MLA_ROPE
MLA_Q_LORA = 384
MLA_KV_LORA = 256
MLA_HD = 256
NSA_HEADS = 16
NSA_GROUPS = 4
NSA_HPG = NSA_HEADS // NSA_GROUPS
NSA_DK = 96
NSA_DV = 64
NSA_ROT = NSA_DK // 4
NSA_HALF = NSA_ROT // 2
NSA_HD = 128
CMP_LEN = 32
CMP_STRIDE = 16
SEL_LEN = 64
N_SELECT = 16
WINDOW = 512
SEL_FORCE = 1e4
NSA_SIZES = (NSA_HEADS * NSA_DK,
             NSA_GROUPS * NSA_DK, NSA_GROUPS * NSA_DV,
             NSA_GROUPS * NSA_DK, NSA_GROUPS * NSA_DV,
             NSA_GROUPS * NSA_DK, NSA_GROUPS * NSA_DV,
             3 * NSA_HEADS, NSA_HEADS * NSA_DV)

LANE = 128
SUB = 8
TT = 256
TO = 512
TKV = 256
TSTEP = 512
TQ_MLA = 512
MLA_HEADS_PER_STEP = 2
TQ_NSA = 128
TQ_SEL = 256
PRE_TILES = 4
RANK_CLASSES = 3
CHUNK = 64
ONES = 16
NEG = -30000.0
M_INIT = -1e30
LOG2E = 1.4426950408889634
C_MLA = (MLA_QK ** -0.5) * LOG2E
C_NSA = (NSA_DK ** -0.5) * LOG2E
VMEM_LIMIT = 56 * 1024 * 1024


def _dot(a, b):
    return jnp.dot(a, b, preferred_element_type=F32)


def _dot_nt(a, b):
    return lax.dot_general(a, b, (((1,), (1,)), ((), ())), preferred_element_type=F32)


def _sigmoid(x):
    return 1.0 / (1.0 + jnp.exp(-x))


def _rms_rows(x, g):
    ms = jnp.mean(x * x, axis=-1, keepdims=True)
    return x * lax.rsqrt(ms + RMS_EPS) * g


def _params(sem):
    return pltpu.CompilerParams(dimension_semantics=sem, vmem_limit_bytes=VMEM_LIMIT)


class _LayerWeights:
    def __init__(self, stack, layer):
        self.stack, self.layer = stack, layer

    def __getitem__(self, name):
        return self.stack[name]

    def spec(self, name):
        shape = self.stack[name].shape[1:]
        layer = self.layer
        return pl.BlockSpec((None,) + shape, lambda *_: (layer,) + (0,) * len(shape))


def _rope_kernel(pos_ref, invf_ref, cm_ref, sm_ref, cn_ref, sn_ref, ctm_ref, stm_ref, ctn_ref, stn_ref):
    ang = invf_ref[...] * pos_ref[...]
    c, s = jnp.cos(ang), jnp.sin(ang)
    hm = MLA_ROPE // 2
    cm, sm, cn, sn = c[:hm], s[:hm], c[hm:hm + 16], s[hm:hm + 16]
    cm_ref[...], sm_ref[...], cn_ref[...], sn_ref[...] = cm, sm, cn, sn
    tn = ang.shape[1]
    one = lambda r: jnp.ones((r, tn), F32)
    zero = lambda r: jnp.zeros((r, tn), F32)
    ctm_ref[...] = jnp.concatenate([cm, one(32), cm, one(32)], axis=0).T
    stm_ref[...] = jnp.concatenate([-sm, zero(32), sm, zero(32)], axis=0).T
    ctn_ref[...] = jnp.concatenate([cn, one(48), cn, one(48)], axis=0).T
    stn_ref[...] = jnp.concatenate([-sn, zero(48), sn, zero(48)], axis=0).T


def _rope_tables(positions):
    B, S = positions.shape
    n = positions.size
    tn = 512
    pos = positions.reshape(1, n).astype(F32)
    half_m = MLA_ROPE // 2
    inv_m = ROPE_THETA ** (-jnp.arange(half_m, dtype=F32) / half_m)
    inv_n = ROPE_THETA ** (-jnp.arange(NSA_HALF, dtype=F32) / NSA_HALF)
    invf = jnp.concatenate([inv_m, inv_n, jnp.zeros((16 - NSA_HALF,), F32)])
    rows = invf.shape[0]
    invf = jnp.broadcast_to(invf[:, None], (rows, tn))
    feat = lambda r: pl.BlockSpec((r, tn), lambda i: (0, i))
    tok = pl.BlockSpec((tn, LANE), lambda i: (i, 0))
    cm, sm, cn, sn, ctm, stm, ctn, stn = pl.pallas_call(
        _rope_kernel,
        grid=(n // tn,),
        in_specs=[pl.BlockSpec((1, tn), lambda i: (0, i)),
                  pl.BlockSpec((rows, tn), lambda i: (0, 0))],
        out_specs=[feat(half_m), feat(half_m), feat(16), feat(16), tok, tok, tok, tok],
        out_shape=[jax.ShapeDtypeStruct((half_m, n), F32)] * 2
        + [jax.ShapeDtypeStruct((16, n), F32)] * 2 + [jax.ShapeDtypeStruct((n, LANE), F32)] * 4,
        compiler_params=_params(("parallel",)),
        name="rope_tables",
    )(pos, invf)
    ncmp = S // CMP_STRIDE
    last = CMP_LEN - 1
    pick = lambda t: jnp.pad(t.reshape(B, S, LANE)[:, last::CMP_STRIDE],
                             ((0, 0), (0, ncmp - (S - last + CMP_STRIDE - 1) // CMP_STRIDE), (0, 0)))
    return {"cosT_mla": cm, "sinT_mla": sm, "cosT_nsa": cn, "sinT_nsa": sn,
            "ct_mla": ctm, "st_mla": stm, "ct_nsa": ctn, "st_nsa": stn,
            "ct_cmp": pick(ctn), "st_cmp": pick(stn)}


def _mla_in_kernel(x_ref, ng_ref, wa_ref, wzT_ref, gcq_ref, gckv_ref, wuqT_ref, gq_ref,
                   wuk_ref, wuvT_ref, gkn_ref, gkp_ref, ct_ref, st_ref, cosT_ref, sinT_ref,
                   qT_ref, k_ref, vT_ref, sz_ref):
    x = x_ref[0]
    n = x.shape[0]
    h = _rms_rows(x, ng_ref[...]).astype(BF16)
    pa = _dot(h, wa_ref[...])
    z = _dot_nt(wzT_ref[...], h)
    sz_ref[0, 0] = z * _sigmoid(z)

    cqn = _rms_rows(pa[:, :MLA_Q_LORA], gcq_ref[...]).astype(BF16)
    ckvn = _rms_rows(pa[:, MLA_Q_LORA:MLA_Q_LORA + MLA_KV_LORA], gckv_ref[...]).astype(BF16)
    kpe = pa[:, MLA_Q_LORA + MLA_KV_LORA:]

    qa = _dot_nt(wuqT_ref[...], cqn)
    cos = cosT_ref[...]
    sin = sinT_ref[...]
    gq = gq_ref[...]
    zeros32 = jnp.zeros((32, n), BF16)
    for hd in range(MLA_HEADS):
        blk = qa[hd * MLA_HD:(hd + 1) * MLA_HD]
        ss = jnp.sum(blk * blk, axis=0, keepdims=True)
        qn = blk * lax.rsqrt(ss * (1.0 / MLA_QK) + RMS_EPS) * gq
        x1 = qn[128:160]
        x2 = qn[192:224]
        qT_ref[0, hd, 0, 0:128, :] = qn[0:128].astype(BF16)
        qT_ref[0, hd, 0, 128:160, :] = (x1 * cos - x2 * sin).astype(BF16)
        qT_ref[0, hd, 0, 160:192, :] = zeros32
        qT_ref[0, hd, 0, 192:224, :] = (x2 * cos + x1 * sin).astype(BF16)
        qT_ref[0, hd, 0, 224:256, :] = zeros32

    kn = _dot(ckvn, wuk_ref[...])
    ss_pe = jnp.sum(kpe * kpe, axis=-1, keepdims=True)
    kpg = kpe * gkp_ref[...]
    prot = kpg * ct_ref[...] + pltpu.roll(kpg, 64, axis=1) * st_ref[...]
    gkn = gkn_ref[...]
    for hd in range(MLA_HEADS):
        kb = kn[:, hd * MLA_NOPE:(hd + 1) * MLA_NOPE]
        ss = jnp.sum(kb * kb, axis=-1, keepdims=True) + ss_pe
        r = lax.rsqrt(ss * (1.0 / MLA_QK) + RMS_EPS)
        k_ref[0, hd, :, 0:128] = (kb * r * gkn).astype(BF16)
        k_ref[0, hd, :, 128:256] = (prot * r).astype(BF16)

    va = _dot_nt(wuvT_ref[...], ckvn)
    ones = jnp.ones((ONES, n), BF16)
    for hd in range(MLA_HEADS):
        vT_ref[0, hd, 0, 0:MLA_V, :] = va[hd * MLA_V:(hd + 1) * MLA_V].astype(BF16)
        vT_ref[0, hd, 0, MLA_V:MLA_V + ONES, :] = ones


def _mla_in(x, ng, w, tabs):
    B, S, D = x.shape
    nt = S // TT
    r = TQ_MLA // TT
    assert TT == TKV and TQ_MLA % TT == 0
    full = lambda shape: pl.BlockSpec(shape, lambda b, t: (0,) * len(shape))
    tok = lambda b, t: (b * nt + t, 0)
    feat = lambda b, t: (0, b * nt + t)
    return pl.pallas_call(
        _mla_in_kernel,
        grid=(B, nt),
        in_specs=[
            pl.BlockSpec((1, TT, D), lambda b, t: (b, t, 0)),
            full((1, D)),
            w.spec("wa"), w.spec("wzT"), w.spec("gcq"), w.spec("gckv"),
            w.spec("wuqT"), w.spec("gq"), w.spec("wuk"), w.spec("wuvT"),
            w.spec("gkn"), w.spec("gkp"),
            pl.BlockSpec((TT, LANE), tok), pl.BlockSpec((TT, LANE), tok),
            pl.BlockSpec((32, TT), feat), pl.BlockSpec((32, TT), feat),
        ],
        out_specs=[
            pl.BlockSpec((1, MLA_HEADS, 1, MLA_HD, TT), lambda b, t: (b, 0, t // r, 0, t % r)),
            pl.BlockSpec((1, MLA_HEADS, TT, MLA_HD), lambda b, t: (b, 0, t, 0)),
            pl.BlockSpec((1, MLA_HEADS, 1, MLA_V + ONES, TT), lambda b, t: (b, 0, t, 0, 0)),
            pl.BlockSpec((1, 1, MLA_HEADS * MLA_V, TT), lambda b, t: (b, t // r, 0, t % r)),
        ],
        out_shape=[
            jax.ShapeDtypeStruct((B, MLA_HEADS, S // TQ_MLA, MLA_HD, TQ_MLA), BF16),
            jax.ShapeDtypeStruct((B, MLA_HEADS, S, MLA_HD), BF16),
            jax.ShapeDtypeStruct((B, MLA_HEADS, nt, MLA_V + ONES, TT), BF16),
            jax.ShapeDtypeStruct((B, S // TQ_MLA, MLA_HEADS * MLA_V, TQ_MLA), F32),
        ],
        compiler_params=_params(("parallel", "parallel")),
        name="mla_in_proj",
    )(x, ng, w["wa"], w["wzT"], w["gcq"], w["gckv"], w["wuqT"], w["gq"], w["wuk"],
      w["wuvT"], w["gkn"], w["gkp"], tabs["ct_mla"], tabs["st_mla"], tabs["cosT_mla"],
      tabs["sinT_mla"])


def _softmax_step_ref(s_ref, carry, vtile=None, segs=None, bias=None):
    m, acc = carry
    n, w = s_ref.shape
    segs = segs or [(t * TKV, (t + 1) * TKV, functools.partial(vtile, t)) for t in range(n // TKV)]
    bias = bias or {}
    fold = lambda a, op: op(a.reshape(a.shape[0] // SUB, SUB, w), axis=0)
    mx = None
    for lo in range(0, n, CHUNK):
        blk = s_ref[lo:lo + CHUNK, :]
        if lo in bias:
            blk = bias[lo](blk) if callable(bias[lo]) else blk + bias[lo]
            s_ref[lo:lo + CHUNK, :] = blk
        mx = fold(blk, jnp.max) if mx is None else jnp.maximum(mx, fold(blk, jnp.max))
    m_new = jnp.maximum(m, jnp.max(mx, axis=0, keepdims=True))
    alpha = jnp.exp2(m - m_new)
    pv = None
    for seg_lo, seg_hi, value_tile in segs:
        parts = [jnp.exp2(s_ref[lo:lo + CHUNK, :] - m_new).astype(BF16)
                 for lo in range(seg_lo, seg_hi, CHUNK)]
        d = _dot(value_tile(), jnp.concatenate(parts, axis=0))
        pv = d if pv is None else pv + d
    return m_new, alpha * acc + pv


def _sweep_steps(scores, vtile, last_bias, last_rows, n_full, carry, s_a, s_b):
    bufs = (s_a, s_b)
    s_a[...] = scores(0)
    for j in range(n_full):
        bufs[(j + 1) % 2][...] = scores(j + 1)
        carry = _softmax_step_ref(bufs[j % 2], carry, vtile(j))
        yield
    src = bufs[n_full % 2]
    return _softmax_step_ref(src.at[0:last_rows], carry, vtile(n_full), bias=last_bias)


def _round_robin(generators):
    live = list(generators)
    while live:
        live = [g for g in live if next(g, "done") != "done"]


def _mla_attn_kernel(qT_ref, k_ref, vT_ref, sz_ref, o_ref, *scratch):
    nh, nq, _, tq = qT_ref.shape[1:]
    nsub = TSTEP // TKV

    row = lax.broadcasted_iota(jnp.int32, (CHUNK, tq), 0)
    col = lax.broadcasted_iota(jnp.int32, (CHUNK, tq), 1)
    causal = {lo: functools.partial(lambda lo, blk: jnp.where(lo + row <= col, blk, NEG), lo)
              for lo in range(0, TSTEP, CHUNK)}

    def qtile(h, i):
        s_a, s_b = scratch[2 * h], scratch[2 * h + 1]
        rows = slice(h * MLA_V, (h + 1) * MLA_V)
        q = qT_ref[0, h, i]
        scores = lambda j: _dot(k_ref[0, h, j * TSTEP:(j + 1) * TSTEP, :], q)
        vtile = lambda j: (lambda r: vT_ref[0, h, j * nsub + r])
        init = (jnp.full((1, tq), M_INIT, F32), jnp.zeros((MLA_V + ONES, tq), F32))
        _, acc = yield from _sweep_steps(scores, vtile, causal, TSTEP, i, init, s_a, s_b)
        o = acc[0:MLA_V] / acc[MLA_V:MLA_V + 1]
        o_ref[0, i, rows, :] = (o * sz_ref[0, i, rows, :]).astype(BF16)

    once = jnp.minimum(pl.program_id(0), 0) + 1
    for i in range(nq):
        lax.fori_loop(0, once, lambda _, c, i=i: (_round_robin(qtile(h, i) for h in range(nh)), c)[1], 0)


def _mla_attn(qT, k, vT, sz):
    B, H, nq, _, tq = qT.shape
    S = k.shape[2]
    nh = MLA_HEADS_PER_STEP
    assert tq == TSTEP and TSTEP % TKV == 0 and H % nh == 0
    per_bh = lambda shape: pl.BlockSpec((1, nh) + shape, lambda b, h: (b, h) + (0,) * len(shape))
    gate = pl.BlockSpec((1, nq, nh * MLA_V, tq), lambda b, h: (b, 0, h, 0))
    return pl.pallas_call(
        _mla_attn_kernel,
        grid=(B, H // nh),
        in_specs=[per_bh((nq, MLA_HD, tq)), per_bh((S, MLA_HD)),
                  per_bh((S // TKV, MLA_V + ONES, TKV)), gate],
        out_specs=gate,
        out_shape=jax.ShapeDtypeStruct((B, nq, H * MLA_V, tq), BF16),
        scratch_shapes=[pltpu.VMEM((TSTEP, tq), F32)] * (2 * nh),
        compiler_params=_params(("parallel", "parallel")),
        name="mla_attention",
    )(qT, k, vT, sz)


def _out_proj_kernel(og_ref, wT_ref, x_ref, o_ref):
    ntile = og_ref.shape[1]
    og = og_ref[0, 0] if ntile == 1 else jnp.concatenate([og_ref[0, r] for r in range(ntile)], axis=1)
    yT = _dot(wT_ref[...], og)
    o_ref[0] = x_ref[0] + yT.T


def _out_proj(og, w, x):
    B, S, D = x.shape
    _, _, K, W = og.shape
    r = TO // W
    wT = w["woT"]
    return pl.pallas_call(
        _out_proj_kernel,
        grid=(B, S // TO),
        in_specs=[
            pl.BlockSpec((1, r, K, W), lambda b, t: (b, t, 0, 0)),
            w.spec("woT"),
            pl.BlockSpec((1, TO, D), lambda b, t: (b, t, 0)),
        ],
        out_specs=pl.BlockSpec((1, TO, D), lambda b, t: (b, t, 0)),
        out_shape=jax.ShapeDtypeStruct((B, S, D), F32),
        compiler_params=_params(("parallel", "parallel")),
        name="out_proj",
    )(og, wT, x)


def _nsa_in_kernel(x_ref, ng_ref, wtok_ref, wfT_ref, gq_ref, gk_ref, ct_ref, st_ref,
                   cosT_ref, sinT_ref,
                   qT_ref, ks_ref, kw_ref, kc_ref, vc_ref, vsT_ref, vwT_ref, g_ref, sz_ref, cv_scr):
    t = pl.program_id(1)
    x = x_ref[0]
    n = x.shape[0]
    tq = qT_ref.shape[-1]
    h = _rms_rows(x, ng_ref[...]).astype(BF16)
    G = NSA_GROUPS
    lanes = [slice(u * tq, (u + 1) * tq) for u in range(n // tq)]

    fa = _dot_nt(wfT_ref[...], h)
    cos = cosT_ref[...]
    sin = sinT_ref[...]
    gq = gq_ref[...]
    for hd in range(NSA_HEADS):
        blk = fa[hd * NSA_HD:(hd + 1) * NSA_HD]
        ss = jnp.sum(blk * blk, axis=0, keepdims=True)
        qn = blk * lax.rsqrt(ss * (1.0 / NSA_DK) + RMS_EPS) * gq
        x1 = qn[0:16]
        x2 = qn[64:80]
        r1 = (x1 * cos - x2 * sin).astype(BF16)
        r2 = (x2 * cos + x1 * sin).astype(BF16)
        qb = qn.astype(BF16)
        for u, sl in enumerate(lanes):
            qT_ref[0, hd, u, 0:16, :] = r1[:, sl]
            qT_ref[0, hd, u, 16:64, :] = qb[16:64, sl]
            qT_ref[0, hd, u, 64:80, :] = r2[:, sl]
            qT_ref[0, hd, u, 80:128, :] = qb[80:128, sl]
    off = NSA_HEADS * NSA_HD
    ones = jnp.ones((ONES, n), BF16)
    for ref in (vsT_ref, vwT_ref):
        for g in range(G):
            ref[0, g, 0, 0:NSA_DV, :] = fa[off + g * NSA_DV:off + (g + 1) * NSA_DV].astype(BF16)
            ref[0, g, 0, NSA_DV:NSA_DV + ONES, :] = ones
        off += G * NSA_DV
    sg = _sigmoid(fa[off:off + 16 * G])
    for g in range(G):
        for u, sl in enumerate(lanes):
            g_ref[0, g, u] = sg[g * 16:(g + 1) * 16, sl]
    off += 16 * G
    z = fa[off:off + NSA_HEADS * NSA_DV]
    sz = z * _sigmoid(z)
    for u, sl in enumerate(lanes):
        sz_ref[0, u] = sz[:, sl]

    pt = _dot(h, wtok_ref[...])
    ct = ct_ref[...]
    st = st_ref[...]
    nblk = ks_ref.shape[-1] - NSA_HD
    row = lax.broadcasted_iota(jnp.int32, (n, nblk), 0) + t * n
    col = lax.broadcasted_iota(jnp.int32, (n, nblk), 1)
    onehot = jnp.where((row // SEL_LEN) == col, 1.0, 0.0).astype(BF16)
    for br, ref in ((0, ks_ref), (1, kw_ref)):
        gk = gk_ref[br + 1:br + 2, :]
        for g in range(G):
            kb = pt[:, (br * G + g) * NSA_HD:(br * G + g + 1) * NSA_HD]
            ss = jnp.sum(kb * kb, axis=-1, keepdims=True)
            kn = kb * lax.rsqrt(ss * (1.0 / NSA_DK) + RMS_EPS) * gk
            kr = kn * ct + pltpu.roll(kn, 64, axis=1) * st
            ref[0, g, :, 0:NSA_HD] = kr.astype(BF16)
            if br == 0:
                ref[0, g, :, NSA_HD:NSA_HD + nblk] = onehot
    for br, ref in ((2, kc_ref), (3, vc_ref)):
        for g in range(G):
            stage = cv_scr.at[(br - 2) * G + g]
            stage[...] = pt[:, (br * G + g) * LANE:(br * G + g + 1) * LANE]
            for l in range(CMP_STRIDE):
                ref[0, g, :, l * LANE:(l + 1) * LANE] = stage[pl.ds(l, n // CMP_STRIDE, stride=CMP_STRIDE), :]


def _nsa_in(x, ng, w, tabs):
    B, S, D = x.shape
    nt = S // TT
    G = NSA_GROUPS
    nblk = S // SEL_LEN
    r = TT // TQ_NSA
    nq = S // TQ_NSA
    assert TT == TKV and TT % TQ_NSA == 0
    full = lambda shape: pl.BlockSpec(shape, lambda b, t: (0,) * len(shape))
    tok = lambda b, t: (b * nt + t, 0)
    feat = lambda b, t: (0, b * nt + t)
    tokmaj = lambda width: pl.BlockSpec((1, G, TT, width), lambda b, t: (b, 0, t, 0))
    vtile = pl.BlockSpec((1, G, 1, NSA_DV + ONES, TT), lambda b, t: (b, 0, t, 0, 0))
    cmp_in = pl.BlockSpec((1, G, TT // CMP_STRIDE, CMP_STRIDE * LANE), lambda b, t: (b, 0, t, 0))
    vshape = jax.ShapeDtypeStruct((B, G, nt, NSA_DV + ONES, TT), BF16)
    return pl.pallas_call(
        _nsa_in_kernel,
        grid=(B, nt),
        in_specs=[
            pl.BlockSpec((1, TT, D), lambda b, t: (b, t, 0)),
            full((1, D)),
            w.spec("wtok"), w.spec("wfT"), w.spec("gq"), w.spec("gk"),
            pl.BlockSpec((TT, LANE), tok), pl.BlockSpec((TT, LANE), tok),
            pl.BlockSpec((16, TT), feat), pl.BlockSpec((16, TT), feat),
        ],
        out_specs=[
            pl.BlockSpec((1, NSA_HEADS, r, NSA_HD, TQ_NSA), lambda b, t: (b, 0, t, 0, 0)),
            tokmaj(NSA_HD + nblk), tokmaj(NSA_HD), cmp_in, cmp_in,
            vtile, vtile,
            pl.BlockSpec((1, G, r, 16, TQ_NSA), lambda b, t: (b, 0, t, 0, 0)),
            pl.BlockSpec((1, r, NSA_HEADS * NSA_DV, TQ_NSA), lambda b, t: (b, t, 0, 0)),
        ],
        out_shape=[
            jax.ShapeDtypeStruct((B, NSA_HEADS, nq, NSA_HD, TQ_NSA), BF16),
            jax.ShapeDtypeStruct((B, G, S, NSA_HD + nblk), BF16),
            jax.ShapeDtypeStruct((B, G, S, NSA_HD), BF16),
            jax.ShapeDtypeStruct((B, G, S // CMP_STRIDE, CMP_STRIDE * LANE), F32),
            jax.ShapeDtypeStruct((B, G, S // CMP_STRIDE, CMP_STRIDE * LANE), F32),
            vshape, vshape,
            jax.ShapeDtypeStruct((B, G, nq, 16, TQ_NSA), F32),
            jax.ShapeDtypeStruct((B, nq, NSA_HEADS * NSA_DV, TQ_NSA), F32),
        ],
        scratch_shapes=[pltpu.VMEM((2 * G, TT, LANE), F32)],
        compiler_params=_params(("parallel", "parallel")),
        name="nsa_in_proj",
    )(x, ng, w["wtok"], w["wfT"], w["gq"], w["gk"], tabs["ct_nsa"], tabs["st_nsa"],
      tabs["cosT_nsa"], tabs["sinT_nsa"])


def _nsa_cmp_kernel(kc_ref, vc_ref, pek_ref, pev_ref, w1k_ref, w2k_ref, w1v_ref, w2vT_ref,
                    gk_ref, ct_ref, st_ref, kcmp_ref, vcmpT_ref):
    nrow = kc_ref.shape[2]

    def pre(x, pe_ref, w1_ref):
        lo = _dot((x + pe_ref[0:1, :]).astype(BF16), w1_ref[0])
        hi = _dot((x + pe_ref[1:2, :]).astype(BF16), w1_ref[1])
        return lo + pltpu.roll(hi, nrow - 1, axis=0)

    a = pre(kc_ref[0, 0], pek_ref, w1k_ref)
    kc = _dot((a * _sigmoid(a)).astype(BF16), w2k_ref[...])
    ss = jnp.sum(kc * kc, axis=-1, keepdims=True)
    kn = kc * lax.rsqrt(ss * (1.0 / NSA_DK) + RMS_EPS) * gk_ref[0:1, :]
    kr = kn * ct_ref[0] + pltpu.roll(kn, 64, axis=1) * st_ref[0]
    kcmp_ref[0, 0] = kr.astype(BF16)

    a = pre(vc_ref[0, 0], pev_ref, w1v_ref)
    sv = (a * _sigmoid(a)).astype(BF16)
    vcmpT_ref[0, 0, 0:NSA_DV, :] = _dot_nt(w2vT_ref[...], sv).astype(BF16)
    vcmpT_ref[0, 0, NSA_DV:NSA_DV + ONES, :] = jnp.ones((ONES, nrow), BF16)


def _nsa_cmp(kc16, vc16, w, tabs):
    B, G, nrow, width = kc16.shape
    blk = pl.BlockSpec((1, 1, nrow, width), lambda b, g: (b, g, 0, 0))
    tab = pl.BlockSpec((1, nrow, LANE), lambda b, g: (b, 0, 0))
    return pl.pallas_call(
        _nsa_cmp_kernel,
        grid=(B, G),
        in_specs=[blk, blk, w.spec("pek"), w.spec("pev"), w.spec("w1k"), w.spec("w2k"),
                  w.spec("w1v"), w.spec("w2vT"), w.spec("gk"), tab, tab],
        out_specs=[pl.BlockSpec((1, 1, nrow, NSA_HD), lambda b, g: (b, g, 0, 0)),
                   pl.BlockSpec((1, 1, NSA_DV + ONES, nrow), lambda b, g: (b, g, 0, 0))],
        out_shape=[jax.ShapeDtypeStruct((B, G, nrow, NSA_HD), BF16),
                   jax.ShapeDtypeStruct((B, G, NSA_DV + ONES, nrow), BF16)],
        compiler_params=_params(("parallel", "parallel")),
        name="nsa_compress",
    )(kc16, vc16, w["pek"], w["pev"], w["w1k"], w["w2k"], w["w1v"], w["w2vT"], w["gk"],
      tabs["ct_cmp"], tabs["st_cmp"])


def _tile4(a):
    return jnp.concatenate([a] * NSA_HPG, axis=1)


def _cmp_exp(s_ref, bias):
    n, w = s_ref.shape
    fold = lambda a, op: op(a.reshape(a.shape[0] // SUB, SUB, w), axis=0)
    rows = [slice(lo, lo + CHUNK) for lo in range(0, n, CHUNK)]
    mx = None
    for sl in rows:
        blk = bias[sl.start](s_ref[sl, :])
        s_ref[sl, :] = blk
        mx = fold(blk, jnp.max) if mx is None else jnp.maximum(mx, fold(blk, jnp.max))
    mc = jnp.maximum(jnp.max(mx, axis=0, keepdims=True), 0.5 * NEG)
    return jnp.concatenate([jnp.exp2(s_ref[sl, :] - mc).astype(BF16) for sl in rows], axis=0)


def _nsa_pre_kernel(qT_ref, kw_ref, vwT_ref, kc_ref, vcT_ref, ovT_ref, g_ref,
                    csel_ref, ocw_ref, *scratch, k_sel):
    nq, _, tq = qT_ref.shape[2:]
    ncmp = kc_ref.shape[2]
    nblk = ovT_ref.shape[0]

    wk = WINDOW + tq
    rloc = lax.broadcasted_iota(jnp.int32, (CHUNK, tq), 0)
    cloc = lax.broadcasted_iota(jnp.int32, (CHUNK, tq), 1)
    mask = lambda ok, lo: (lambda blk: blk + _tile4(jnp.where(ok(lo), 0.0, NEG)))

    def one_tile(t, u, clamped, jlim):
        q0 = t * tq
        qT = jnp.concatenate([qT_ref[0, hh, t] for hh in range(NSA_HPG)], axis=1)
        tok = q0 + lax.broadcasted_iota(jnp.int32, (1, tq), 1)

        sc_scr, sw_scr = scratch[2 * u], scratch[2 * u + 1]
        sc_scr[...] = _dot(kc_ref[0, 0], qT)
        yield
        cvalid = lambda lo: ((lo + rloc) * CMP_STRIDE + (CMP_LEN - 1)) <= tok
        cbias = {lo: mask(cvalid, lo) for lo in range(0, ncmp, CHUNK)}
        ec = _cmp_exp(sc_scr, cbias)
        ac = _dot(vcT_ref[0, 0], ec)
        dc = ac[NSA_DV:NSA_DV + 1]
        inv = 1.0 / jnp.where(dc > 0, dc, 1.0)
        o_c = ac[0:NSA_DV] * inv

        if clamped:
            start, off = 0, 0
            wbias = {lo: mask(lambda lo: lo + rloc <= tok, lo) for lo in range(0, wk, CHUNK)}
        else:
            start, off = q0 - WINDOW, (u % (TKV // tq)) * tq
            wbias = {lo: mask(lambda lo: lo + rloc > cloc, lo) for lo in range(0, tq, CHUNK)}
            wbias.update({lo: mask(lambda lo: lo - WINDOW + rloc <= cloc, lo)
                          for lo in range(WINDOW, wk, CHUNK)})
        sw_scr[...] = _dot(kw_ref[0, 0, pl.ds(pl.multiple_of(start, tq), wk), :], qT)
        yield
        w0 = start // TKV
        segs, pos, tile = [], 0, 0
        while pos < wk:
            width = min(TKV - off, wk - pos)
            segs.append((pos, pos + width, functools.partial(
                lambda tile, off, width: vwT_ref[0, 0, w0 + tile, :, off:off + width], tile, off, width)))
            pos, tile, off = pos + width, tile + 1, 0
        init = (jnp.full((1, NSA_HPG * tq), M_INIT, F32),
                jnp.zeros((NSA_DV + ONES, NSA_HPG * tq), F32))
        _, aw = _softmax_step_ref(sw_scr, init, segs=segs, bias=wbias)
        o_w = aw[0:NSA_DV] * (1.0 / aw[NSA_DV:NSA_DV + 1])
        yield

        if jlim:
            iw = _dot(ovT_ref[0:jlim, :], ec) * inv
            imp = iw[:, 0:tq]
            for hh in range(1, NSA_HPG):
                imp = imp + iw[:, hh * tq:(hh + 1) * tq]
            bidx = lax.broadcasted_iota(jnp.int32, (jlim, tq), 0)
            cur = tok // SEL_LEN
            forced = (bidx == 0) | (bidx == cur) | (bidx == cur - 1)
            valid = (bidx * SEL_LEN) <= tok
            score = jnp.where(forced, SEL_FORCE, jnp.where(valid, imp, -1.0))
            grp = [score[r * SUB:(r + 1) * SUB] for r in range(jlim // SUB)]
            bsub = lax.broadcasted_iota(jnp.int32, (SUB, tq), 0)
            cnt = [jnp.zeros((SUB, tq), F32) for _ in grp]
            for j in range(jlim):
                rowj = score[j:j + 1, :]
                for r in range(jlim // SUB):
                    if r * SUB > j:
                        cnt[r] = cnt[r] + jnp.where(rowj >= grp[r], 1.0, 0.0)
                    elif r * SUB + SUB - 1 < j:
                        cnt[r] = cnt[r] + jnp.where(rowj > grp[r], 1.0, 0.0)
                    else:
                        tie = jnp.where(bsub + r * SUB > j, 1.0, 0.0)
                        cnt[r] = cnt[r] + jnp.where(rowj > grp[r], 1.0, 0.0)
                        cnt[r] = cnt[r] + jnp.where(rowj == grp[r], tie, 0.0)
            cnt = jnp.concatenate(cnt, axis=0)
            csel_ref[0, 0, t, 0:jlim, :] = jnp.where(cnt < k_sel, 0.0, NEG).astype(BF16)
        if jlim < nblk:
            csel_ref[0, 0, t, jlim:nblk, :] = jnp.zeros((nblk - jlim, tq), BF16)

        for hh in range(NSA_HPG):
            sl = slice(hh * tq, (hh + 1) * tq)
            g0 = g_ref[0, 0, t, hh:hh + 1, :]
            g2 = g_ref[0, 0, t, 2 * NSA_HPG + hh:2 * NSA_HPG + hh + 1, :]
            ocw_ref[0, 0, t, hh * NSA_DV:(hh + 1) * NSA_DV, :] = g0 * o_c[:, sl] + g2 * o_w[:, sl]

    def make_trip(clamped, jlim):
        def trip(i, _):
            _round_robin(one_tile(i * PRE_TILES + u, u, clamped, jlim) for u in range(PRE_TILES))
            return 0
        return trip

    step = tq * PRE_TILES
    n_trips = nq // PRE_TILES
    n_clamped = min(WINDOW // step, n_trips)
    n_norank = min(max((k_sel * SEL_LEN) // step, n_clamped), n_trips)
    cuts = sorted({0, n_clamped, n_norank, n_trips}
                  | {n_norank + (k * (n_trips - n_norank)) // RANK_CLASSES for k in range(RANK_CLASSES)})
    for lo, hi in zip(cuts[:-1], cuts[1:]):
        causal_blocks = -(-(hi * step) // SEL_LEN)
        jlim = 0 if hi <= n_norank else min(nblk, -(-causal_blocks // SUB) * SUB)
        lax.fori_loop(lo, hi, make_trip(lo < n_clamped, jlim), 0)


def _nsa_pre(qT, kw, vwT, kcmp, vcmpT, ovT, gates):
    B, H, nq, _, tq = qT.shape
    G = NSA_GROUPS
    S = kw.shape[2]
    nblk = S // SEL_LEN
    ncmp = kcmp.shape[2]
    assert TKV % tq == 0 and nq % PRE_TILES == 0 and PRE_TILES % (TKV // tq) == 0
    assert WINDOW % (tq * PRE_TILES) == 0 and WINDOW % TKV == 0 and S >= WINDOW + tq
    kern = functools.partial(_nsa_pre_kernel, k_sel=min(N_SELECT, nblk))
    per_bg = lambda shape: pl.BlockSpec((1, 1) + shape, lambda b, g: (b, g) + (0,) * len(shape))
    wide = NSA_HPG * tq
    return pl.pallas_call(
        kern,
        grid=(B, G),
        in_specs=[
            pl.BlockSpec((1, NSA_HPG, nq, NSA_HD, tq), lambda b, g: (b, g, 0, 0, 0)),
            per_bg((S, NSA_HD)),
            per_bg((S // TKV, NSA_DV + ONES, TKV)),
            per_bg((ncmp, NSA_HD)),
            per_bg((NSA_DV + ONES, ncmp)),
            pl.BlockSpec((nblk, ncmp), lambda b, g: (0, 0)),
            per_bg((nq, 16, tq)),
        ],
        out_specs=[per_bg((nq, nblk, tq)), per_bg((nq, NSA_HPG * NSA_DV, tq))],
        out_shape=[jax.ShapeDtypeStruct((B, G, nq, nblk, tq), BF16),
                   jax.ShapeDtypeStruct((B, G, nq, NSA_HPG * NSA_DV, tq), F32)],
        scratch_shapes=[pltpu.VMEM((ncmp, wide), F32), pltpu.VMEM((WINDOW + tq, wide), F32)] * PRE_TILES,
        compiler_params=_params(("parallel", "parallel")),
        name="nsa_branches",
    )(qT, kw, vwT, kcmp, vcmpT, ovT, gates)


def _nsa_sel_kernel(qT_ref, csel_ref, ks_ref, vsT_ref, ocw_ref, g_ref, sz_ref, o_ref,
                    *scratch):
    nq, _, tq = qT_ref.shape[2:]
    nblk = csel_ref.shape[3]
    nsub = TSTEP // TKV
    per = TQ_SEL // tq
    wide = NSA_HPG * TQ_SEL

    row = lax.broadcasted_iota(jnp.int32, (CHUNK, TQ_SEL), 0)
    col = lax.broadcasted_iota(jnp.int32, (CHUNK, TQ_SEL), 1)

    def qtile(sup, p):
        qaug_ref, s_a, s_b = scratch[3 * p:3 * p + 3]
        i = sup * (TSTEP // TQ_SEL) + p
        for hh in range(NSA_HPG):
            for u in range(per):
                lanes = slice(hh * TQ_SEL + u * tq, hh * TQ_SEL + (u + 1) * tq)
                qaug_ref[0:NSA_HD, lanes] = qT_ref[0, hh, i * per + u]
                qaug_ref[NSA_HD:NSA_HD + nblk, lanes] = csel_ref[0, 0, i * per + u]
        qaug = qaug_ref[...]

        def scores(j):
            kt = ks_ref[0, 0, j * TSTEP:(j + 1) * TSTEP, :]
            return _dot(kt, qaug)

        def vtile(j):
            return lambda r: vsT_ref[0, 0, j * nsub + r]

        own = p * TQ_SEL
        causal = {own + lo: functools.partial(
            lambda lo, blk: blk + _tile4(jnp.where(lo + row <= col, 0.0, NEG)), lo)
            for lo in range(0, TQ_SEL, CHUNK)}
        init = (jnp.full((1, wide), M_INIT, F32), jnp.zeros((NSA_DV + ONES, wide), F32))
        _, acc = yield from _sweep_steps(scores, vtile, causal, own + TQ_SEL, sup, init, s_a, s_b)
        o_s = acc[0:NSA_DV] * (1.0 / acc[NSA_DV:NSA_DV + 1])

        for hh in range(NSA_HPG):
            rows = slice(hh * NSA_DV, (hh + 1) * NSA_DV)
            for u in range(per):
                t = i * per + u
                lanes = slice(hh * TQ_SEL + u * tq, hh * TQ_SEL + (u + 1) * tq)
                g1 = g_ref[0, 0, t, NSA_HPG + hh:NSA_HPG + hh + 1, :]
                o = ocw_ref[0, 0, t, rows, :] + g1 * o_s[:, lanes]
                o_ref[0, t, rows, :] = (o * sz_ref[0, t, rows, :]).astype(BF16)

    once = jnp.minimum(pl.program_id(0), 0) + 1
    for sup in range((nq * tq) // TSTEP):
        lax.fori_loop(0, once, lambda _, c, sup=sup: (
            _round_robin(qtile(sup, p) for p in range(TSTEP // TQ_SEL)), c)[1], 0)


def _nsa_sel(qT, csel, ks, vsT, ocw, gates, sz):
    B, H, nq, _, tq = qT.shape
    G = NSA_GROUPS
    S = ks.shape[2]
    nblk = S // SEL_LEN
    assert TQ_SEL % tq == 0 and TSTEP % TQ_SEL == 0 and TSTEP % TKV == 0 and S % TSTEP == 0
    per_bg = lambda shape: pl.BlockSpec((1, 1) + shape, lambda b, g: (b, g) + (0,) * len(shape))
    gate = pl.BlockSpec((1, nq, NSA_HPG * NSA_DV, tq), lambda b, g: (b, 0, g, 0))
    return pl.pallas_call(
        _nsa_sel_kernel,
        grid=(B, G),
        in_specs=[
            pl.BlockSpec((1, NSA_HPG, nq, NSA_HD, tq), lambda b, g: (b, g, 0, 0, 0)),
            per_bg((nq, nblk, tq)),
            per_bg((S, NSA_HD + nblk)),
            per_bg((S // TKV, NSA_DV + ONES, TKV)),
            per_bg((nq, NSA_HPG * NSA_DV, tq)),
            per_bg((nq, 16, tq)),
            gate,
        ],
        out_specs=gate,
        out_shape=jax.ShapeDtypeStruct((B, nq, H * NSA_DV, tq), BF16),
        scratch_shapes=([pltpu.VMEM((NSA_HD + nblk, NSA_HPG * TQ_SEL), BF16)]
                        + [pltpu.VMEM((TSTEP, NSA_HPG * TQ_SEL), F32)] * 2) * (TSTEP // TQ_SEL),
        compiler_params=_params(("parallel", "parallel")),
        name="nsa_selected",
    )(qT, csel, ks, vsT, ocw, gates, sz)


def _nsa_perm():
    src = np.full((NSA_HD,), -1, np.int64)
    src[0:12] = np.arange(0, 12)
    src[12:16] = np.arange(24, 28)
    src[16:64] = np.arange(28, 76)
    src[64:76] = np.arange(12, 24)
    src[76:80] = np.arange(76, 80)
    src[80:96] = np.arange(80, 96)
    return src


def _take_cols(w, src):
    idx = np.where(src >= 0, src, 0)
    out = jnp.take(w, jnp.asarray(idx), axis=-1)
    return jnp.where(jnp.asarray(src >= 0), out, 0.0)


def _pad_last(w, width):
    return jnp.pad(w, [(0, 0)] * (w.ndim - 1) + [(0, width - w.shape[-1])])


def _prep_mla(w_in, g_cq, w_uq, g_ckv, w_ukv, g_q, g_k, w_out):
    D = w_in.shape[0]
    o1 = MLA_Q_LORA + MLA_KV_LORA
    z32 = jnp.zeros((D, 32), F32)
    wa = jnp.concatenate([w_in[:, :o1], w_in[:, o1:o1 + 32], z32, w_in[:, o1 + 32:o1 + 64], z32], axis=1)
    wq = w_uq.reshape(MLA_Q_LORA, MLA_HEADS, MLA_QK)
    zq = jnp.zeros((MLA_Q_LORA, MLA_HEADS, 32), F32)
    wq = jnp.concatenate([wq[:, :, 64:], wq[:, :, :32], zq, wq[:, :, 32:64], zq], axis=2)
    z1 = jnp.zeros((32,), F32)
    gq = jnp.concatenate([g_q[64:], g_q[:32], z1, g_q[32:64], z1]) * C_MLA
    wkv = w_ukv.reshape(MLA_KV_LORA, MLA_HEADS, MLA_NOPE + MLA_V)
    return {
        "wa": wa.astype(BF16),
        "wzT": w_in[:, o1 + MLA_ROPE:].T.astype(BF16),
        "gcq": g_cq.reshape(1, -1), "gckv": g_ckv.reshape(1, -1),
        "wuqT": wq.reshape(MLA_Q_LORA, MLA_HEADS * MLA_HD).T.astype(BF16),
        "gq": jnp.broadcast_to(gq[:, None], (MLA_HD, TT)),
        "wuk": wkv[:, :, :MLA_NOPE].reshape(MLA_KV_LORA, -1).astype(BF16),
        "wuvT": wkv[:, :, MLA_NOPE:].reshape(MLA_KV_LORA, -1).T.astype(BF16),
        "gkn": g_k[64:].reshape(1, -1),
        "gkp": jnp.concatenate([g_k[:32], z1, g_k[32:64], z1]).reshape(1, -1),
        "woT": w_out.T.astype(BF16),
    }


def _prep_nsa(w_in, g_q, g_k, pe_k, w1_k, w2_k, pe_v, w1_v, w2_v, w_out):
    D = w_in.shape[0]
    G = NSA_GROUPS
    src = _nsa_perm()
    offs = np.concatenate([[0], np.cumsum(NSA_SIZES)])
    part = lambda i: w_in[:, offs[i]:offs[i + 1]]
    q, kc, vc, ks, vs, kw, vw, gl, z = [part(i) for i in range(9)]
    perm_heads = lambda w, nh: _take_cols(w.reshape(D, nh, NSA_DK), src).reshape(D, nh * NSA_HD)
    pad_groups = lambda w, d: _pad_last(w.reshape(D, G, d), LANE).reshape(D, G * LANE)
    wtok = jnp.concatenate([perm_heads(ks, G), perm_heads(kw, G), pad_groups(kc, NSA_DK),
                            pad_groups(vc, NSA_DV)], axis=1)
    glr = gl.reshape(D, G, NSA_HPG, 3).transpose(0, 1, 3, 2).reshape(D, G, 3 * NSA_HPG)
    glr = _pad_last(glr, 16).reshape(D, G * 16)
    wf = jnp.concatenate([perm_heads(q, NSA_HEADS), vs, vw, glr, z], axis=1)
    halves = lambda w, d: _pad_last(w.reshape(2, CMP_LEN // 2, d, -1).transpose(0, 1, 3, 2), LANE) \
        .transpose(0, 1, 3, 2).reshape(2, (CMP_LEN // 2) * LANE, -1)
    pe_flat = lambda pe: _pad_last(pe, LANE).reshape(2, (CMP_LEN // 2) * LANE)
    return {
        "wtok": wtok.astype(BF16),
        "wfT": wf.T.astype(BF16),
        "gq": jnp.broadcast_to((_take_cols(g_q, src) * C_NSA)[:, None], (NSA_HD, TT)),
        "gk": _take_cols(g_k, src),
        "pek": pe_flat(pe_k), "pev": pe_flat(pe_v),
        "w1k": _pad_last(halves(w1_k, NSA_DK), LANE).astype(BF16),
        "w2k": _pad_last(_take_cols(w2_k, src).T, LANE).T.astype(BF16),
        "w1v": _pad_last(halves(w1_v, NSA_DV), LANE).astype(BF16),
        "w2vT": _pad_last(w2_v.T, LANE).astype(BF16),
        "woT": w_out.T.astype(BF16),
    }


def _overlap_T(S):
    ncmp = S // CMP_STRIDE
    nblk = S // SEL_LEN
    n_cmp = (S - CMP_LEN) // CMP_STRIDE + 1
    cs = np.arange(ncmp) * CMP_STRIDE
    ss = np.arange(nblk) * SEL_LEN
    ov = (cs[None, :] < ss[:, None] + SEL_LEN) & (cs[None, :] + CMP_LEN > ss[:, None])
    ov = ov & (np.arange(ncmp)[None, :] < n_cmp)
    return jnp.asarray(ov.astype(np.float32)).astype(BF16)


def _mla_layer(x, ng, w, tabs):
    qT, k, vT, sz = _mla_in(x, ng.reshape(1, -1), w, tabs)
    og = _mla_attn(qT, k, vT, sz)
    return _out_proj(og, w, x)


def _nsa_layer(x, ng, w, tabs, ovT):
    B, S, _ = x.shape
    qT, ks, kw, kc16, vc16, vsT, vwT, gates, sz = _nsa_in(x, ng.reshape(1, -1), w, tabs)
    kcmp, vcmpT = _nsa_cmp(kc16, vc16, w, tabs)
    csel, ocw = _nsa_pre(qT, kw, vwT, kcmp, vcmpT, ovT, gates)
    og = _nsa_sel(qT, csel, ks, vsT, ocw, gates, sz)
    return _out_proj(og, w, x)


def kernel(x, positions, norm_g, mla_w_in, mla_g_cq, mla_w_uq, mla_g_ckv, mla_w_ukv, mla_g_q, mla_g_k, mla_w_out, nsa_w_in, nsa_g_q, nsa_g_k, nsa_pe_k, nsa_w1_k, nsa_w2_k, nsa_pe_v, nsa_w1_v, nsa_w2_v, nsa_w_out):
    B, S, _ = x.shape
    tabs = _rope_tables(positions)
    ovT = _overlap_T(S)
    w_mla = jax.vmap(_prep_mla)(mla_w_in, mla_g_cq, mla_w_uq, mla_g_ckv, mla_w_ukv, mla_g_q,
                                mla_g_k, mla_w_out)
    w_nsa = jax.vmap(_prep_nsa)(nsa_w_in, nsa_g_q, nsa_g_k, nsa_pe_k, nsa_w1_k, nsa_w2_k,
                                nsa_pe_v, nsa_w1_v, nsa_w2_v, nsa_w_out)
    for i in range(DEPTH):
        j = i // N_MIXERS
        if i % N_MIXERS == 0:
            x = _mla_layer(x, norm_g[i], _LayerWeights(w_mla, j), tabs)
        else:
            x = _nsa_layer(x, norm_g[i], _LayerWeights(w_nsa, j), tabs, ovT)
    return x
```

```python
import functools

import numpy as np
import jax
import jax.numpy as jnp
from jax import lax
from jax.experimental import pallas as pl
from jax.experimental.pallas import tpu as pltpu

F32 = jnp.float32
BF16 = jnp.bfloat16

D_MODEL = 1024
DEPTH = 4
N_MIXERS = 2
ROPE_THETA = 500000.0
RMS_EPS = 1e-6
MLA_HEADS = 16
MLA_NOPE = 128
MLA_ROPE = 64
MLA_V = 128
MLA_QK = MLA_NOPE + MLA_ROPE
MLA_Q_LORA = 384
MLA_KV_LORA = 256
MLA_HD = 256
NSA_HEADS = 16
NSA_GROUPS = 4
NSA_HPG = NSA_HEADS // NSA_GROUPS
NSA_DK = 96
NSA_DV = 64
NSA_ROT = NSA_DK // 4
NSA_HALF = NSA_ROT // 2
NSA_HD = 128
CMP_LEN = 32
CMP_STRIDE = 16
SEL_LEN = 64
N_SELECT = 16
WINDOW = 512
SEL_FORCE = 1e4
NSA_SIZES = (NSA_HEADS * NSA_DK,
             NSA_GROUPS * NSA_DK, NSA_GROUPS * NSA_DV,
             NSA_GROUPS * NSA_DK, NSA_GROUPS * NSA_DV,
             NSA_GROUPS * NSA_DK, NSA_GROUPS * NSA_DV,
             3 * NSA_HEADS, NSA_HEADS * NSA_DV)

LANE = 128
SUB = 8
TT = 256
TO = 512
TKV = 256
TSTEP = 512
TQ_MLA = 512
MLA_HEADS_PER_STEP = 2
TQ_NSA = 128
TQ_SEL = 256
PRE_TILES = 4
RANK_CLASSES = 3
CHUNK = 64
ONES = 16
NEG = -1e30
M_INIT = -1e30
LOG2E = 1.4426950408889634
C_MLA = (MLA_QK ** -0.5) * LOG2E
C_NSA = (NSA_DK ** -0.5) * LOG2E
VMEM_LIMIT = 56 * 1024 * 1024


def _dot(a, b):
    return jnp.dot(a, b, preferred_element_type=F32)


def _dot_nt(a, b):
    return lax.dot_general(a, b, (((1,), (1,)), ((), ())), preferred_element_type=F32)


def _sigmoid(x):
    return 1.0 / (1.0 + jnp.exp(-x))


def _rms_rows(x, g):
    ms = jnp.mean(x * x, axis=-1, keepdims=True)
    return x * lax.rsqrt(ms + RMS_EPS) * g


def _params(sem):
    return pltpu.CompilerParams(dimension_semantics=sem, vmem_limit_bytes=VMEM_LIMIT)


class _LayerWeights:
    def __init__(self, stack, layer):
        self.stack, self.layer = stack, layer

    def __getitem__(self, name):
        return self.stack[name]

    def spec(self, name):
        shape = self.stack[name].shape[1:]
        layer = self.layer
        return pl.BlockSpec((None,) + shape, lambda *_: (layer,) + (0,) * len(shape))


def _rope_kernel(pos_ref, invf_ref, cm_ref, sm_ref, cn_ref, sn_ref, ctm_ref, stm_ref, ctn_ref, stn_ref):
    ang = invf_ref[...] * pos_ref[...]
    c, s = jnp.cos(ang), jnp.sin(ang)
    hm = MLA_ROPE // 2
    cm, sm, cn, sn = c[:hm], s[:hm], c[hm:hm + 16], s[hm:hm + 16]
    cm_ref[...], sm_ref[...], cn_ref[...], sn_ref[...] = cm, sm, cn, sn
    tn = ang.shape[1]
    one = lambda r: jnp.ones((r, tn), F32)
    zero = lambda r: jnp.zeros((r, tn), F32)
    ctm_ref[...] = jnp.concatenate([cm, one(32), cm, one(32)], axis=0).T
    stm_ref[...] = jnp.concatenate([-sm, zero(32), sm, zero(32)], axis=0).T
    ctn_ref[...] = jnp.concatenate([cn, one(48), cn, one(48)], axis=0).T
    stn_ref[...] = jnp.concatenate([-sn, zero(48), sn, zero(48)], axis=0).T


def _rope_tables(positions):
    B, S = positions.shape
    n = positions.size
    tn = 512
    pos = positions.reshape(1, n).astype(F32)
    half_m = MLA_ROPE // 2
    inv_m = ROPE_THETA ** (-jnp.arange(half_m, dtype=F32) / half_m)
    inv_n = ROPE_THETA ** (-jnp.arange(NSA_HALF, dtype=F32) / NSA_HALF)
    invf = jnp.concatenate([inv_m, inv_n, jnp.zeros((16 - NSA_HALF,), F32)])
    rows = invf.shape[0]
    invf = jnp.broadcast_to(invf[:, None], (rows, tn))
    feat = lambda r: pl.BlockSpec((r, tn), lambda i: (0, i))
    tok = pl.BlockSpec((tn, LANE), lambda i: (i, 0))
    cm, sm, cn, sn, ctm, stm, ctn, stn = pl.pallas_call(
        _rope_kernel,
        grid=(n // tn,),
        in_specs=[pl.BlockSpec((1, tn), lambda i: (0, i)),
                  pl.BlockSpec((rows, tn), lambda i: (0, 0))],
        out_specs=[feat(half_m), feat(half_m), feat(16), feat(16), tok, tok, tok, tok],
        out_shape=[jax.ShapeDtypeStruct((half_m, n), F32)] * 2
        + [jax.ShapeDtypeStruct((16, n), F32)] * 2 + [jax.ShapeDtypeStruct((n, LANE), F32)] * 4,
        compiler_params=_params(("parallel",)),
        name="rope_tables",
    )(pos, invf)
    ncmp = S // CMP_STRIDE
    last = CMP_LEN - 1
    pick = lambda t: jnp.pad(t.reshape(B, S, LANE)[:, last::CMP_STRIDE],
                             ((0, 0), (0, ncmp - (S - last + CMP_STRIDE - 1) // CMP_STRIDE), (0, 0)))
    return {"cosT_mla": cm, "sinT_mla": sm, "cosT_nsa": cn, "sinT_nsa": sn,
            "ct_mla": ctm, "st_mla": stm, "ct_nsa": ctn, "st_nsa": stn,
            "ct_cmp": pick(ctn), "st_cmp": pick(stn)}


def _mla_in_kernel(x_ref, ng_ref, wa_ref, wzT_ref, gcq_ref, gckv_ref, wuqT_ref, gq_ref,
                   wuk_ref, wuvT_ref, gkn_ref, gkp_ref, ct_ref, st_ref, cosT_ref, sinT_ref,
                   qT_ref, k_ref, vT_ref, sz_ref):
    x = x_ref[0]
    n = x.shape[0]
    h = _rms_rows(x, ng_ref[...]).astype(BF16)
    pa = _dot(h, wa_ref[...])
    z = _dot_nt(wzT_ref[...], h)
    sz_ref[0, 0] = z * _sigmoid(z)

    cqn = _rms_rows(pa[:, :MLA_Q_LORA], gcq_ref[...]).astype(BF16)
    ckvn = _rms_rows(pa[:, MLA_Q_LORA:MLA_Q_LORA + MLA_KV_LORA], gckv_ref[...]).astype(BF16)
    kpe = pa[:, MLA_Q_LORA + MLA_KV_LORA:]

    qa = _dot_nt(wuqT_ref[...], cqn)
    cos = cosT_ref[...]
    sin = sinT_ref[...]
    gq = gq_ref[...]
    zeros32 = jnp.zeros((32, n), BF16)
    for hd in range(MLA_HEADS):
        blk = qa[hd * MLA_HD:(hd + 1) * MLA_HD]
        ss = jnp.sum(blk * blk, axis=0, keepdims=True)
        qn = blk * lax.rsqrt(ss * (1.0 / MLA_QK) + RMS_EPS) * gq
        x1 = qn[128:160]
        x2 = qn[192:224]
        qT_ref[0, hd, 0, 0:128, :] = qn[0:128].astype(BF16)
        qT_ref[0, hd, 0, 128:160, :] = (x1 * cos - x2 * sin).astype(BF16)
        qT_ref[0, hd, 0, 160:192, :] = zeros32
        qT_ref[0, hd, 0, 192:224, :] = (x2 * cos + x1 * sin).astype(BF16)
        qT_ref[0, hd, 0, 224:256, :] = zeros32

    kn = _dot(ckvn, wuk_ref[...])
    ss_pe = jnp.sum(kpe * kpe, axis=-1, keepdims=True)
    kpg = kpe * gkp_ref[...]
    prot = kpg * ct_ref[...] + pltpu.roll(kpg, 64, axis=1) * st_ref[...]
    gkn = gkn_ref[...]
    for hd in range(MLA_HEADS):
        kb = kn[:, hd * MLA_NOPE:(hd + 1) * MLA_NOPE]
        ss = jnp.sum(kb * kb, axis=-1, keepdims=True) + ss_pe
        r = lax.rsqrt(ss * (1.0 / MLA_QK) + RMS_EPS)
        k_ref[0, hd, :, 0:128] = (kb * r * gkn).astype(BF16)
        k_ref[0, hd, :, 128:256] = (prot * r).astype(BF16)

    va = _dot_nt(wuvT_ref[...], ckvn)
    ones = jnp.ones((ONES, n), BF16)
    for hd in range(MLA_HEADS):
        vT_ref[0, hd, 0, 0:MLA_V, :] = va[hd * MLA_V:(hd + 1) * MLA_V].astype(BF16)
        vT_ref[0, hd, 0, MLA_V:MLA_V + ONES, :] = ones


def _mla_in(x, ng, w, tabs):
    B, S, D = x.shape
    nt = S // TT
    r = TQ_MLA // TT
    assert TT == TKV and TQ_MLA % TT == 0
    full = lambda shape: pl.BlockSpec(shape, lambda b, t: (0,) * len(shape))
    tok = lambda b, t: (b * nt + t, 0)
    feat = lambda b, t: (0, b * nt + t)
    return pl.pallas_call(
        _mla_in_kernel,
        grid=(B, nt),
        in_specs=[
            pl.BlockSpec((1, TT, D), lambda b, t: (b, t, 0)),
            full((1, D)),
            w.spec("wa"), w.spec("wzT"), w.spec("gcq"), w.spec("gckv"),
            w.spec("wuqT"), w.spec("gq"), w.spec("wuk"), w.spec("wuvT"),
            w.spec("gkn"), w.spec("gkp"),
            pl.BlockSpec((TT, LANE), tok), pl.BlockSpec((TT, LANE), tok),
            pl.BlockSpec((32, TT), feat), pl.BlockSpec((32, TT), feat),
        ],
        out_specs=[
            pl.BlockSpec((1, MLA_HEADS, 1, MLA_HD, TT), lambda b, t: (b, 0, t // r, 0, t % r)),
            pl.BlockSpec((1, MLA_HEADS, TT, MLA_HD), lambda b, t: (b, 0, t, 0)),
            pl.BlockSpec((1, MLA_HEADS, 1, MLA_V + ONES, TT), lambda b, t: (b, 0, t, 0, 0)),
            pl.BlockSpec((1, 1, MLA_HEADS * MLA_V, TT), lambda b, t: (b, t // r, 0, t % r)),
        ],
        out_shape=[
            jax.ShapeDtypeStruct((B, MLA_HEADS, S // TQ_MLA, MLA_HD, TQ_MLA), BF16),
            jax.ShapeDtypeStruct((B, MLA_HEADS, S, MLA_HD), BF16),
            jax.ShapeDtypeStruct((B, MLA_HEADS, nt, MLA_V + ONES, TT), BF16),
            jax.ShapeDtypeStruct((B, S // TQ_MLA, MLA_HEADS * MLA_V, TQ_MLA), F32),
        ],
        compiler_params=_params(("parallel", "parallel")),
        name="mla_in_proj",
    )(x, ng, w["wa"], w["wzT"], w["gcq"], w["gckv"], w["wuqT"], w["gq"], w["wuk"],
      w["wuvT"], w["gkn"], w["gkp"], tabs["ct_mla"], tabs["st_mla"], tabs["cosT_mla"],
      tabs["sinT_mla"])


def _softmax_step_ref(s_ref, carry, vtile=None, segs=None, bias=None):
    m, acc = carry
    n, w = s_ref.shape
    segs = segs or [(t * TKV, (t + 1) * TKV, functools.partial(vtile, t)) for t in range(n // TKV)]
    bias = bias or {}
    fold = lambda a, op: op(a.reshape(a.shape[0] // SUB, SUB, w), axis=0)
    mx = None
    for lo in range(0, n, CHUNK):
        blk = s_ref[lo:lo + CHUNK, :]
        if lo in bias:
            blk = bias[lo](blk) if callable(bias[lo]) else blk + bias[lo]
            s_ref[lo:lo + CHUNK, :] = blk
        mx = fold(blk, jnp.max) if mx is None else jnp.maximum(mx, fold(blk, jnp.max))
    m_new = jnp.maximum(m, jnp.max(mx, axis=0, keepdims=True))
    alpha = jnp.exp2(m - m_new)
    pv = None
    for seg_lo, seg_hi, value_tile in segs:
        parts = [jnp.exp2(s_ref[lo:lo + CHUNK, :] - m_new).astype(BF16)
                 for lo in range(seg_lo, seg_hi, CHUNK)]
        d = _dot(value_tile(), jnp.concatenate(parts, axis=0))
        pv = d if pv is None else pv + d
    return m_new, alpha * acc + pv


def _sweep_steps(scores, vtile, last_bias, last_rows, n_full, carry, s_a, s_b):
    bufs = (s_a, s_b)
    s_a[...] = scores(0)
    for j in range(n_full):
        bufs[(j + 1) % 2][...] = scores(j + 1)
        carry = _softmax_step_ref(bufs[j % 2], carry, vtile(j))
        yield
    src = bufs[n_full % 2]
    return _softmax_step_ref(src.at[0:last_rows], carry, vtile(n_full), bias=last_bias)


def _round_robin(generators):
    live = list(generators)
    while live:
        live = [g for g in live if next(g, "done") != "done"]


def _mla_attn_kernel(qT_ref, k_ref, vT_ref, sz_ref, o_ref, *scratch):
    nh, nq, _, tq = qT_ref.shape[1:]
    nsub = TSTEP // TKV

    row = lax.broadcasted_iota(jnp.int32, (CHUNK, tq), 0)
    col = lax.broadcasted_iota(jnp.int32, (CHUNK, tq), 1)
    causal = {lo: functools.partial(lambda lo, blk: jnp.where(lo + row <= col, blk, NEG), lo)
              for lo in range(0, TSTEP, CHUNK)}

    def qtile(h, i):
        s_a, s_b = scratch[2 * h], scratch[2 * h + 1]
        rows = slice(h * MLA_V, (h + 1) * MLA_V)
        q = qT_ref[0, h, i]
        scores = lambda j: _dot(k_ref[0, h, j * TSTEP:(j + 1) * TSTEP, :], q)
        vtile = lambda j: (lambda r: vT_ref[0, h, j * nsub + r])
        init = (jnp.full((1, tq), M_INIT, F32), jnp.zeros((MLA_V + ONES, tq), F32))
        _, acc = yield from _sweep_steps(scores, vtile, causal, TSTEP, i, init, s_a, s_b)
        o = acc[0:MLA_V] / acc[MLA_V:MLA_V + 1]
        o_ref[0, i, rows, :] = (o * sz_ref[0, i, rows, :]).astype(BF16)

    once = jnp.minimum(pl.program_id(0), 0) + 1
    for i in range(nq):
        lax.fori_loop(0, once, lambda _, c, i=i: (_round_robin(qtile(h, i) for h in range(nh)), c)[1], 0)


def _mla_attn(qT, k, vT, sz):
    B, H, nq, _, tq = qT.shape
    S = k.shape[2]
    nh = MLA_HEADS_PER_STEP
    assert tq == TSTEP and TSTEP % TKV == 0 and H % nh == 0
    per_bh = lambda shape: pl.BlockSpec((1, nh) + shape, lambda b, h: (b, h) + (0,) * len(shape))
    gate = pl.BlockSpec((1, nq, nh * MLA_V, tq), lambda b, h: (b, 0, h, 0))
    return pl.pallas_call(
        _mla_attn_kernel,
        grid=(B, H // nh),
        in_specs=[per_bh((nq, MLA_HD, tq)), per_bh((S, MLA_HD)),
                  per_bh((S // TKV, MLA_V + ONES, TKV)), gate],
        out_specs=gate,
        out_shape=jax.ShapeDtypeStruct((B, nq, H * MLA_V, tq), BF16),
        scratch_shapes=[pltpu.VMEM((TSTEP, tq), F32)] * (2 * nh),
        compiler_params=_params(("parallel", "parallel")),
        name="mla_attention",
    )(qT, k, vT, sz)


def _out_proj_kernel(og_ref, wT_ref, x_ref, o_ref):
    ntile = og_ref.shape[1]
    og = og_ref[0, 0] if ntile == 1 else jnp.concatenate([og_ref[0, r] for r in range(ntile)], axis=1)
    yT = _dot(wT_ref[...], og)
    o_ref[0] = x_ref[0] + yT.T


def _out_proj(og, w, x):
    B, S, D = x.shape
    _, _, K, W = og.shape
    r = TO // W
    wT = w["woT"]
    return pl.pallas_call(
        _out_proj_kernel,
        grid=(B, S // TO),
        in_specs=[
            pl.BlockSpec((1, r, K, W), lambda b, t: (b, t, 0, 0)),
            w.spec("woT"),
            pl.BlockSpec((1, TO, D), lambda b, t: (b, t, 0)),
        ],
        out_specs=pl.BlockSpec((1, TO, D), lambda b, t: (b, t, 0)),
        out_shape=jax.ShapeDtypeStruct((B, S, D), F32),
        compiler_params=_params(("parallel", "parallel")),
        name="out_proj",
    )(og, wT, x)


def _nsa_in_kernel(x_ref, ng_ref, wtok_ref, wfT_ref, gq_ref, gk_ref, ct_ref, st_ref,
                   cosT_ref, sinT_ref,
                   qT_ref, ks_ref, kw_ref, kc_ref, vc_ref, vsT_ref, vwT_ref, g_ref, sz_ref, cv_scr):
    t = pl.program_id(1)
    x = x_ref[0]
    n = x.shape[0]
    tq = qT_ref.shape[-1]
    h = _rms_rows(x, ng_ref[...]).astype(BF16)
    G = NSA_GROUPS
    lanes = [slice(u * tq, (u + 1) * tq) for u in range(n // tq)]

    fa = _dot_nt(wfT_ref[...], h)
    cos = cosT_ref[...]
    sin = sinT_ref[...]
    gq = gq_ref[...]
    for hd in range(NSA_HEADS):
        blk = fa[hd * NSA_HD:(hd + 1) * NSA_HD]
        ss = jnp.sum(blk * blk, axis=0, keepdims=True)
        qn = blk * lax.rsqrt(ss * (1.0 / NSA_DK) + RMS_EPS) * gq
        x1 = qn[0:16]
        x2 = qn[64:80]
        r1 = (x1 * cos - x2 * sin).astype(BF16)
        r2 = (x2 * cos + x1 * sin).astype(BF16)
        qb = qn.astype(BF16)
        for u, sl in enumerate(lanes):
            qT_ref[0, hd, u, 0:16, :] = r1[:, sl]
            qT_ref[0, hd, u, 16:64, :] = qb[16:64, sl]
            qT_ref[0, hd, u, 64:80, :] = r2[:, sl]
            qT_ref[0, hd, u, 80:128, :] = qb[80:128, sl]
    off = NSA_HEADS * NSA_HD
    ones = jnp.ones((ONES, n), BF16)
    for ref in (vsT_ref, vwT_ref):
        for g in range(G):
            ref[0, g, 0, 0:NSA_DV, :] = fa[off + g * NSA_DV:off + (g + 1) * NSA_DV].astype(BF16)
            ref[0, g, 0, NSA_DV:NSA_DV + ONES, :] = ones
        off += G * NSA_DV
    sg = _sigmoid(fa[off:off + 16 * G])
    for g in range(G):
        for u, sl in enumerate(lanes):
            g_ref[0, g, u] = sg[g * 16:(g + 1) * 16, sl]
    off += 16 * G
    z = fa[off:off + NSA_HEADS * NSA_DV]
    sz = z * _sigmoid(z)
    for u, sl in enumerate(lanes):
        sz_ref[0, u] = sz[:, sl]

    pt = _dot(h, wtok_ref[...])
    ct = ct_ref[...]
    st = st_ref[...]
    nblk = ks_ref.shape[-1] - NSA_HD
    row = lax.broadcasted_iota(jnp.int32, (n, nblk), 0) + t * n
    col = lax.broadcasted_iota(jnp.int32, (n, nblk), 1)
    onehot = jnp.where((row // SEL_LEN) == col, 1.0, 0.0).astype(BF16)
    for br, ref in ((0, ks_ref), (1, kw_ref)):
        gk = gk_ref[br + 1:br + 2, :]
        for g in range(G):
            kb = pt[:, (br * G + g) * NSA_HD:(br * G + g + 1) * NSA_HD]
            ss = jnp.sum(kb * kb, axis=-1, keepdims=True)
            kn = kb * lax.rsqrt(ss * (1.0 / NSA_DK) + RMS_EPS) * gk
            kr = kn * ct + pltpu.roll(kn, 64, axis=1) * st
            ref[0, g, :, 0:NSA_HD] = kr.astype(BF16)
            if br == 0:
                ref[0, g, :, NSA_HD:NSA_HD + nblk] = onehot
    for br, ref in ((2, kc_ref), (3, vc_ref)):
        for g in range(G):
            stage = cv_scr.at[(br - 2) * G + g]
            stage[...] = pt[:, (br * G + g) * LANE:(br * G + g + 1) * LANE]
            for l in range(CMP_STRIDE):
                ref[0, g, :, l * LANE:(l + 1) * LANE] = stage[pl.ds(l, n // CMP_STRIDE, stride=CMP_STRIDE), :]


def _nsa_in(x, ng, w, tabs):
    B, S, D = x.shape
    nt = S // TT
    G = NSA_GROUPS
    nblk = S // SEL_LEN
    r = TT // TQ_NSA
    nq = S // TQ_NSA
    assert TT == TKV and TT % TQ_NSA == 0
    full = lambda shape: pl.BlockSpec(shape, lambda b, t: (0,) * len(shape))
    tok = lambda b, t: (b * nt + t, 0)
    feat = lambda b, t: (0, b * nt + t)
    tokmaj = lambda width: pl.BlockSpec((1, G, TT, width), lambda b, t: (b, 0, t, 0))
    vtile = pl.BlockSpec((1, G, 1, NSA_DV + ONES, TT), lambda b, t: (b, 0, t, 0, 0))
    cmp_in = pl.BlockSpec((1, G, TT // CMP_STRIDE, CMP_STRIDE * LANE), lambda b, t: (b, 0, t, 0))
    vshape = jax.ShapeDtypeStruct((B, G, nt, NSA_DV + ONES, TT), BF16)
    return pl.pallas_call(
        _nsa_in_kernel,
        grid=(B, nt),
        in_specs=[
            pl.BlockSpec((1, TT, D), lambda b, t: (b, t, 0)),
            full((1, D)),
            w.spec("wtok"), w.spec("wfT"), w.spec("gq"), w.spec("gk"),
            pl.BlockSpec((TT, LANE), tok), pl.BlockSpec((TT, LANE), tok),
            pl.BlockSpec((16, TT), feat), pl.BlockSpec((16, TT), feat),
        ],
        out_specs=[
            pl.BlockSpec((1, NSA_HEADS, r, NSA_HD, TQ_NSA), lambda b, t: (b, 0, t, 0, 0)),
            tokmaj(NSA_HD + nblk), tokmaj(NSA_HD), cmp_in, cmp_in,
            vtile, vtile,
            pl.BlockSpec((1, G, r, 16, TQ_NSA), lambda b, t: (b, 0, t, 0, 0)),
            pl.BlockSpec((1, r, NSA_HEADS * NSA_DV, TQ_NSA), lambda b, t: (b, t, 0, 0)),
        ],
        out_shape=[
            jax.ShapeDtypeStruct((B, NSA_HEADS, nq, NSA_HD, TQ_NSA), BF16),
            jax.ShapeDtypeStruct((B, G, S, NSA_HD + nblk), BF16),
            jax.ShapeDtypeStruct((B, G, S, NSA_HD), BF16),
            jax.ShapeDtypeStruct((B, G, S // CMP_STRIDE, CMP_STRIDE * LANE), F32),
            jax.ShapeDtypeStruct((B, G, S // CMP_STRIDE, CMP_STRIDE * LANE), F32),
            vshape, vshape,
            jax.ShapeDtypeStruct((B, G, nq, 16, TQ_NSA), F32),
            jax.ShapeDtypeStruct((B, nq, NSA_HEADS * NSA_DV, TQ_NSA), F32),
        ],
        scratch_shapes=[pltpu.VMEM((2 * G, TT, LANE), F32)],
        compiler_params=_params(("parallel", "parallel")),
        name="nsa_in_proj",
    )(x, ng, w["wtok"], w["wfT"], w["gq"], w["gk"], tabs["ct_nsa"], tabs["st_nsa"],
      tabs["cosT_nsa"], tabs["sinT_nsa"])


def _nsa_cmp_kernel(kc_ref, vc_ref, pek_ref, pev_ref, w1k_ref, w2k_ref, w1v_ref, w2vT_ref,
                    gk_ref, ct_ref, st_ref, kcmp_ref, vcmpT_ref):
    nrow = kc_ref.shape[2]

    def pre(x, pe_ref, w1_ref):
        lo = _dot((x + pe_ref[0:1, :]).astype(BF16), w1_ref[0])
        hi = _dot((x + pe_ref[1:2, :]).astype(BF16), w1_ref[1])
        return lo + pltpu.roll(hi, nrow - 1, axis=0)

    a = pre(kc_ref[0, 0], pek_ref, w1k_ref)
    kc = _dot((a * _sigmoid(a)).astype(BF16), w2k_ref[...])
    ss = jnp.sum(kc * kc, axis=-1, keepdims=True)
    kn = kc * lax.rsqrt(ss * (1.0 / NSA_DK) + RMS_EPS) * gk_ref[0:1, :]
    kr = kn * ct_ref[0] + pltpu.roll(kn, 64, axis=1) * st_ref[0]
    kcmp_ref[0, 0] = kr.astype(BF16)

    a = pre(vc_ref[0, 0], pev_ref, w1v_ref)
    sv = (a * _sigmoid(a)).astype(BF16)
    vcmpT_ref[0, 0, 0:NSA_DV, :] = _dot_nt(w2vT_ref[...], sv).astype(BF16)
    vcmpT_ref[0, 0, NSA_DV:NSA_DV + ONES, :] = jnp.ones((ONES, nrow), BF16)


def _nsa_cmp(kc16, vc16, w, tabs):
    B, G, nrow, width = kc16.shape
    blk = pl.BlockSpec((1, 1, nrow, width), lambda b, g: (b, g, 0, 0))
    tab = pl.BlockSpec((1, nrow, LANE), lambda b, g: (b, 0, 0))
    return pl.pallas_call(
        _nsa_cmp_kernel,
        grid=(B, G),
        in_specs=[blk, blk, w.spec("pek"), w.spec("pev"), w.spec("w1k"), w.spec("w2k"),
                  w.spec("w1v"), w.spec("w2vT"), w.spec("gk"), tab, tab],
        out_specs=[pl.BlockSpec((1, 1, nrow, NSA_HD), lambda b, g: (b, g, 0, 0)),
                   pl.BlockSpec((1, 1, NSA_DV + ONES, nrow), lambda b, g: (b, g, 0, 0))],
        out_shape=[jax.ShapeDtypeStruct((B, G, nrow, NSA_HD), BF16),
                   jax.ShapeDtypeStruct((B, G, NSA_DV + ONES, nrow), BF16)],
        compiler_params=_params(("parallel", "parallel")),
        name="nsa_compress",
    )(kc16, vc16, w["pek"], w["pev"], w["w1k"], w["w2k"], w["w1v"], w["w2vT"], w["gk"],
      tabs["ct_cmp"], tabs["st_cmp"])


def _tile4(a):
    return jnp.concatenate([a] * NSA_HPG, axis=1)


def _cmp_exp(s_ref, bias):
    n, w = s_ref.shape
    fold = lambda a, op: op(a.reshape(a.shape[0] // SUB, SUB, w), axis=0)
    rows = [slice(lo, lo + CHUNK) for lo in range(0, n, CHUNK)]
    mx = None
    for sl in rows:
        blk = bias[sl.start](s_ref[sl, :])
        s_ref[sl, :] = blk
        mx = fold(blk, jnp.max) if mx is None else jnp.maximum(mx, fold(blk, jnp.max))
    mc = jnp.maximum(jnp.max(mx, axis=0, keepdims=True), 0.5 * NEG)
    return jnp.concatenate([jnp.exp2(s_ref[sl, :] - mc).astype(BF16) for sl in rows], axis=0)


def _nsa_pre_kernel(qT_ref, kw_ref, vwT_ref, kc_ref, vcT_ref, ovT_ref, g_ref,
                    csel_ref, ocw_ref, *scratch, k_sel):
    nq, _, tq = qT_ref.shape[2:]
    ncmp = kc_ref.shape[2]
    nblk = ovT_ref.shape[0]

    wk = WINDOW + tq
    rloc = lax.broadcasted_iota(jnp.int32, (CHUNK, tq), 0)
    cloc = lax.broadcasted_iota(jnp.int32, (CHUNK, tq), 1)
    mask = lambda ok, lo: (lambda blk: blk + _tile4(jnp.where(ok(lo), 0.0, NEG)))

    def one_tile(t, u, clamped, jlim):
        q0 = t * tq
        qT = jnp.concatenate([qT_ref[0, hh, t] for hh in range(NSA_HPG)], axis=1)
        tok = q0 + lax.broadcasted_iota(jnp.int32, (1, tq), 1)

        sc_scr, sw_scr = scratch[2 * u], scratch[2 * u + 1]
        sc_scr[...] = _dot(kc_ref[0, 0], qT)
        yield
        cvalid = lambda lo: ((lo + rloc) * CMP_STRIDE + (CMP_LEN - 1)) <= tok
        cbias = {lo: mask(cvalid, lo) for lo in range(0, ncmp, CHUNK)}
        ec = _cmp_exp(sc_scr, cbias)
        ac = _dot(vcT_ref[0, 0], ec)
        dc = ac[NSA_DV:NSA_DV + 1]
        inv = 1.0 / jnp.where(dc > 0, dc, 1.0)
        o_c = ac[0:NSA_DV] * inv

        if clamped:
            start, off = 0, 0
            wbias = {lo: mask(lambda lo: lo + rloc <= tok, lo) for lo in range(0, wk, CHUNK)}
        else:
            start, off = q0 - WINDOW, (u % (TKV // tq)) * tq
            wbias = {lo: mask(lambda lo: lo + rloc > cloc, lo) for lo in range(0, tq, CHUNK)}
            wbias.update({lo: mask(lambda lo: lo - WINDOW + rloc <= cloc, lo)
                          for lo in range(WINDOW, wk, CHUNK)})
        sw_scr[...] = _dot(kw_ref[0, 0, pl.ds(pl.multiple_of(start, tq), wk), :], qT)
        yield
        w0 = start // TKV
        segs, pos, tile = [], 0, 0
        while pos < wk:
            width = min(TKV - off, wk - pos)
            segs.append((pos, pos + width, functools.partial(
                lambda tile, off, width: vwT_ref[0, 0, w0 + tile, :, off:off + width], tile, off, width)))
            pos, tile, off = pos + width, tile + 1, 0
        init = (jnp.full((1, NSA_HPG * tq), M_INIT, F32),
                jnp.zeros((NSA_DV + ONES, NSA_HPG * tq), F32))
        _, aw = _softmax_step_ref(sw_scr, init, segs=segs, bias=wbias)
        o_w = aw[0:NSA_DV] * (1.0 / aw[NSA_DV:NSA_DV + 1])
        yield

        if jlim:
            iw = _dot(ovT_ref[0:jlim, :], ec) * inv
            imp = iw[:, 0:tq]
            for hh in range(1, NSA_HPG):
                imp = imp + iw[:, hh * tq:(hh + 1) * tq]
            bidx = lax.broadcasted_iota(jnp.int32, (jlim, tq), 0)
            cur = tok // SEL_LEN
            forced = (bidx == 0) | (bidx == cur) | (bidx == cur - 1)
            valid = (bidx * SEL_LEN) <= tok
            score = jnp.where(forced, SEL_FORCE, jnp.where(valid, imp, -1.0))
            grp = [score[r * SUB:(r + 1) * SUB] for r in range(jlim // SUB)]
            bsub = lax.broadcasted_iota(jnp.int32, (SUB, tq), 0)
            cnt = [jnp.zeros((SUB, tq), F32) for _ in grp]
            for j in range(jlim):
                rowj = score[j:j + 1, :]
                for r in range(jlim // SUB):
                    if r * SUB > j:
                        cnt[r] = cnt[r] + jnp.where(rowj >= grp[r], 1.0, 0.0)
                    elif r * SUB + SUB - 1 < j:
                        cnt[r] = cnt[r] + jnp.where(rowj > grp[r], 1.0, 0.0)
                    else:
                        tie = jnp.where(bsub + r * SUB > j, 1.0, 0.0)
                        cnt[r] = cnt[r] + jnp.where(rowj > grp[r], 1.0, 0.0)
                        cnt[r] = cnt[r] + jnp.where(rowj == grp[r], tie, 0.0)
            cnt = jnp.concatenate(cnt, axis=0)
            csel_ref[0, 0, t, 0:jlim, :] = jnp.where(cnt < k_sel, 0.0, NEG).astype(BF16)
        if jlim < nblk:
            csel_ref[0, 0, t, jlim:nblk, :] = jnp.zeros((nblk - jlim, tq), BF16)

        for hh in range(NSA_HPG):
            sl = slice(hh * tq, (hh + 1) * tq)
            g0 = g_ref[0, 0, t, hh:hh + 1, :]
            g2 = g_ref[0, 0, t, 2 * NSA_HPG + hh:2 * NSA_HPG + hh + 1, :]
            ocw_ref[0, 0, t, hh * NSA_DV:(hh + 1) * NSA_DV, :] = g0 * o_c[:, sl] + g2 * o_w[:, sl]

    def make_trip(clamped, jlim):
        def trip(i, _):
            _round_robin(one_tile(i * PRE_TILES + u, u, clamped, jlim) for u in range(PRE_TILES))
            return 0
        return trip

    step = tq * PRE_TILES
    n_trips = nq // PRE_TILES
    n_clamped = min(WINDOW // step, n_trips)
    n_norank = min(max((k_sel * SEL_LEN) // step, n_clamped), n_trips)
    cuts = sorted({0, n_clamped, n_norank, n_trips}
                  | {n_norank + (k * (n_trips - n_norank)) // RANK_CLASSES for k in range(RANK_CLASSES)})
    for lo, hi in zip(cuts[:-1], cuts[1:]):
        causal_blocks = -(-(hi * step) // SEL_LEN)
        jlim = 0 if hi <= n_norank else min(nblk, -(-causal_blocks // SUB) * SUB)
        lax.fori_loop(lo, hi, make_trip(lo < n_clamped, jlim), 0)


def _nsa_pre(qT, kw, vwT, kcmp, vcmpT, ovT, gates):
    B, H, nq, _, tq = qT.shape
    G = NSA_GROUPS
    S = kw.shape[2]
    nblk = S // SEL_LEN
    ncmp = kcmp.shape[2]
    assert TKV % tq == 0 and nq % PRE_TILES == 0 and PRE_TILES % (TKV // tq) == 0
    assert WINDOW % (tq * PRE_TILES) == 0 and WINDOW % TKV == 0 and S >= WINDOW + tq
    kern = functools.partial(_nsa_pre_kernel, k_sel=min(N_SELECT, nblk))
    per_bg = lambda shape: pl.BlockSpec((1, 1) + shape, lambda b, g: (b, g) + (0,) * len(shape))
    wide = NSA_HPG * tq
    return pl.pallas_call(
        kern,
        grid=(B, G),
        in_specs=[
            pl.BlockSpec((1, NSA_HPG, nq, NSA_HD, tq), lambda b, g: (b, g, 0, 0, 0)),
            per_bg((S, NSA_HD)),
            per_bg((S // TKV, NSA_DV + ONES, TKV)),
            per_bg((ncmp, NSA_HD)),
            per_bg((NSA_DV + ONES, ncmp)),
            pl.BlockSpec((nblk, ncmp), lambda b, g: (0, 0)),
            per_bg((nq, 16, tq)),
        ],
        out_specs=[per_bg((nq, nblk, tq)), per_bg((nq, NSA_HPG * NSA_DV, tq))],
        out_shape=[jax.ShapeDtypeStruct((B, G, nq, nblk, tq), BF16),
                   jax.ShapeDtypeStruct((B, G, nq, NSA_HPG * NSA_DV, tq), F32)],
        scratch_shapes=[pltpu.VMEM((ncmp, wide), F32), pltpu.VMEM((WINDOW + tq, wide), F32)] * PRE_TILES,
        compiler_params=_params(("parallel", "parallel")),
        name="nsa_branches",
    )(qT, kw, vwT, kcmp, vcmpT, ovT, gates)


def _nsa_sel_kernel(qT_ref, csel_ref, ks_ref, vsT_ref, ocw_ref, g_ref, sz_ref, o_ref,
                    *scratch):
    nq, _, tq = qT_ref.shape[2:]
    nblk = csel_ref.shape[3]
    nsub = TSTEP // TKV
    per = TQ_SEL // tq
    wide = NSA_HPG * TQ_SEL

    row = lax.broadcasted_iota(jnp.int32, (CHUNK, TQ_SEL), 0)
    col = lax.broadcasted_iota(jnp.int32, (CHUNK, TQ_SEL), 1)

    def qtile(sup, p):
        qaug_ref, s_a, s_b = scratch[3 * p:3 * p + 3]
        i = sup * (TSTEP // TQ_SEL) + p
        for hh in range(NSA_HPG):
            for u in range(per):
                lanes = slice(hh * TQ_SEL + u * tq, hh * TQ_SEL + (u + 1) * tq)
                qaug_ref[0:NSA_HD, lanes] = qT_ref[0, hh, i * per + u]
                qaug_ref[NSA_HD:NSA_HD + nblk, lanes] = csel_ref[0, 0, i * per + u]
        qaug = qaug_ref[...]

        def scores(j):
            kt = ks_ref[0, 0, j * TSTEP:(j + 1) * TSTEP, :]
            return _dot(kt, qaug)

        def vtile(j):
            return lambda r: vsT_ref[0, 0, j * nsub + r]

        own = p * TQ_SEL
        causal = {own + lo: functools.partial(
            lambda lo, blk: blk + _tile4(jnp.where(lo + row <= col, 0.0, NEG)), lo)
            for lo in range(0, TQ_SEL, CHUNK)}
        init = (jnp.full((1, wide), M_INIT, F32), jnp.zeros((NSA_DV + ONES, wide), F32))
        _, acc = yield from _sweep_steps(scores, vtile, causal, own + TQ_SEL, sup, init, s_a, s_b)
        o_s = acc[0:NSA_DV] * (1.0 / acc[NSA_DV:NSA_DV + 1])

        for hh in range(NSA_HPG):
            rows = slice(hh * NSA_DV, (hh + 1) * NSA_DV)
            for u in range(per):
                t = i * per + u
                lanes = slice(hh * TQ_SEL + u * tq, hh * TQ_SEL + (u + 1) * tq)
                g1 = g_ref[0, 0, t, NSA_HPG + hh:NSA_HPG + hh + 1, :]
                o = ocw_ref[0, 0, t, rows, :] + g1 * o_s[:, lanes]
                o_ref[0, t, rows, :] = (o * sz_ref[0, t, rows, :]).astype(BF16)

    once = jnp.minimum(pl.program_id(0), 0) + 1
    for sup in range((nq * tq) // TSTEP):
        lax.fori_loop(0, once, lambda _, c, sup=sup: (
            _round_robin(qtile(sup, p) for p in range(TSTEP // TQ_SEL)), c)[1], 0)


def _nsa_sel(qT, csel, ks, vsT, ocw, gates, sz):
    B, H, nq, _, tq = qT.shape
    G = NSA_GROUPS
    S = ks.shape[2]
    nblk = S // SEL_LEN
    assert TQ_SEL % tq == 0 and TSTEP % TQ_SEL == 0 and TSTEP % TKV == 0 and S % TSTEP == 0
    per_bg = lambda shape: pl.BlockSpec((1, 1) + shape, lambda b, g: (b, g) + (0,) * len(shape))
    gate = pl.BlockSpec((1, nq, NSA_HPG * NSA_DV, tq), lambda b, g: (b, 0, g, 0))
    return pl.pallas_call(
        _nsa_sel_kernel,
        grid=(B, G),
        in_specs=[
            pl.BlockSpec((1, NSA_HPG, nq, NSA_HD, tq), lambda b, g: (b, g, 0, 0, 0)),
            per_bg((nq, nblk, tq)),
            per_bg((S, NSA_HD + nblk)),
            per_bg((S // TKV, NSA_DV + ONES, TKV)),
            per_bg((nq, NSA_HPG * NSA_DV, tq)),
            per_bg((nq, 16, tq)),
            gate,
        ],
        out_specs=gate,
        out_shape=jax.ShapeDtypeStruct((B, nq, H * NSA_DV, tq), BF16),
        scratch_shapes=([pltpu.VMEM((NSA_HD + nblk, NSA_HPG * TQ_SEL), BF16)]
                        + [pltpu.VMEM((TSTEP, NSA_HPG * TQ_SEL), F32)] * 2) * (TSTEP // TQ_SEL),
        compiler_params=_params(("parallel", "parallel")),
        name="nsa_selected",
    )(qT, csel, ks, vsT, ocw, gates, sz)


def _nsa_perm():
    src = np.full((NSA_HD,), -1, np.int64)
    src[0:12] = np.arange(0, 12)
    src[12:16] = np.arange(24, 28)
    src[16:64] = np.arange(28, 76)
    src[64:76] = np.arange(12, 24)
    src[76:80] = np.arange(76, 80)
    src[80:96] = np.arange(80, 96)
    return src


def _take_cols(w, src):
    idx = np.where(src >= 0, src, 0)
    out = jnp.take(w, jnp.asarray(idx), axis=-1)
    return jnp.where(jnp.asarray(src >= 0), out, 0.0)


def _pad_last(w, width):
    return jnp.pad(w, [(0, 0)] * (w.ndim - 1) + [(0, width - w.shape[-1])])


def _prep_mla(w_in, g_cq, w_uq, g_ckv, w_ukv, g_q, g_k, w_out):
    D = w_in.shape[0]
    o1 = MLA_Q_LORA + MLA_KV_LORA
    z32 = jnp.zeros((D, 32), F32)
    wa = jnp.concatenate([w_in[:, :o1], w_in[:, o1:o1 + 32], z32, w_in[:, o1 + 32:o1 + 64], z32], axis=1)
    wq = w_uq.reshape(MLA_Q_LORA, MLA_HEADS, MLA_QK)
    zq = jnp.zeros((MLA_Q_LORA, MLA_HEADS, 32), F32)
    wq = jnp.concatenate([wq[:, :, 64:], wq[:, :, :32], zq, wq[:, :, 32:64], zq], axis=2)
    z1 = jnp.zeros((32,), F32)
    gq = jnp.concatenate([g_q[64:], g_q[:32], z1, g_q[32:64], z1]) * C_MLA
    wkv = w_ukv.reshape(MLA_KV_LORA, MLA_HEADS, MLA_NOPE + MLA_V)
    return {
        "wa": wa.astype(BF16),
        "wzT": w_in[:, o1 + MLA_ROPE:].T.astype(BF16),
        "gcq": g_cq.reshape(1, -1), "gckv": g_ckv.reshape(1, -1),
        "wuqT": wq.reshape(MLA_Q_LORA, MLA_HEADS * MLA_HD).T.astype(BF16),
        "gq": jnp.broadcast_to(gq[:, None], (MLA_HD, TT)),
        "wuk": wkv[:, :, :MLA_NOPE].reshape(MLA_KV_LORA, -1).astype(BF16),
        "wuvT": wkv[:, :, MLA_NOPE:].reshape(MLA_KV_LORA, -1).T.astype(BF16),
        "gkn": g_k[64:].reshape(1, -1),
        "gkp": jnp.concatenate([g_k[:32], z1, g_k[32:64], z1]).reshape(1, -1),
        "woT": w_out.T.astype(BF16),
    }


def _prep_nsa(w_in, g_q, g_k, pe_k, w1_k, w2_k, pe_v, w1_v, w2_v, w_out):
    D = w_in.shape[0]
    G = NSA_GROUPS
    src = _nsa_perm()
    offs = np.concatenate([[0], np.cumsum(NSA_SIZES)])
    part = lambda i: w_in[:, offs[i]:offs[i + 1]]
    q, kc, vc, ks, vs, kw, vw, gl, z = [part(i) for i in range(9)]
    perm_heads = lambda w, nh: _take_cols(w.reshape(D, nh, NSA_DK), src).reshape(D, nh * NSA_HD)
    pad_groups = lambda w, d: _pad_last(w.reshape(D, G, d), LANE).reshape(D, G * LANE)
    wtok = jnp.concatenate([perm_heads(ks, G), perm_heads(kw, G), pad_groups(kc, NSA_DK),
                            pad_groups(vc, NSA_DV)], axis=1)
    glr = gl.reshape(D, G, NSA_HPG, 3).transpose(0, 1, 3, 2).reshape(D, G, 3 * NSA_HPG)
    glr = _pad_last(glr, 16).reshape(D, G * 16)
    wf = jnp.concatenate([perm_heads(q, NSA_HEADS), vs, vw, glr, z], axis=1)
    halves = lambda w, d: _pad_last(w.reshape(2, CMP_LEN // 2, d, -1).transpose(0, 1, 3, 2), LANE) \
        .transpose(0, 1, 3, 2).reshape(2, (CMP_LEN // 2) * LANE, -1)
    pe_flat = lambda pe: _pad_last(pe, LANE).reshape(2, (CMP_LEN // 2) * LANE)
    return {
        "wtok": wtok.astype(BF16),
        "wfT": wf.T.astype(BF16),
        "gq": jnp.broadcast_to((_take_cols(g_q, src) * C_NSA)[:, None], (NSA_HD, TT)),
        "gk": _take_cols(g_k, src),
        "pek": pe_flat(pe_k), "pev": pe_flat(pe_v),
        "w1k": _pad_last(halves(w1_k, NSA_DK), LANE).astype(BF16),
        "w2k": _pad_last(_take_cols(w2_k, src).T, LANE).T.astype(BF16),
        "w1v": _pad_last(halves(w1_v, NSA_DV), LANE).astype(BF16),
        "w2vT": _pad_last(w2_v.T, LANE).astype(BF16),
        "woT": w_out.T.astype(BF16),
    }


def _overlap_T(S):
    ncmp = S // CMP_STRIDE
    nblk = S // SEL_LEN
    n_cmp = (S - CMP_LEN) // CMP_STRIDE + 1
    cs = np.arange(ncmp) * CMP_STRIDE
    ss = np.arange(nblk) * SEL_LEN
    ov = (cs[None, :] < ss[:, None] + SEL_LEN) & (cs[None, :] + CMP_LEN > ss[:, None])
    ov = ov & (np.arange(ncmp)[None, :] < n_cmp)
    return jnp.asarray(ov.astype(np.float32)).astype(BF16)


def _mla_layer(x, ng, w, tabs):
    qT, k, vT, sz = _mla_in(x, ng.reshape(1, -1), w, tabs)
    og = _mla_attn(qT, k, vT, sz)
    return _out_proj(og, w, x)


def _nsa_layer(x, ng, w, tabs, ovT):
    B, S, _ = x.shape
    qT, ks, kw, kc16, vc16, vsT, vwT, gates, sz = _nsa_in(x, ng.reshape(1, -1), w, tabs)
    kcmp, vcmpT = _nsa_cmp(kc16, vc16, w, tabs)
    csel, ocw = _nsa_pre(qT, kw, vwT, kcmp, vcmpT, ovT, gates)
    og = _nsa_sel(qT, csel, ks, vsT, ocw, gates, sz)
    return _out_proj(og, w, x)


def kernel(x, positions, norm_g, mla_w_in, mla_g_cq, mla_w_uq, mla_g_ckv, mla_w_ukv, mla_g_q, mla_g_k, mla_w_out, nsa_w_in, nsa_g_q, nsa_g_k, nsa_pe_k, nsa_w1_k, nsa_w2_k, nsa_pe_v, nsa_w1_v, nsa_w2_v, nsa_w_out):
    B, S, _ = x.shape
    tabs = _rope_tables(positions)
    ovT = _overlap_T(S)
    w_mla = jax.vmap(_prep_mla)(mla_w_in, mla_g_cq, mla_w_uq, mla_g_ckv, mla_w_ukv, mla_g_q,
                                mla_g_k, mla_w_out)
    w_nsa = jax.vmap(_prep_nsa)(nsa_w_in, nsa_g_q, nsa_g_k, nsa_pe_k, nsa_w1_k, nsa_w2_k,
                                nsa_pe_v, nsa_w1_v, nsa_w2_v, nsa_w_out)
    for i in range(DEPTH):
        j = i // N_MIXERS
        if i % N_MIXERS == 0:
            x = _mla_layer(x, norm_g[i], _LayerWeights(w_mla, j), tabs)
        else:
            x = _nsa_layer(x, norm_g[i], _LayerWeights(w_nsa, j), tabs, ovT)
    return x
```

```python
import functools

import numpy as np
import jax
import jax.numpy as jnp
from jax import lax
from jax.experimental import pallas as pl
from jax.experimental.pallas import tpu as pltpu

F32 = jnp.float32
BF16 = jnp.bfloat16

D_MODEL = 1024
DEPTH = 4
N_MIXERS = 2
ROPE_THETA = 500000.0
RMS_EPS = 1e-6
MLA_HEADS = 16
MLA_NOPE = 128
MLA_ROPE = 64
MLA_V = 128
MLA_QK = MLA_NOPE + MLA_ROPE
MLA_Q_LORA = 384
MLA_KV_LORA = 256
MLA_HD = 256
NSA_HEADS = 16
NSA_GROUPS = 4
NSA_HPG = NSA_HEADS // NSA_GROUPS
NSA_DK = 96
NSA_DV = 64
NSA_ROT = NSA_DK // 4
NSA_HALF = NSA_ROT // 2
NSA_HD = 128
CMP_LEN = 32
CMP_STRIDE = 16
SEL_LEN = 64
N_SELECT = 16
WINDOW = 512
SEL_FORCE = 1e4
NSA_SIZES = (NSA_HEADS * NSA_DK,
             NSA_GROUPS * NSA_DK, NSA_GROUPS * NSA_DV,
             NSA_GROUPS * NSA_DK, NSA_GROUPS * NSA_DV,
             NSA_GROUPS * NSA_DK, NSA_GROUPS * NSA_DV,
             3 * NSA_HEADS, NSA_HEADS * NSA_DV)

LANE = 128
SUB = 8
TT = 256
TO = 512
TKV = 256
TSTEP = 512
TQ_MLA = 512
MLA_HEADS_PER_STEP = 2
TQ_NSA = 128
TQ_SEL = 256
PRE_TILES = 4
RANK_CLASSES = 3
CHUNK = 64
ONES = 16
NEG = -1e30
M_INIT = -1e30
LOG2E = 1.4426950408889634
C_MLA = (MLA_QK ** -0.5) * LOG2E
C_NSA = (NSA_DK ** -0.5) * LOG2E
VMEM_LIMIT = 56 * 1024 * 1024


def _dot(a, b):
    return jnp.dot(a, b, preferred_element_type=F32)


def _dot_nt(a, b):
    return lax.dot_general(a, b, (((1,), (1,)), ((), ())), preferred_element_type=F32)


def _sigmoid(x):
    return 0.5 * jnp.tanh(0.5 * x) + 0.5


def _rms_rows(x, g):
    ms = jnp.mean(x * x, axis=-1, keepdims=True)
    return x * lax.rsqrt(ms + RMS_EPS) * g


def _params(sem):
    return pltpu.CompilerParams(dimension_semantics=sem, vmem_limit_bytes=VMEM_LIMIT)


class _LayerWeights:
    def __init__(self, stack, layer):
        self.stack, self.layer = stack, layer

    def __getitem__(self, name):
        return self.stack[name]

    def spec(self, name):
        shape = self.stack[name].shape[1:]
        layer = self.layer
        return pl.BlockSpec((None,) + shape, lambda *_: (layer,) + (0,) * len(shape))


def _rope_kernel(pos_ref, invf_ref, cm_ref, sm_ref, cn_ref, sn_ref, ctm_ref, stm_ref, ctn_ref, stn_ref):
    ang = invf_ref[...] * pos_ref[...]
    c, s = jnp.cos(ang), jnp.sin(ang)
    hm = MLA_ROPE // 2
    cm, sm, cn, sn = c[:hm], s[:hm], c[hm:hm + 16], s[hm:hm + 16]
    cm_ref[...], sm_ref[...], cn_ref[...], sn_ref[...] = cm, sm, cn, sn
    tn = ang.shape[1]
    one = lambda r: jnp.ones((r, tn), F32)
    zero = lambda r: jnp.zeros((r, tn), F32)
    ctm_ref[...] = jnp.concatenate([cm, one(32), cm, one(32)], axis=0).T
    stm_ref[...] = jnp.concatenate([-sm, zero(32), sm, zero(32)], axis=0).T
    ctn_ref[...] = jnp.concatenate([cn, one(48), cn, one(48)], axis=0).T
    stn_ref[...] = jnp.concatenate([-sn, zero(48), sn, zero(48)], axis=0).T


def _rope_tables(positions):
    B, S = positions.shape
    n = positions.size
    tn = 512
    pos = positions.reshape(1, n).astype(F32)
    half_m = MLA_ROPE // 2
    inv_m = ROPE_THETA ** (-jnp.arange(half_m, dtype=F32) / half_m)
    inv_n = ROPE_THETA ** (-jnp.arange(NSA_HALF, dtype=F32) / NSA_HALF)
    invf = jnp.concatenate([inv_m, inv_n, jnp.zeros((16 - NSA_HALF,), F32)])
    rows = invf.shape[0]
    invf = jnp.broadcast_to(invf[:, None], (rows, tn))
    feat = lambda r: pl.BlockSpec((r, tn), lambda i: (0, i))
    tok = pl.BlockSpec((tn, LANE), lambda i: (i, 0))
    cm, sm, cn, sn, ctm, stm, ctn, stn = pl.pallas_call(
        _rope_kernel,
        grid=(n // tn,),
        in_specs=[pl.BlockSpec((1, tn), lambda i: (0, i)),
                  pl.BlockSpec((rows, tn), lambda i: (0, 0))],
        out_specs=[feat(half_m), feat(half_m), feat(16), feat(16), tok, tok, tok, tok],
        out_shape=[jax.ShapeDtypeStruct((half_m, n), F32)] * 2
        + [jax.ShapeDtypeStruct((16, n), F32)] * 2 + [jax.ShapeDtypeStruct((n, LANE), F32)] * 4,
        compiler_params=_params(("parallel",)),
        name="rope_tables",
    )(pos, invf)
    ncmp = S // CMP_STRIDE
    last = CMP_LEN - 1
    pick = lambda t: jnp.pad(t.reshape(B, S, LANE)[:, last::CMP_STRIDE],
                             ((0, 0), (0, ncmp - (S - last + CMP_STRIDE - 1) // CMP_STRIDE), (0, 0)))
    return {"cosT_mla": cm, "sinT_mla": sm, "cosT_nsa": cn, "sinT_nsa": sn,
            "ct_mla": ctm, "st_mla": stm, "ct_nsa": ctn, "st_nsa": stn,
            "ct_cmp": pick(ctn), "st_cmp": pick(stn)}


def _mla_in_kernel(x_ref, ng_ref, wa_ref, wzT_ref, gcq_ref, gckv_ref, wuqT_ref, gq_ref,
                   wuk_ref, wuvT_ref, gkn_ref, gkp_ref, ct_ref, st_ref, cosT_ref, sinT_ref,
                   qT_ref, k_ref, vT_ref, sz_ref):
    x = x_ref[0]
    n = x.shape[0]
    h = _rms_rows(x, ng_ref[...]).astype(BF16)
    pa = _dot(h, wa_ref[...])
    z = _dot_nt(wzT_ref[...], h)
    sz_ref[0, 0] = z * _sigmoid(z)

    cqn = _rms_rows(pa[:, :MLA_Q_LORA], gcq_ref[...]).astype(BF16)
    ckvn = _rms_rows(pa[:, MLA_Q_LORA:MLA_Q_LORA + MLA_KV_LORA], gckv_ref[...]).astype(BF16)
    kpe = pa[:, MLA_Q_LORA + MLA_KV_LORA:]

    qa = _dot_nt(wuqT_ref[...], cqn)
    cos = cosT_ref[...]
    sin = sinT_ref[...]
    gq = gq_ref[...]
    zeros32 = jnp.zeros((32, n), BF16)
    for hd in range(MLA_HEADS):
        blk = qa[hd * MLA_HD:(hd + 1) * MLA_HD]
        ss = jnp.sum(blk * blk, axis=0, keepdims=True)
        qn = blk * lax.rsqrt(ss * (1.0 / MLA_QK) + RMS_EPS) * gq
        x1 = qn[128:160]
        x2 = qn[192:224]
        qT_ref[0, hd, 0, 0:128, :] = qn[0:128].astype(BF16)
        qT_ref[0, hd, 0, 128:160, :] = (x1 * cos - x2 * sin).astype(BF16)
        qT_ref[0, hd, 0, 160:192, :] = zeros32
        qT_ref[0, hd, 0, 192:224, :] = (x2 * cos + x1 * sin).astype(BF16)
        qT_ref[0, hd, 0, 224:256, :] = zeros32

    kn = _dot(ckvn, wuk_ref[...])
    ss_pe = jnp.sum(kpe * kpe, axis=-1, keepdims=True)
    kpg = kpe * gkp_ref[...]
    prot = kpg * ct_ref[...] + pltpu.roll(kpg, 64, axis=1) * st_ref[...]
    gkn = gkn_ref[...]
    for hd in range(MLA_HEADS):
        kb = kn[:, hd * MLA_NOPE:(hd + 1) * MLA_NOPE]
        ss = jnp.sum(kb * kb, axis=-1, keepdims=True) + ss_pe
        r = lax.rsqrt(ss * (1.0 / MLA_QK) + RMS_EPS)
        k_ref[0, hd, :, 0:128] = (kb * r * gkn).astype(BF16)
        k_ref[0, hd, :, 128:256] = (prot * r).astype(BF16)

    va = _dot_nt(wuvT_ref[...], ckvn)
    ones = jnp.ones((ONES, n), BF16)
    for hd in range(MLA_HEADS):
        vT_ref[0, hd, 0, 0:MLA_V, :] = va[hd * MLA_V:(hd + 1) * MLA_V].astype(BF16)
        vT_ref[0, hd, 0, MLA_V:MLA_V + ONES, :] = ones


def _mla_in(x, ng, w, tabs):
    B, S, D = x.shape
    nt = S // TT
    r = TQ_MLA // TT
    assert TT == TKV and TQ_MLA % TT == 0
    full = lambda shape: pl.BlockSpec(shape, lambda b, t: (0,) * len(shape))
    tok = lambda b, t: (b * nt + t, 0)
    feat = lambda b, t: (0, b * nt + t)
    return pl.pallas_call(
        _mla_in_kernel,
        grid=(B, nt),
        in_specs=[
            pl.BlockSpec((1, TT, D), lambda b, t: (b, t, 0)),
            full((1, D)),
            w.spec("wa"), w.spec("wzT"), w.spec("gcq"), w.spec("gckv"),
            w.spec("wuqT"), w.spec("gq"), w.spec("wuk"), w.spec("wuvT"),
            w.spec("gkn"), w.spec("gkp"),
            pl.BlockSpec((TT, LANE), tok), pl.BlockSpec((TT, LANE), tok),
            pl.BlockSpec((32, TT), feat), pl.BlockSpec((32, TT), feat),
        ],
        out_specs=[
            pl.BlockSpec((1, MLA_HEADS, 1, MLA_HD, TT), lambda b, t: (b, 0, t // r, 0, t % r)),
            pl.BlockSpec((1, MLA_HEADS, TT, MLA_HD), lambda b, t: (b, 0, t, 0)),
            pl.BlockSpec((1, MLA_HEADS, 1, MLA_V + ONES, TT), lambda b, t: (b, 0, t, 0, 0)),
            pl.BlockSpec((1, 1, MLA_HEADS * MLA_V, TT), lambda b, t: (b, t // r, 0, t % r)),
        ],
        out_shape=[
            jax.ShapeDtypeStruct((B, MLA_HEADS, S // TQ_MLA, MLA_HD, TQ_MLA), BF16),
            jax.ShapeDtypeStruct((B, MLA_HEADS, S, MLA_HD), BF16),
            jax.ShapeDtypeStruct((B, MLA_HEADS, nt, MLA_V + ONES, TT), BF16),
            jax.ShapeDtypeStruct((B, S // TQ_MLA, MLA_HEADS * MLA_V, TQ_MLA), F32),
        ],
        compiler_params=_params(("parallel", "parallel")),
        name="mla_in_proj",
    )(x, ng, w["wa"], w["wzT"], w["gcq"], w["gckv"], w["wuqT"], w["gq"], w["wuk"],
      w["wuvT"], w["gkn"], w["gkp"], tabs["ct_mla"], tabs["st_mla"], tabs["cosT_mla"],
      tabs["sinT_mla"])


def _softmax_step_ref(s_ref, carry, vtile=None, segs=None, bias=None):
    m, acc = carry
    n, w = s_ref.shape
    segs = segs or [(t * TKV, (t + 1) * TKV, functools.partial(vtile, t)) for t in range(n // TKV)]
    bias = bias or {}
    fold = lambda a, op: op(a.reshape(a.shape[0] // SUB, SUB, w), axis=0)
    mx = None
    for lo in range(0, n, CHUNK):
        blk = s_ref[lo:lo + CHUNK, :]
        if lo in bias:
            blk = bias[lo](blk) if callable(bias[lo]) else blk + bias[lo]
            s_ref[lo:lo + CHUNK, :] = blk
        mx = fold(blk, jnp.max) if mx is None else jnp.maximum(mx, fold(blk, jnp.max))
    m_new = jnp.maximum(m, jnp.max(mx, axis=0, keepdims=True))
    alpha = jnp.exp2(m - m_new)
    pv = None
    for seg_lo, seg_hi, value_tile in segs:
        parts = [jnp.exp2(s_ref[lo:lo + CHUNK, :] - m_new).astype(BF16)
                 for lo in range(seg_lo, seg_hi, CHUNK)]
        d = _dot(value_tile(), jnp.concatenate(parts, axis=0))
        pv = d if pv is None else pv + d
    return m_new, alpha * acc + pv


def _sweep_steps(scores, vtile, last_bias, last_rows, n_full, carry, s_a, s_b):
    bufs = (s_a, s_b)
    s_a[...] = scores(0)
    for j in range(n_full):
        bufs[(j + 1) % 2][...] = scores(j + 1)
        carry = _softmax_step_ref(bufs[j % 2], carry, vtile(j))
        yield
    src = bufs[n_full % 2]
    return _softmax_step_ref(src.at[0:last_rows], carry, vtile(n_full), bias=last_bias)


def _round_robin(generators):
    live = list(generators)
    while live:
        live = [g for g in live if next(g, "done") != "done"]


def _mla_attn_kernel(qT_ref, k_ref, vT_ref, sz_ref, o_ref, *scratch):
    nh, nq, _, tq = qT_ref.shape[1:]
    nsub = TSTEP // TKV

    row = lax.broadcasted_iota(jnp.int32, (CHUNK, tq), 0)
    col = lax.broadcasted_iota(jnp.int32, (CHUNK, tq), 1)
    causal = {lo: functools.partial(lambda lo, blk: jnp.where(lo + row <= col, blk, NEG), lo)
              for lo in range(0, TSTEP, CHUNK)}

    def qtile(h, i):
        s_a, s_b = scratch[2 * h], scratch[2 * h + 1]
        rows = slice(h * MLA_V, (h + 1) * MLA_V)
        q = qT_ref[0, h, i]
        scores = lambda j: _dot(k_ref[0, h, j * TSTEP:(j + 1) * TSTEP, :], q)
        vtile = lambda j: (lambda r: vT_ref[0, h, j * nsub + r])
        init = (jnp.full((1, tq), M_INIT, F32), jnp.zeros((MLA_V + ONES, tq), F32))
        _, acc = yield from _sweep_steps(scores, vtile, causal, TSTEP, i, init, s_a, s_b)
        o = acc[0:MLA_V] / acc[MLA_V:MLA_V + 1]
        o_ref[0, i, rows, :] = (o * sz_ref[0, i, rows, :]).astype(BF16)

    once = jnp.minimum(pl.program_id(0), 0) + 1
    for i in range(nq):
        lax.fori_loop(0, once, lambda _, c, i=i: (_round_robin(qtile(h, i) for h in range(nh)), c)[1], 0)


def _mla_attn(qT, k, vT, sz):
    B, H, nq, _, tq = qT.shape
    S = k.shape[2]
    nh = MLA_HEADS_PER_STEP
    assert tq == TSTEP and TSTEP % TKV == 0 and H % nh == 0
    per_bh = lambda shape: pl.BlockSpec((1, nh) + shape, lambda b, h: (b, h) + (0,) * len(shape))
    gate = pl.BlockSpec((1, nq, nh * MLA_V, tq), lambda b, h: (b, 0, h, 0))
    return pl.pallas_call(
        _mla_attn_kernel,
        grid=(B, H // nh),
        in_specs=[per_bh((nq, MLA_HD, tq)), per_bh((S, MLA_HD)),
                  per_bh((S // TKV, MLA_V + ONES, TKV)), gate],
        out_specs=gate,
        out_shape=jax.ShapeDtypeStruct((B, nq, H * MLA_V, tq), BF16),
        scratch_shapes=[pltpu.VMEM((TSTEP, tq), F32)] * (2 * nh),
        compiler_params=_params(("parallel", "parallel")),
        name="mla_attention",
    )(qT, k, vT, sz)


def _out_proj_kernel(og_ref, wT_ref, x_ref, o_ref):
    ntile = og_ref.shape[1]
    og = og_ref[0, 0] if ntile == 1 else jnp.concatenate([og_ref[0, r] for r in range(ntile)], axis=1)
    yT = _dot(wT_ref[...], og)
    o_ref[0] = x_ref[0] + yT.T


def _out_proj(og, w, x):
    B, S, D = x.shape
    _, _, K, W = og.shape
    r = TO // W
    wT = w["woT"]
    return pl.pallas_call(
        _out_proj_kernel,
        grid=(B, S // TO),
        in_specs=[
            pl.BlockSpec((1, r, K, W), lambda b, t: (b, t, 0, 0)),
            w.spec("woT"),
            pl.BlockSpec((1, TO, D), lambda b, t: (b, t, 0)),
        ],
        out_specs=pl.BlockSpec((1, TO, D), lambda b, t: (b, t, 0)),
        out_shape=jax.ShapeDtypeStruct((B, S, D), F32),
        compiler_params=_params(("parallel", "parallel")),
        name="out_proj",
    )(og, wT, x)


def _nsa_in_kernel(x_ref, ng_ref, wtok_ref, wfT_ref, gq_ref, gk_ref, ct_ref, st_ref,
                   cosT_ref, sinT_ref,
                   qT_ref, ks_ref, kw_ref, kc_ref, vc_ref, vsT_ref, vwT_ref, g_ref, sz_ref, cv_scr):
    t = pl.program_id(1)
    x = x_ref[0]
    n = x.shape[0]
    tq = qT_ref.shape[-1]
    h = _rms_rows(x, ng_ref[...]).astype(BF16)
    G = NSA_GROUPS
    lanes = [slice(u * tq, (u + 1) * tq) for u in range(n // tq)]

    fa = _dot_nt(wfT_ref[...], h)
    cos = cosT_ref[...]
    sin = sinT_ref[...]
    gq = gq_ref[...]
    for hd in range(NSA_HEADS):
        blk = fa[hd * NSA_HD:(hd + 1) * NSA_HD]
        ss = jnp.sum(blk * blk, axis=0, keepdims=True)
        qn = blk * lax.rsqrt(ss * (1.0 / NSA_DK) + RMS_EPS) * gq
        x1 = qn[0:16]
        x2 = qn[64:80]
        r1 = (x1 * cos - x2 * sin).astype(BF16)
        r2 = (x2 * cos + x1 * sin).astype(BF16)
        qb = qn.astype(BF16)
        for u, sl in enumerate(lanes):
            qT_ref[0, hd, u, 0:16, :] = r1[:, sl]
            qT_ref[0, hd, u, 16:64, :] = qb[16:64, sl]
            qT_ref[0, hd, u, 64:80, :] = r2[:, sl]
            qT_ref[0, hd, u, 80:128, :] = qb[80:128, sl]
    off = NSA_HEADS * NSA_HD
    ones = jnp.ones((ONES, n), BF16)
    for ref in (vsT_ref, vwT_ref):
        for g in range(G):
            ref[0, g, 0, 0:NSA_DV, :] = fa[off + g * NSA_DV:off + (g + 1) * NSA_DV].astype(BF16)
            ref[0, g, 0, NSA_DV:NSA_DV + ONES, :] = ones
        off += G * NSA_DV
    sg = _sigmoid(fa[off:off + 16 * G])
    for g in range(G):
        for u, sl in enumerate(lanes):
            g_ref[0, g, u] = sg[g * 16:(g + 1) * 16, sl]
    off += 16 * G
    z = fa[off:off + NSA_HEADS * NSA_DV]
    sz = z * _sigmoid(z)
    for u, sl in enumerate(lanes):
        sz_ref[0, u] = sz[:, sl]

    pt = _dot(h, wtok_ref[...])
    ct = ct_ref[...]
    st = st_ref[...]
    nblk = ks_ref.shape[-1] - NSA_HD
    row = lax.broadcasted_iota(jnp.int32, (n, nblk), 0) + t * n
    col = lax.broadcasted_iota(jnp.int32, (n, nblk), 1)
    onehot = jnp.where((row // SEL_LEN) == col, 1.0, 0.0).astype(BF16)
    for br, ref in ((0, ks_ref), (1, kw_ref)):
        gk = gk_ref[br + 1:br + 2, :]
        for g in range(G):
            kb = pt[:, (br * G + g) * NSA_HD:(br * G + g + 1) * NSA_HD]
            ss = jnp.sum(kb * kb, axis=-1, keepdims=True)
            kn = kb * lax.rsqrt(ss * (1.0 / NSA_DK) + RMS_EPS) * gk
            kr = kn * ct + pltpu.roll(kn, 64, axis=1) * st
            ref[0, g, :, 0:NSA_HD] = kr.astype(BF16)
            if br == 0:
                ref[0, g, :, NSA_HD:NSA_HD + nblk] = onehot
    for br, ref in ((2, kc_ref), (3, vc_ref)):
        for g in range(G):
            stage = cv_scr.at[(br - 2) * G + g]
            stage[...] = pt[:, (br * G + g) * LANE:(br * G + g + 1) * LANE]
            for l in range(CMP_STRIDE):
                ref[0, g, :, l * LANE:(l + 1) * LANE] = stage[pl.ds(l, n // CMP_STRIDE, stride=CMP_STRIDE), :]


def _nsa_in(x, ng, w, tabs):
    B, S, D = x.shape
    nt = S // TT
    G = NSA_GROUPS
    nblk = S // SEL_LEN
    r = TT // TQ_NSA
    nq = S // TQ_NSA
    assert TT == TKV and TT % TQ_NSA == 0
    full = lambda shape: pl.BlockSpec(shape, lambda b, t: (0,) * len(shape))
    tok = lambda b, t: (b * nt + t, 0)
    feat = lambda b, t: (0, b * nt + t)
    tokmaj = lambda width: pl.BlockSpec((1, G, TT, width), lambda b, t: (b, 0, t, 0))
    vtile = pl.BlockSpec((1, G, 1, NSA_DV + ONES, TT), lambda b, t: (b, 0, t, 0, 0))
    cmp_in = pl.BlockSpec((1, G, TT // CMP_STRIDE, CMP_STRIDE * LANE), lambda b, t: (b, 0, t, 0))
    vshape = jax.ShapeDtypeStruct((B, G, nt, NSA_DV + ONES, TT), BF16)
    return pl.pallas_call(
        _nsa_in_kernel,
        grid=(B, nt),
        in_specs=[
            pl.BlockSpec((1, TT, D), lambda b, t: (b, t, 0)),
            full((1, D)),
            w.spec("wtok"), w.spec("wfT"), w.spec("gq"), w.spec("gk"),
            pl.BlockSpec((TT, LANE), tok), pl.BlockSpec((TT, LANE), tok),
            pl.BlockSpec((16, TT), feat), pl.BlockSpec((16, TT), feat),
        ],
        out_specs=[
            pl.BlockSpec((1, NSA_HEADS, r, NSA_HD, TQ_NSA), lambda b, t: (b, 0, t, 0, 0)),
            tokmaj(NSA_HD + nblk), tokmaj(NSA_HD), cmp_in, cmp_in,
            vtile, vtile,
            pl.BlockSpec((1, G, r, 16, TQ_NSA), lambda b, t: (b, 0, t, 0, 0)),
            pl.BlockSpec((1, r, NSA_HEADS * NSA_DV, TQ_NSA), lambda b, t: (b, t, 0, 0)),
        ],
        out_shape=[
            jax.ShapeDtypeStruct((B, NSA_HEADS, nq, NSA_HD, TQ_NSA), BF16),
            jax.ShapeDtypeStruct((B, G, S, NSA_HD + nblk), BF16),
            jax.ShapeDtypeStruct((B, G, S, NSA_HD), BF16),
            jax.ShapeDtypeStruct((B, G, S // CMP_STRIDE, CMP_STRIDE * LANE), F32),
            jax.ShapeDtypeStruct((B, G, S // CMP_STRIDE, CMP_STRIDE * LANE), F32),
            vshape, vshape,
            jax.ShapeDtypeStruct((B, G, nq, 16, TQ_NSA), F32),
            jax.ShapeDtypeStruct((B, nq, NSA_HEADS * NSA_DV, TQ_NSA), F32),
        ],
        scratch_shapes=[pltpu.VMEM((2 * G, TT, LANE), F32)],
        compiler_params=_params(("parallel", "parallel")),
        name="nsa_in_proj",
    )(x, ng, w["wtok"], w["wfT"], w["gq"], w["gk"], tabs["ct_nsa"], tabs["st_nsa"],
      tabs["cosT_nsa"], tabs["sinT_nsa"])


def _nsa_cmp_kernel(kc_ref, vc_ref, pek_ref, pev_ref, w1k_ref, w2k_ref, w1v_ref, w2vT_ref,
                    gk_ref, ct_ref, st_ref, kcmp_ref, vcmpT_ref):
    nrow = kc_ref.shape[2]

    def pre(x, pe_ref, w1_ref):
        lo = _dot((x + pe_ref[0:1, :]).astype(BF16), w1_ref[0])
        hi = _dot((x + pe_ref[1:2, :]).astype(BF16), w1_ref[1])
        return lo + pltpu.roll(hi, nrow - 1, axis=0)

    a = pre(kc_ref[0, 0], pek_ref, w1k_ref)
    kc = _dot((a * _sigmoid(a)).astype(BF16), w2k_ref[...])
    ss = jnp.sum(kc * kc, axis=-1, keepdims=True)
    kn = kc * lax.rsqrt(ss * (1.0 / NSA_DK) + RMS_EPS) * gk_ref[0:1, :]
    kr = kn * ct_ref[0] + pltpu.roll(kn, 64, axis=1) * st_ref[0]
    kcmp_ref[0, 0] = kr.astype(BF16)

    a = pre(vc_ref[0, 0], pev_ref, w1v_ref)
    sv = (a * _sigmoid(a)).astype(BF16)
    vcmpT_ref[0, 0, 0:NSA_DV, :] = _dot_nt(w2vT_ref[...], sv).astype(BF16)
    vcmpT_ref[0, 0, NSA_DV:NSA_DV + ONES, :] = jnp.ones((ONES, nrow), BF16)


def _nsa_cmp(kc16, vc16, w, tabs):
    B, G, nrow, width = kc16.shape
    blk = pl.BlockSpec((1, 1, nrow, width), lambda b, g: (b, g, 0, 0))
    tab = pl.BlockSpec((1, nrow, LANE), lambda b, g: (b, 0, 0))
    return pl.pallas_call(
        _nsa_cmp_kernel,
        grid=(B, G),
        in_specs=[blk, blk, w.spec("pek"), w.spec("pev"), w.spec("w1k"), w.spec("w2k"),
                  w.spec("w1v"), w.spec("w2vT"), w.spec("gk"), tab, tab],
        out_specs=[pl.BlockSpec((1, 1, nrow, NSA_HD), lambda b, g: (b, g, 0, 0)),
                   pl.BlockSpec((1, 1, NSA_DV + ONES, nrow), lambda b, g: (b, g, 0, 0))],
        out_shape=[jax.ShapeDtypeStruct((B, G, nrow, NSA_HD), BF16),
                   jax.ShapeDtypeStruct((B, G, NSA_DV + ONES, nrow), BF16)],
        compiler_params=_params(("parallel", "parallel")),
        name="nsa_compress",
    )(kc16, vc16, w["pek"], w["pev"], w["w1k"], w["w2k"], w["w1v"], w["w2vT"], w["gk"],
      tabs["ct_cmp"], tabs["st_cmp"])


def _tile4(a):
    return jnp.concatenate([a] * NSA_HPG, axis=1)


def _cmp_exp(s_ref, bias):
    n, w = s_ref.shape
    fold = lambda a, op: op(a.reshape(a.shape[0] // SUB, SUB, w), axis=0)
    rows = [slice(lo, lo + CHUNK) for lo in range(0, n, CHUNK)]
    mx = None
    for sl in rows:
        blk = bias[sl.start](s_ref[sl, :])
        s_ref[sl, :] = blk
        mx = fold(blk, jnp.max) if mx is None else jnp.maximum(mx, fold(blk, jnp.max))
    mc = jnp.maximum(jnp.max(mx, axis=0, keepdims=True), 0.5 * NEG)
    return jnp.concatenate([jnp.exp2(s_ref[sl, :] - mc).astype(BF16) for sl in rows], axis=0)


def _nsa_pre_kernel(qT_ref, kw_ref, vwT_ref, kc_ref, vcT_ref, ovT_ref, g_ref,
                    csel_ref, ocw_ref, *scratch, k_sel):
    nq, _, tq = qT_ref.shape[2:]
    ncmp = kc_ref.shape[2]
    nblk = ovT_ref.shape[0]

    wk = WINDOW + tq
    rloc = lax.broadcasted_iota(jnp.int32, (CHUNK, tq), 0)
    cloc = lax.broadcasted_iota(jnp.int32, (CHUNK, tq), 1)
    mask = lambda ok, lo: (lambda blk: blk + _tile4(jnp.where(ok(lo), 0.0, NEG)))

    def one_tile(t, u, clamped, jlim):
        q0 = t * tq
        qT = jnp.concatenate([qT_ref[0, hh, t] for hh in range(NSA_HPG)], axis=1)
        tok = q0 + lax.broadcasted_iota(jnp.int32, (1, tq), 1)

        sc_scr, sw_scr = scratch[2 * u], scratch[2 * u + 1]
        sc_scr[...] = _dot(kc_ref[0, 0], qT)
        yield
        cvalid = lambda lo: ((lo + rloc) * CMP_STRIDE + (CMP_LEN - 1)) <= tok
        cbias = {lo: mask(cvalid, lo) for lo in range(0, ncmp, CHUNK)}
        ec = _cmp_exp(sc_scr, cbias)
        ac = _dot(vcT_ref[0, 0], ec)
        dc = ac[NSA_DV:NSA_DV + 1]
        inv = 1.0 / jnp.where(dc > 0, dc, 1.0)
        o_c = ac[0:NSA_DV] * inv

        if clamped:
            start, off = 0, 0
            wbias = {lo: mask(lambda lo: lo + rloc <= tok, lo) for lo in range(0, wk, CHUNK)}
        else:
            start, off = q0 - WINDOW, (u % (TKV // tq)) * tq
            wbias = {lo: mask(lambda lo: lo + rloc > cloc, lo) for lo in range(0, tq, CHUNK)}
            wbias.update({lo: mask(lambda lo: lo - WINDOW + rloc <= cloc, lo)
                          for lo in range(WINDOW, wk, CHUNK)})
        sw_scr[...] = _dot(kw_ref[0, 0, pl.ds(pl.multiple_of(start, tq), wk), :], qT)
        yield
        w0 = start // TKV
        segs, pos, tile = [], 0, 0
        while pos < wk:
            width = min(TKV - off, wk - pos)
            segs.append((pos, pos + width, functools.partial(
                lambda tile, off, width: vwT_ref[0, 0, w0 + tile, :, off:off + width], tile, off, width)))
            pos, tile, off = pos + width, tile + 1, 0
        init = (jnp.full((1, NSA_HPG * tq), M_INIT, F32),
                jnp.zeros((NSA_DV + ONES, NSA_HPG * tq), F32))
        _, aw = _softmax_step_ref(sw_scr, init, segs=segs, bias=wbias)
        o_w = aw[0:NSA_DV] * (1.0 / aw[NSA_DV:NSA_DV + 1])
        yield

        if jlim:
            iw = _dot(ovT_ref[0:jlim, :], ec) * inv
            imp = iw[:, 0:tq]
            for hh in range(1, NSA_HPG):
                imp = imp + iw[:, hh * tq:(hh + 1) * tq]
            bidx = lax.broadcasted_iota(jnp.int32, (jlim, tq), 0)
            cur = tok // SEL_LEN
            forced = (bidx == 0) | (bidx == cur) | (bidx == cur - 1)
            valid = (bidx * SEL_LEN) <= tok
            score = jnp.where(forced, SEL_FORCE, jnp.where(valid, imp, -1.0))
            grp = [score[r * SUB:(r + 1) * SUB] for r in range(jlim // SUB)]
            bsub = lax.broadcasted_iota(jnp.int32, (SUB, tq), 0)
            cnt = [jnp.zeros((SUB, tq), F32) for _ in grp]
            for j in range(jlim):
                rowj = score[j:j + 1, :]
                for r in range(jlim // SUB):
                    if r * SUB > j:
                        cnt[r] = cnt[r] + jnp.where(rowj >= grp[r], 1.0, 0.0)
                    elif r * SUB + SUB - 1 < j:
                        cnt[r] = cnt[r] + jnp.where(rowj > grp[r], 1.0, 0.0)
                    else:
                        tie = jnp.where(bsub + r * SUB > j, 1.0, 0.0)
                        cnt[r] = cnt[r] + jnp.where(rowj > grp[r], 1.0, 0.0)
                        cnt[r] = cnt[r] + jnp.where(rowj == grp[r], tie, 0.0)
            cnt = jnp.concatenate(cnt, axis=0)
            csel_ref[0, 0, t, 0:jlim, :] = jnp.where(cnt < k_sel, 0.0, NEG).astype(BF16)
        if jlim < nblk:
            csel_ref[0, 0, t, jlim:nblk, :] = jnp.zeros((nblk - jlim, tq), BF16)

        for hh in range(NSA_HPG):
            sl = slice(hh * tq, (hh + 1) * tq)
            g0 = g_ref[0, 0, t, hh:hh + 1, :]
            g2 = g_ref[0, 0, t, 2 * NSA_HPG + hh:2 * NSA_HPG + hh + 1, :]
            ocw_ref[0, 0, t, hh * NSA_DV:(hh + 1) * NSA_DV, :] = g0 * o_c[:, sl] + g2 * o_w[:, sl]

    def make_trip(clamped, jlim):
        def trip(i, _):
            _round_robin(one_tile(i * PRE_TILES + u, u, clamped, jlim) for u in range(PRE_TILES))
            return 0
        return trip

    step = tq * PRE_TILES
    n_trips = nq // PRE_TILES
    n_clamped = min(WINDOW // step, n_trips)
    n_norank = min(max((k_sel * SEL_LEN) // step, n_clamped), n_trips)
    cuts = sorted({0, n_clamped, n_norank, n_trips}
                  | {n_norank + (k * (n_trips - n_norank)) // RANK_CLASSES for k in range(RANK_CLASSES)})
    for lo, hi in zip(cuts[:-1], cuts[1:]):
        causal_blocks = -(-(hi * step) // SEL_LEN)
        jlim = 0 if hi <= n_norank else min(nblk, -(-causal_blocks // SUB) * SUB)
        lax.fori_loop(lo, hi, make_trip(lo < n_clamped, jlim), 0)


def _nsa_pre(qT, kw, vwT, kcmp, vcmpT, ovT, gates):
    B, H, nq, _, tq = qT.shape
    G = NSA_GROUPS
    S = kw.shape[2]
    nblk = S // SEL_LEN
    ncmp = kcmp.shape[2]
    assert TKV % tq == 0 and nq % PRE_TILES == 0 and PRE_TILES % (TKV // tq) == 0
    assert WINDOW % (tq * PRE_TILES) == 0 and WINDOW % TKV == 0 and S >= WINDOW + tq
    kern = functools.partial(_nsa_pre_kernel, k_sel=min(N_SELECT, nblk))
    per_bg = lambda shape: pl.BlockSpec((1, 1) + shape, lambda b, g: (b, g) + (0,) * len(shape))
    wide = NSA_HPG * tq
    return pl.pallas_call(
        kern,
        grid=(B, G),
        in_specs=[
            pl.BlockSpec((1, NSA_HPG, nq, NSA_HD, tq), lambda b, g: (b, g, 0, 0, 0)),
            per_bg((S, NSA_HD)),
            per_bg((S // TKV, NSA_DV + ONES, TKV)),
            per_bg((ncmp, NSA_HD)),
            per_bg((NSA_DV + ONES, ncmp)),
            pl.BlockSpec((nblk, ncmp), lambda b, g: (0, 0)),
            per_bg((nq, 16, tq)),
        ],
        out_specs=[per_bg((nq, nblk, tq)), per_bg((nq, NSA_HPG * NSA_DV, tq))],
        out_shape=[jax.ShapeDtypeStruct((B, G, nq, nblk, tq), BF16),
                   jax.ShapeDtypeStruct((B, G, nq, NSA_HPG * NSA_DV, tq), F32)],
        scratch_shapes=[pltpu.VMEM((ncmp, wide), F32), pltpu.VMEM((WINDOW + tq, wide), F32)] * PRE_TILES,
        compiler_params=_params(("parallel", "parallel")),
        name="nsa_branches",
    )(qT, kw, vwT, kcmp, vcmpT, ovT, gates)


def _nsa_sel_kernel(qT_ref, csel_ref, ks_ref, vsT_ref, ocw_ref, g_ref, sz_ref, o_ref,
                    *scratch):
    nq, _, tq = qT_ref.shape[2:]
    nblk = csel_ref.shape[3]
    nsub = TSTEP // TKV
    per = TQ_SEL // tq
    wide = NSA_HPG * TQ_SEL

    row = lax.broadcasted_iota(jnp.int32, (CHUNK, TQ_SEL), 0)
    col = lax.broadcasted_iota(jnp.int32, (CHUNK, TQ_SEL), 1)

    def qtile(sup, p):
        qaug_ref, s_a, s_b = scratch[3 * p:3 * p + 3]
        i = sup * (TSTEP // TQ_SEL) + p
        for hh in range(NSA_HPG):
            for u in range(per):
                lanes = slice(hh * TQ_SEL + u * tq, hh * TQ_SEL + (u + 1) * tq)
                qaug_ref[0:NSA_HD, lanes] = qT_ref[0, hh, i * per + u]
                qaug_ref[NSA_HD:NSA_HD + nblk, lanes] = csel_ref[0, 0, i * per + u]
        qaug = qaug_ref[...]

        def scores(j):
            kt = ks_ref[0, 0, j * TSTEP:(j + 1) * TSTEP, :]
            return _dot(kt, qaug)

        def vtile(j):
            return lambda r: vsT_ref[0, 0, j * nsub + r]

        own = p * TQ_SEL
        causal = {own + lo: functools.partial(
            lambda lo, blk: blk + _tile4(jnp.where(lo + row <= col, 0.0, NEG)), lo)
            for lo in range(0, TQ_SEL, CHUNK)}
        init = (jnp.full((1, wide), M_INIT, F32), jnp.zeros((NSA_DV + ONES, wide), F32))
        _, acc = yield from _sweep_steps(scores, vtile, causal, own + TQ_SEL, sup, init, s_a, s_b)
        o_s = acc[0:NSA_DV] * (1.0 / acc[NSA_DV:NSA_DV + 1])

        for hh in range(NSA_HPG):
            rows = slice(hh * NSA_DV, (hh + 1) * NSA_DV)
            for u in range(per):
                t = i * per + u
                lanes = slice(hh * TQ_SEL + u * tq, hh * TQ_SEL + (u + 1) * tq)
                g1 = g_ref[0, 0, t, NSA_HPG + hh:NSA_HPG + hh + 1, :]
                o = ocw_ref[0, 0, t, rows, :] + g1 * o_s[:, lanes]
                o_ref[0, t, rows, :] = (o * sz_ref[0, t, rows, :]).astype(BF16)

    once = jnp.minimum(pl.program_id(0), 0) + 1
    for sup in range((nq * tq) // TSTEP):
        lax.fori_loop(0, once, lambda _, c, sup=sup: (
            _round_robin(qtile(sup, p) for p in range(TSTEP // TQ_SEL)), c)[1], 0)


def _nsa_sel(qT, csel, ks, vsT, ocw, gates, sz):
    B, H, nq, _, tq = qT.shape
    G = NSA_GROUPS
    S = ks.shape[2]
    nblk = S // SEL_LEN
    assert TQ_SEL % tq == 0 and TSTEP % TQ_SEL == 0 and TSTEP % TKV == 0 and S % TSTEP == 0
    per_bg = lambda shape: pl.BlockSpec((1, 1) + shape, lambda b, g: (b, g) + (0,) * len(shape))
    gate = pl.BlockSpec((1, nq, NSA_HPG * NSA_DV, tq), lambda b, g: (b, 0, g, 0))
    return pl.pallas_call(
        _nsa_sel_kernel,
        grid=(B, G),
        in_specs=[
            pl.BlockSpec((1, NSA_HPG, nq, NSA_HD, tq), lambda b, g: (b, g, 0, 0, 0)),
            per_bg((nq, nblk, tq)),
            per_bg((S, NSA_HD + nblk)),
            per_bg((S // TKV, NSA_DV + ONES, TKV)),
            per_bg((nq, NSA_HPG * NSA_DV, tq)),
            per_bg((nq, 16, tq)),
            gate,
        ],
        out_specs=gate,
        out_shape=jax.ShapeDtypeStruct((B, nq, H * NSA_DV, tq), BF16),
        scratch_shapes=([pltpu.VMEM((NSA_HD + nblk, NSA_HPG * TQ_SEL), BF16)]
                        + [pltpu.VMEM((TSTEP, NSA_HPG * TQ_SEL), F32)] * 2) * (TSTEP // TQ_SEL),
        compiler_params=_params(("parallel", "parallel")),
        name="nsa_selected",
    )(qT, csel, ks, vsT, ocw, gates, sz)


def _nsa_perm():
    src = np.full((NSA_HD,), -1, np.int64)
    src[0:12] = np.arange(0, 12)
    src[12:16] = np.arange(24, 28)
    src[16:64] = np.arange(28, 76)
    src[64:76] = np.arange(12, 24)
    src[76:80] = np.arange(76, 80)
    src[80:96] = np.arange(80, 96)
    return src


def _take_cols(w, src):
    idx = np.where(src >= 0, src, 0)
    out = jnp.take(w, jnp.asarray(idx), axis=-1)
    return jnp.where(jnp.asarray(src >= 0), out, 0.0)


def _pad_last(w, width):
    return jnp.pad(w, [(0, 0)] * (w.ndim - 1) + [(0, width - w.shape[-1])])


def _prep_mla(w_in, g_cq, w_uq, g_ckv, w_ukv, g_q, g_k, w_out):
    D = w_in.shape[0]
    o1 = MLA_Q_LORA + MLA_KV_LORA
    z32 = jnp.zeros((D, 32), F32)
    wa = jnp.concatenate([w_in[:, :o1], w_in[:, o1:o1 + 32], z32, w_in[:, o1 + 32:o1 + 64], z32], axis=1)
    wq = w_uq.reshape(MLA_Q_LORA, MLA_HEADS, MLA_QK)
    zq = jnp.zeros((MLA_Q_LORA, MLA_HEADS, 32), F32)
    wq = jnp.concatenate([wq[:, :, 64:], wq[:, :, :32], zq, wq[:, :, 32:64], zq], axis=2)
    z1 = jnp.zeros((32,), F32)
    gq = jnp.concatenate([g_q[64:], g_q[:32], z1, g_q[32:64], z1]) * C_MLA
    wkv = w_ukv.reshape(MLA_KV_LORA, MLA_HEADS, MLA_NOPE + MLA_V)
    return {
        "wa": wa.astype(BF16),
        "wzT": w_in[:, o1 + MLA_ROPE:].T.astype(BF16),
        "gcq": g_cq.reshape(1, -1), "gckv": g_ckv.reshape(1, -1),
        "wuqT": wq.reshape(MLA_Q_LORA, MLA_HEADS * MLA_HD).T.astype(BF16),
        "gq": jnp.broadcast_to(gq[:, None], (MLA_HD, TT)),
        "wuk": wkv[:, :, :MLA_NOPE].reshape(MLA_KV_LORA, -1).astype(BF16),
        "wuvT": wkv[:, :, MLA_NOPE:].reshape(MLA_KV_LORA, -1).T.astype(BF16),
        "gkn": g_k[64:].reshape(1, -1),
        "gkp": jnp.concatenate([g_k[:32], z1, g_k[32:64], z1]).reshape(1, -1),
        "woT": w_out.T.astype(BF16),
    }


def _prep_nsa(w_in, g_q, g_k, pe_k, w1_k, w2_k, pe_v, w1_v, w2_v, w_out):
    D = w_in.shape[0]
    G = NSA_GROUPS
    src = _nsa_perm()
    offs = np.concatenate([[0], np.cumsum(NSA_SIZES)])
    part = lambda i: w_in[:, offs[i]:offs[i + 1]]
    q, kc, vc, ks, vs, kw, vw, gl, z = [part(i) for i in range(9)]
    perm_heads = lambda w, nh: _take_cols(w.reshape(D, nh, NSA_DK), src).reshape(D, nh * NSA_HD)
    pad_groups = lambda w, d: _pad_last(w.reshape(D, G, d), LANE).reshape(D, G * LANE)
    wtok = jnp.concatenate([perm_heads(ks, G), perm_heads(kw, G), pad_groups(kc, NSA_DK),
                            pad_groups(vc, NSA_DV)], axis=1)
    glr = gl.reshape(D, G, NSA_HPG, 3).transpose(0, 1, 3, 2).reshape(D, G, 3 * NSA_HPG)
    glr = _pad_last(glr, 16).reshape(D, G * 16)
    wf = jnp.concatenate([perm_heads(q, NSA_HEADS), vs, vw, glr, z], axis=1)
    halves = lambda w, d: _pad_last(w.reshape(2, CMP_LEN // 2, d, -1).transpose(0, 1, 3, 2), LANE) \
        .transpose(0, 1, 3, 2).reshape(2, (CMP_LEN // 2) * LANE, -1)
    pe_flat = lambda pe: _pad_last(pe, LANE).reshape(2, (CMP_LEN // 2) * LANE)
    return {
        "wtok": wtok.astype(BF16),
        "wfT": wf.T.astype(BF16),
        "gq": jnp.broadcast_to((_take_cols(g_q, src) * C_NSA)[:, None], (NSA_HD, TT)),
        "gk": _take_cols(g_k, src),
        "pek": pe_flat(pe_k), "pev": pe_flat(pe_v),
        "w1k": _pad_last(halves(w1_k, NSA_DK), LANE).astype(BF16),
        "w2k": _pad_last(_take_cols(w2_k, src).T, LANE).T.astype(BF16),
        "w1v": _pad_last(halves(w1_v, NSA_DV), LANE).astype(BF16),
        "w2vT": _pad_last(w2_v.T, LANE).astype(BF16),
        "woT": w_out.T.astype(BF16),
    }


def _overlap_T(S):
    ncmp = S // CMP_STRIDE
    nblk = S // SEL_LEN
    n_cmp = (S - CMP_LEN) // CMP_STRIDE + 1
    cs = np.arange(ncmp) * CMP_STRIDE
    ss = np.arange(nblk) * SEL_LEN
    ov = (cs[None, :] < ss[:, None] + SEL_LEN) & (cs[None, :] + CMP_LEN > ss[:, None])
    ov = ov & (np.arange(ncmp)[None, :] < n_cmp)
    return jnp.asarray(ov.astype(np.float32)).astype(BF16)


def _mla_layer(x, ng, w, tabs):
    qT, k, vT, sz = _mla_in(x, ng.reshape(1, -1), w, tabs)
    og = _mla_attn(qT, k, vT, sz)
    return _out_proj(og, w, x)


def _nsa_layer(x, ng, w, tabs, ovT):
    B, S, _ = x.shape
    qT, ks, kw, kc16, vc16, vsT, vwT, gates, sz = _nsa_in(x, ng.reshape(1, -1), w, tabs)
    kcmp, vcmpT = _nsa_cmp(kc16, vc16, w, tabs)
    csel, ocw = _nsa_pre(qT, kw, vwT, kcmp, vcmpT, ovT, gates)
    og = _nsa_sel(qT, csel, ks, vsT, ocw, gates, sz)
    return _out_proj(og, w, x)


def kernel(x, positions, norm_g, mla_w_in, mla_g_cq, mla_w_uq, mla_g_ckv, mla_w_ukv, mla_g_q, mla_g_k, mla_w_out, nsa_w_in, nsa_g_q, nsa_g_k, nsa_pe_k, nsa_w1_k, nsa_w2_k, nsa_pe_v, nsa_w1_v, nsa_w2_v, nsa_w_out):
    B, S, _ = x.shape
    tabs = _rope_tables(positions)
    ovT = _overlap_T(S)
    w_mla = jax.vmap(_prep_mla)(mla_w_in, mla_g_cq, mla_w_uq, mla_g_ckv, mla_w_ukv, mla_g_q,
                                mla_g_k, mla_w_out)
    w_nsa = jax.vmap(_prep_nsa)(nsa_w_in, nsa_g_q, nsa_g_k, nsa_pe_k, nsa_w1_k, nsa_w2_k,
                                nsa_pe_v, nsa_w1_v, nsa_w2_v, nsa_w_out)
    for i in range(DEPTH):
        j = i // N_MIXERS
        if i % N_MIXERS == 0:
            x = _mla_layer(x, norm_g[i], _LayerWeights(w_mla, j), tabs)
        else:
            x = _nsa_layer(x, norm_g[i], _LayerWeights(w_nsa, j), tabs, ovT)
    return x
```
